```python
import math
import jax, jax.numpy as jnp
from jax import lax
import numpy as np

D_MODEL = 1024
BATCH = 32
SEQ = 2048
DEPTH = 2

GRID_W = 64
CTX_LEN = 256
EPS = 1e-6
ROPE_THETA = 10000.0

A_HEADS = 4
A_DH = 64
A_W = A_HEADS * 2 * A_DH
Q_BLOCK = 128
B_GROUPS = 4
B_GW = 128
B_W = B_GROUPS * B_GW
POOL_WINDOWS = (2, 4, 8, 16)
C_HEADS = 8
C_DH = 64
C_W = C_HEADS * C_DH
NA_ROWS = 8
NA_COLS = 16
D_GROUPS = 4
D_GW = 128
D_W = D_GROUPS * D_GW
CHUNK = 128
N_BRANCH = 4
FFN_HIDDEN = ((8 * D_MODEL // 3 + 255) // 256) * 256

OFF_AQ = 0
OFF_AK = OFF_AQ + A_W
OFF_AV = OFF_AK + A_W
OFF_B = OFF_AV + A_W
OFF_CQ = OFF_B + B_W
OFF_CK = OFF_CQ + C_W
OFF_CV = OFF_CK + C_W
OFF_DU = OFF_CV + C_W
OFF_DV = OFF_DU + D_W
OFF_G = OFF_DV + D_W
IN_COLS = OFF_G + N_BRANCH * D_MODEL
BRANCH_WIDTHS = (A_W, B_W, C_W, D_W)
MIX_W = A_W + B_W + C_W + D_W

kernel_name = 'hybrid_gated_branch_diffusion_block'


def _rmsnorm(t, g):
    tf = t.astype(jnp.float32)
    n = tf * lax.rsqrt(jnp.mean(tf * tf, axis=-1, keepdims=True) + EPS)
    return n.astype(t.dtype) * g


def _modulate(h, shift, scale):
    return h * (1 + scale) + shift


def _axial_rope(n_tok, dim, dtype):
    nf = dim // 4
    t = jnp.arange(n_tok)
    row = (t // GRID_W).astype(jnp.float32)
    col = (t % GRID_W).astype(jnp.float32)
    inv = ROPE_THETA ** (-jnp.arange(nf, dtype=jnp.float32) / nf)
    ar = row[:, None] * inv
    ac = col[:, None] * inv
    ang = jnp.concatenate([ar, ar, ac, ac], axis=-1)
    return jnp.cos(ang).astype(dtype), jnp.sin(ang).astype(dtype)


def _rope(t, cos, sin):
    a, b, cc, d = jnp.split(t, 4, axis=-1)
    rot = jnp.concatenate([-b, a, -d, cc], axis=-1)
    return t * cos + rot * sin


def _col_tables():
    ncb = GRID_W // NA_COLS
    qcol = np.arange(GRID_W).reshape(ncb, NA_COLS)
    band0 = np.clip(np.arange(ncb) * NA_COLS - NA_COLS // 2, 0, GRID_W - 2 * NA_COLS)
    band = band0[:, None] + np.arange(2 * NA_COLS)
    win0 = np.clip(qcol - NA_COLS // 2, 0, GRID_W - NA_COLS)
    kcol = band[:, None, :]
    valid = (kcol >= win0[..., None]) & (kcol < win0[..., None] + NA_COLS)
    dc_idx = np.clip(kcol - qcol[..., None] + NA_COLS - 1, 0, 2 * NA_COLS - 2)
    return band, valid, dc_idx


def _heads(t, h, d):
    return t.reshape(t.shape[0], t.shape[1], h, d).transpose(0, 2, 1, 3)


def _merge_heads(t):
    return t.transpose(0, 2, 1, 3).reshape(t.shape[0], t.shape[2], -1)


def _diff_heads(t, g):
    t = t.reshape(t.shape[0], t.shape[1], A_HEADS, 2, A_DH)
    return _rmsnorm(t, g).transpose(0, 2, 3, 1, 4)


def _diff_attend(q, k, v, lam):
    s = jnp.einsum('bhcqd,bhckd->bhcqk', q, k).astype(jnp.float32) * (A_DH ** -0.5)
    p = jax.nn.softmax(s, axis=-1)
    a = p[:, :, 0] - lam * p[:, :, 1]
    return jnp.einsum('bhqk,bhkv->bhqv', a.astype(v.dtype), v)


def _diff_attention_blocks(q, k, v, lam):
    b_, h, _, s, dh = q.shape
    nb = s // Q_BLOCK
    qb = q.reshape(b_, h, 2, nb, Q_BLOCK, dh).transpose(3, 0, 1, 2, 4, 5)
    out = lax.map(lambda qi: _diff_attend(qi, k, v, lam), qb)
    return out.transpose(1, 2, 0, 3, 4).reshape(b_, h, s, 2 * A_DH)


def _dense_attention(q, k, v):
    s = jnp.einsum('bqhd,bkhd->bhqk', q, k).astype(jnp.float32) * (q.shape[-1] ** -0.5)
    p = jax.nn.softmax(s, axis=-1).astype(v.dtype)
    o = jnp.einsum('bhqk,bkhd->bqhd', p, v)
    return o.reshape(o.shape[0], o.shape[1], -1)


def _neighbourhood_attention(q, k, v, k_ctx, v_ctx, rpb, col_tabs):
    band, valid, dc_idx = col_tabs
    b_, s, h, dh = q.shape
    rows = s // GRID_W
    wr = min(NA_ROWS, rows)
    ncb = GRID_W // NA_COLS
    nk = 2 * NA_COLS
    scale = dh ** -0.5
    qr = q.reshape(b_, rows, ncb, NA_COLS, h, dh).transpose(1, 0, 4, 2, 3, 5)

    def grid_band(t):
        t = t.reshape(b_, rows, GRID_W, h, dh).transpose(0, 3, 1, 2, 4)
        return t[:, :, :, band]

    kb_all = grid_band(k)
    vb_all = grid_band(v)
    kc = k_ctx.transpose(0, 2, 1, 3)
    vc = v_ctx.transpose(0, 2, 1, 3)
    col_bias = rpb[:, :, dc_idx]
    n_loc = wr * nk

    def row_step(args):
        r, qrow = args
        rs = jnp.clip(r - wr // 2, 0, rows - wr)
        kb = lax.dynamic_slice_in_dim(kb_all, rs, wr, axis=2)
        vb = lax.dynamic_slice_in_dim(vb_all, rs, wr, axis=2)
        ridx = rs + jnp.arange(wr) - r + NA_ROWS - 1
        bias = jnp.take(col_bias, ridx, axis=1).transpose(0, 2, 3, 1, 4)
        s_loc = jnp.einsum('bhjqd,bhrjkd->bhjqrk', qrow, kb).astype(jnp.float32) * scale
        s_loc = jnp.where(valid[:, :, None, :], s_loc + bias.astype(jnp.float32), -jnp.inf)
        s_ctx = jnp.einsum('bhjqd,bhkd->bhjqk', qrow, kc).astype(jnp.float32) * scale
        sc = jnp.concatenate([s_loc.reshape(b_, h, ncb, NA_COLS, n_loc), s_ctx], axis=-1)
        p = jax.nn.softmax(sc, axis=-1).astype(v.dtype)
        p_loc = p[..., :n_loc].reshape(b_, h, ncb, NA_COLS, wr, nk)
        return (jnp.einsum('bhjqrk,bhrjkd->bhjqd', p_loc, vb)
                + jnp.einsum('bhjqk,bhkd->bhjqd', p[..., n_loc:], vc))

    out = lax.map(row_step, (jnp.arange(rows), qr))
    return out.transpose(1, 0, 3, 4, 2, 5).reshape(b_, s, h * dh)


def _pool_mixer(p, w_pool, s_pool):
    b_, l_, _ = p.shape
    pg = p.reshape(b_, l_, B_GROUPS, B_GW)
    cs = jnp.cumsum(pg.astype(jnp.float32), axis=1)
    cs = jnp.concatenate([jnp.zeros_like(cs[:, :1]), cs], axis=1)
    t = jnp.arange(l_)
    means = []
    for g, w in enumerate(POOL_WINDOWS):
        lo = jnp.clip(t - w // 2, 0, l_)
        hi = jnp.clip(t + w // 2, 0, l_)
        cnt = (hi - lo).astype(jnp.float32)[None, :, None]
        means.append((cs[:, hi, g] - cs[:, lo, g]) / cnt)
    pooled = jnp.stack(means, axis=2).astype(p.dtype) - pg
    y = jnp.einsum('blgc,gcd->blgd', pooled, w_pool)
    return y.reshape(b_, l_, B_W) * s_pool


def _spatial_gating(u, v, vn_g, w_s, b_s):
    b_, l_, _ = u.shape
    v = _rmsnorm(v, vn_g).reshape(b_, l_ // CHUNK, CHUNK, D_GROUPS, D_GW)
    sv = jnp.einsum('gpq,bnqgc->bnpgc', w_s, v) + b_s.T[:, :, None]
    return u * sv.reshape(b_, l_, D_W)


def _merge_branches(ys, gate_pre, w_branch, w_out):
    d = w_out.shape[0]
    acc = None
    off = 0
    for i, (y, w) in enumerate(zip(ys, BRANCH_WIDTHS)):
        term = jax.nn.sigmoid(gate_pre[..., i * d:(i + 1) * d]) * (y @ w_branch[off:off + w])
        acc = term if acc is None else acc + term
        off += w
    return acc @ w_out


def _swiglu(h, w_gu, w_down):
    a, b = jnp.split(h @ w_gu, 2, axis=-1)
    return (jax.nn.silu(a) * b) @ w_down


def setup_inputs(seed: int = 0) -> dict:
    key = jax.random.key(seed)
    ks = jax.random.split(key, 24)
    f32 = jnp.float32
    D = D_MODEL
    L = DEPTH

    def nrm(k, shape, scale):
        return jax.random.normal(k, shape, f32) * scale

    return {
        'x': nrm(ks[0], (BATCH, SEQ, D), 1.0),
        'c': nrm(ks[1], (BATCH, D), 1.0),
        'ctx': nrm(ks[2], (BATCH, CTX_LEN, D), 1.0),
        'c_ctx': nrm(ks[3], (D,), 1.0),
        'w_mod': nrm(ks[4], (L, D, 6 * D), 0.5 * D ** -0.5),
        'b_mod': nrm(ks[5], (L, 6 * D), 0.02),
        'norm1_g': 1.0 + nrm(ks[6], (L, D), 0.02),
        'w_in': nrm(ks[7], (L, D, IN_COLS), D ** -0.5),
        'a_qk_g': 1.0 + nrm(ks[8], (L, 2, A_DH), 0.02),
        'a_lambda': nrm(ks[9], (L, 4, A_DH), 0.1),
        'a_subln_g': 1.0 + nrm(ks[10], (L, 2 * A_DH), 0.02),
        'b_pool_w': nrm(ks[11], (L, B_GROUPS, B_GW, B_GW), B_GW ** -0.5),
        'b_pool_s': 1.0 + nrm(ks[12], (L, B_W), 0.1),
        'c_qk_g': 1.0 + nrm(ks[13], (L, 2, C_DH), 0.02),
        'c_rpb': nrm(ks[14], (L, C_HEADS, 2 * NA_ROWS - 1, 2 * NA_COLS - 1), 0.5),
        'd_vn_g': 1.0 + nrm(ks[15], (L, D_W), 0.02),
        'd_ws': nrm(ks[16], (L, D_GROUPS, CHUNK, CHUNK), CHUNK ** -0.5),
        'd_bs': 1.0 + nrm(ks[17], (L, D_GROUPS, CHUNK), 0.02),
        'w_branch': nrm(ks[18], (L, MIX_W, D), A_W ** -0.5),
        'w_out': nrm(ks[19], (L, D, D), D ** -0.5),
        'norm2_g': 1.0 + nrm(ks[20], (L, D), 0.02),
        'w_gu': nrm(ks[21], (L, D, 2 * FFN_HIDDEN), D ** -0.5),
        'w_down': nrm(ks[22], (L, FFN_HIDDEN, D), FFN_HIDDEN ** -0.5),
    }


def reference(x, c, ctx, c_ctx, w_mod, b_mod, norm1_g, w_in, a_qk_g, a_lambda, a_subln_g,
              b_pool_w, b_pool_s, c_qk_g, c_rpb, d_vn_g, d_ws, d_bs, w_branch, w_out,
              norm2_g, w_gu, w_down):
    s = x.shape[1]
    cos_a, sin_a = _axial_rope(s, A_DH, x.dtype)
    col_tabs = _col_tables()
    for l in range(DEPTH):
        last = l == DEPTH - 1
        lam_init = 0.8 - 0.6 * math.exp(-0.3 * l)
        lam = (jnp.exp(jnp.sum(a_lambda[l, 0] * a_lambda[l, 1]))
               - jnp.exp(jnp.sum(a_lambda[l, 2] * a_lambda[l, 3])) + lam_init)

        mod_x = jax.nn.silu(c) @ w_mod[l] + b_mod[l]
        mod_c = jax.nn.silu(c_ctx) @ w_mod[l] + b_mod[l]
        sh1, sc1, g1, sh2, sc2, g2 = jnp.split(mod_x[:, None, :], 6, axis=-1)
        sh1c, sc1c, g1c, sh2c, sc2c, g2c = jnp.split(mod_c, 6)

        hx = _modulate(_rmsnorm(x, norm1_g[l]), sh1, sc1)
        hc = _modulate(_rmsnorm(ctx, norm1_g[l]), sh1c, sc1c)
        px = hx @ w_in[l]
        if last:
            ccol = lambda off, w: hc @ w_in[l][:, off:off + w]
        else:
            pc = hc @ w_in[l]
            ccol = lambda off, w: pc[..., off:off + w]
        xcol = lambda off, w: px[..., off:off + w]

        ga_q, ga_k = a_qk_g[l, 0], a_qk_g[l, 1]
        aq = _rope(_diff_heads(xcol(OFF_AQ, A_W), ga_q), cos_a, sin_a)
        ak = _rope(_diff_heads(xcol(OFF_AK, A_W), ga_k), cos_a, sin_a)
        av = _heads(xcol(OFF_AV, A_W), A_HEADS, 2 * A_DH)
        ak_c = _diff_heads(ccol(OFF_AK, A_W), ga_k)
        av_c = _heads(ccol(OFF_AV, A_W), A_HEADS, 2 * A_DH)
        k_all = jnp.concatenate([ak, ak_c], axis=3)
        v_all = jnp.concatenate([av, av_c], axis=2)
        ya = _diff_attention_blocks(aq, k_all, v_all, lam)
        ya = _merge_heads(_rmsnorm(ya, a_subln_g[l]) * (1 - lam_init))

        yb = _pool_mixer(xcol(OFF_B, B_W), b_pool_w[l], b_pool_s[l])

        gc_q, gc_k = c_qk_g[l, 0], c_qk_g[l, 1]
        split_c = lambda t: t.reshape(t.shape[0], t.shape[1], C_HEADS, C_DH)
        cq = _rmsnorm(split_c(xcol(OFF_CQ, C_W)), gc_q)
        ck = _rmsnorm(split_c(xcol(OFF_CK, C_W)), gc_k)
        cv = split_c(xcol(OFF_CV, C_W))
        ck_c = _rmsnorm(split_c(ccol(OFF_CK, C_W)), gc_k)
        cv_c = split_c(ccol(OFF_CV, C_W))
        yc = _neighbourhood_attention(cq, ck, cv, ck_c, cv_c, c_rpb[l], col_tabs)

        yd = _spatial_gating(xcol(OFF_DU, D_W), xcol(OFF_DV, D_W), d_vn_g[l], d_ws[l], d_bs[l])

        mix_x = _merge_branches((ya, yb, yc, yd), xcol(OFF_G, N_BRANCH * D_MODEL),
                                w_branch[l], w_out[l])
        x = x + g1 * mix_x
        x = x + g2 * _swiglu(_modulate(_rmsnorm(x, norm2_g[l]), sh2, sc2), w_gu[l], w_down[l])

        if not last:
            aq_c = _diff_heads(ccol(OFF_AQ, A_W), ga_q)
            ya_c = _merge_heads(_rmsnorm(_diff_attend(aq_c, ak_c, av_c, lam), a_subln_g[l]) * (1 - lam_init))
            yb_c = _pool_mixer(ccol(OFF_B, B_W), b_pool_w[l], b_pool_s[l])
            cq_c = _rmsnorm(split_c(ccol(OFF_CQ, C_W)), gc_q)
            yc_c = _dense_attention(cq_c, ck_c, cv_c)
            yd_c = _spatial_gating(ccol(OFF_DU, D_W), ccol(OFF_DV, D_W), d_vn_g[l], d_ws[l], d_bs[l])
            mix_c = _merge_branches((ya_c, yb_c, yc_c, yd_c), ccol(OFF_G, N_BRANCH * D_MODEL),
                                    w_branch[l], w_out[l])
            ctx = ctx + g1c * mix_c
            ctx = ctx + g2c * _swiglu(_modulate(_rmsnorm(ctx, norm2_g[l]), sh2c, sc2c), w_gu[l], w_down[l])
    return x
```

```python
import functools
import math

import jax
import jax.numpy as jnp
import numpy as np
from jax import lax
from jax.experimental import pallas as pl
from jax.experimental.pallas import tpu as pltpu

F32 = jnp.float32
BF16 = jnp.bfloat16

D_MODEL = 1024
GRID_W = 64
EPS = 1e-6
ROPE_THETA = 10000.0
A_HEADS = 4
A_DH = 64
POOL_WINDOWS = (2, 4, 8, 16)
C_DH = 64
NA_ROWS = 8
NA_COLS = 16
CHUNK = 128
N_BRANCH = 4
SEG_W = 512
N_SEG = 9
MIX_COLS = N_SEG * SEG_W
FFN_HIDDEN = 2816
SEG_AQ, SEG_AK, SEG_AV, SEG_B, SEG_CQ, SEG_CK, SEG_CV, SEG_DU, SEG_DV = range(N_SEG)
SEG_DTYPES = (BF16, BF16, BF16, F32, BF16, BF16, BF16, F32, BF16)

V7X_LANES = 128
V7X_VMEM_BYTES = 64 * 1024 * 1024
V7X_VMEM_LIMIT = V7X_VMEM_BYTES - 8 * 1024 * 1024

ROW_TILE = 512
ATTN_Q_TILE = 256
POOL_TILE = 256
POOL_HALO = 8
FFN_CHUNK = 256


def _params(n_axes, vmem_bytes):
    return pltpu.CompilerParams(
        dimension_semantics=("parallel",) * n_axes,
        vmem_limit_bytes=int(min(vmem_bytes, V7X_VMEM_LIMIT)))


def _resident(shape):
    zeros = (0,) * len(shape)
    return pl.BlockSpec(shape, lambda *_: zeros, pipeline_mode=pl.Buffered(1))


def _dot(a, b):
    return jnp.dot(a, b, preferred_element_type=F32)


def _dot_nt(a, b):
    return lax.dot_general(a, b, (((1,), (1,)), ((), ())), preferred_element_type=F32)


def _dot_tn(a, b):
    return lax.dot_general(a, b, (((0,), (0,)), ((), ())), preferred_element_type=F32)


def _split_bf16(t):
    hi = t.astype(BF16)
    lo = (t - hi.astype(F32)).astype(BF16)
    return hi, lo


def _norm_modulate(x, gain, shift, scale):
    ms = jnp.mean(x * x, axis=-1, keepdims=True)
    n = x * lax.rsqrt(ms + EPS)
    return ((n * gain) * (1.0 + scale) + shift).astype(BF16)


def _mod_kernel(c_ref, w_ref, b_ref, o_ref):
    a = c_ref[...]
    a = a * jax.nn.sigmoid(a)
    a_hi, a_lo = _split_bf16(a)
    w_hi, w_lo = _split_bf16(w_ref[...])
    o_ref[...] = _dot(a_hi, w_hi) + _dot(a_hi, w_lo) + _dot(a_lo, w_hi) + b_ref[...]


def _modulation(c_all, w_mod, b_mod):
    depth, d, n = w_mod.shape
    rows = c_all.shape[0]
    tn = 768
    return pl.pallas_call(
        _mod_kernel,
        grid=(depth, n // tn),
        in_specs=[
            pl.BlockSpec((rows, d), lambda l, j: (0, 0)),
            pl.BlockSpec((None, d, tn), lambda l, j: (l, 0, j)),
            pl.BlockSpec((None, 1, tn), lambda l, j: (l, 0, j)),
        ],
        out_specs=pl.BlockSpec((None, rows, tn), lambda l, j: (l, 0, j)),
        out_shape=jax.ShapeDtypeStruct((depth, rows, n), F32),
        compiler_params=_params(2, 32 * 1024 * 1024),
        name="modulation",
    )(c_all, w_mod, b_mod.reshape(depth, 1, n))


def _group_rms(r, grp, gain):
    ms = _dot((r * r).astype(BF16), grp) * (1.0 / A_DH)
    return r * lax.rsqrt(ms + EPS) * gain


def _rope_block(t, cos, sin_up, sin_dn):
    return (t * cos + pltpu.roll(t, V7X_LANES - 16, 1) * sin_up + pltpu.roll(t, 16, 1) * sin_dn)


def _in_proj_kernel(*refs, rope):
    x_ref, sh_ref, sc_ref, ng_ref, w_ref, grp_ref, qkg_ref, vng_ref = refs[:8]
    pos = 8
    if rope:
        cos_ref, sup_ref, sdn_ref = refs[8:11]
        pos = 11
    outs = refs[pos:pos + N_SEG]
    hb = _norm_modulate(x_ref[...], ng_ref[...], sh_ref[...], sc_ref[...])
    qk_row = {SEG_AQ: 0, SEG_AK: 1, SEG_CQ: 2, SEG_CK: 3}
    for seg in range(N_SEG):
        r = _dot(hb, w_ref[:, seg * SEG_W:(seg + 1) * SEG_W])
        if seg in qk_row:
            row = qk_row[seg]
            r = _group_rms(r, grp_ref[...], qkg_ref[row:row + 1, :])
            if rope and seg in (SEG_AQ, SEG_AK):
                cos, sup, sdn = cos_ref[...], sup_ref[...], sdn_ref[...]
                r = jnp.concatenate(
                    [_rope_block(r[:, j * V7X_LANES:(j + 1) * V7X_LANES], cos, sup, sdn)
                     for j in range(SEG_W // V7X_LANES)], axis=1)
        elif seg == SEG_DV:
            ms = jnp.mean(r * r, axis=-1, keepdims=True)
            r = r * lax.rsqrt(ms + EPS) * vng_ref[...]
        outs[seg][...] = r.astype(outs[seg].dtype)


def _in_proj(x, shift, scale, norm_g, w_mix, grp, qk_gains, vn_g, rope_tabs):
    b, l, d = x.shape
    tm = min(ROW_TILE, l)
    batched = shift.shape[0] != 1
    mod_map = (lambda bi, i: (bi, 0, 0)) if batched else (lambda bi, i: (0, 0, 0))
    rope = rope_tabs is not None
    in_specs = [
        pl.BlockSpec((None, tm, d), lambda bi, i: (bi, i, 0)),
        pl.BlockSpec((None, 1, d), mod_map),
        pl.BlockSpec((None, 1, d), mod_map),
        _resident((1, d)),
        _resident((d, MIX_COLS)),
        _resident((SEG_W, SEG_W)),
        _resident((4, SEG_W)),
        _resident((1, SEG_W)),
    ]
    args = [x, shift, scale, norm_g, w_mix, grp, qk_gains, vn_g]
    if rope:
        in_specs += [pl.BlockSpec((tm, V7X_LANES), lambda bi, i: (i, 0))] * 3
        args += list(rope_tabs)
    out_bytes = sum(tm * SEG_W * jnp.dtype(t).itemsize for t in SEG_DTYPES)
    vmem = d * MIX_COLS * 2 + 2 * (tm * d * 4 + out_bytes) + 8 * tm * SEG_W * 4 + (8 << 20)
    return pl.pallas_call(
        functools.partial(_in_proj_kernel, rope=rope),
        grid=(b, l // tm),
        in_specs=in_specs,
        out_specs=[pl.BlockSpec((None, tm, SEG_W), lambda bi, i: (bi, i, 0))] * N_SEG,
        out_shape=[jax.ShapeDtypeStruct((b, l, SEG_W), t) for t in SEG_DTYPES],
        compiler_params=_params(2, vmem),
        name="in_proj_rope" if rope else "in_proj",
    )(*args)


def _diff_attn_kernel(*refs, n_parts, lq, tq, lam_init):
    al_ref, sg_ref, q_ref = refs[:3]
    kv = refs[3:3 + 2 * n_parts]
    o_ref = refs[3 + 2 * n_parts]
    al = al_ref[...]
    lam = (jnp.exp(jnp.sum(al[0:1] * al[1:2], axis=-1, keepdims=True))
           - jnp.exp(jnp.sum(al[2:3] * al[3:4], axis=-1, keepdims=True)) + lam_init)
    lane = lax.broadcasted_iota(jnp.int32, (1, V7X_LANES), 1)
    m_first = (lane < A_DH).astype(BF16)
    m_second = (lane >= A_DH).astype(BF16)
    scale = A_DH ** -0.5
    sub_gain = sg_ref[...] * (1.0 - lam_init)

    def body(ci, carry):
        r0 = pl.multiple_of(ci * tq, tq)
        qc = q_ref[pl.ds(r0, tq), :]
        comps = []
        for mask in (m_first, m_second):
            qm = qc * mask
            s = [_dot_nt(qm, kv[2 * p][...]) * scale for p in range(n_parts)]
            m = functools.reduce(jnp.maximum, [jnp.max(t, axis=-1, keepdims=True) for t in s])
            e = [jnp.exp(t - m) for t in s]
            denom = functools.reduce(jnp.add, [jnp.sum(t, axis=-1, keepdims=True) for t in e])
            comps.append((e, 1.0 / denom))
        (e1, r1), (e2, r2) = comps
        r2 = r2 * lam
        o = None
        for p in range(n_parts):
            a = (e1[p] * r1 - e2[p] * r2).astype(BF16)
            t = _dot(a, kv[2 * p + 1][...])
            o = t if o is None else o + t
        ms = jnp.mean(o * o, axis=-1, keepdims=True)
        o_ref[pl.ds(r0, tq), :] = (o * lax.rsqrt(ms + EPS) * sub_gain).astype(o_ref.dtype)
        return carry

    lax.fori_loop(0, lq // tq, body, 0)


def _diff_attn(q, kv_parts, a_lambda, subln_g, lam_init):
    b, lq, _ = q.shape
    tq = min(ATTN_Q_TILE, lq)
    hw = 2 * A_DH
    head_spec = lambda n: pl.BlockSpec((None, n, hw), lambda bi, h: (bi, 0, h))
    in_specs = [_resident(a_lambda.shape), _resident((1, hw)), head_spec(lq)]
    args = [a_lambda, subln_g.reshape(1, hw), q]
    lk = 0
    for k, v in kv_parts:
        in_specs += [head_spec(k.shape[1]), head_spec(v.shape[1])]
        args += [k, v]
        lk += k.shape[1]
    vmem = 2 * (2 * lq + 4 * lk) * hw * 2 + 10 * tq * lk * 4 + (8 << 20)
    return pl.pallas_call(
        functools.partial(_diff_attn_kernel, n_parts=len(kv_parts), lq=lq, tq=tq, lam_init=lam_init),
        grid=(b, A_HEADS),
        in_specs=in_specs,
        out_specs=head_spec(lq),
        out_shape=jax.ShapeDtypeStruct((b, lq, A_HEADS * hw), BF16),
        compiler_params=_params(2, vmem),
        name="diff_attn",
    )(*args)


def _pool_kernel(p_ref, w_ref, s_ref, o_ref, *, l, tp):
    i = pl.program_id(1)
    t0 = pl.multiple_of(i * tp, tp)
    span = tp + 2 * POOL_HALO
    base = pl.multiple_of(jnp.clip(t0 - POOL_HALO, 0, l - span), POOL_HALO)
    win_hi, win_lo = _split_bf16(p_ref[pl.ds(base, span), :])
    centre = p_ref[pl.ds(t0, tp), :]
    t_band = t0 + lax.broadcasted_iota(jnp.int32, (tp, span), 0)
    j_band = base + lax.broadcasted_iota(jnp.int32, (tp, span), 1)
    t_row = t0 + lax.broadcasted_iota(jnp.int32, (tp, V7X_LANES), 0)
    for g, w in enumerate(POOL_WINDOWS):
        lanes = slice(g * V7X_LANES, (g + 1) * V7X_LANES)
        lo = jnp.maximum(t_band - w // 2, 0)
        hi = jnp.minimum(t_band + w // 2, l)
        band = jnp.where((j_band >= lo) & (j_band < hi), 1.0, 0.0).astype(BF16)
        total = _dot(band, win_hi[:, lanes]) + _dot(band, win_lo[:, lanes])
        cnt = (jnp.minimum(t_row + w // 2, l) - jnp.maximum(t_row - w // 2, 0)).astype(F32)
        pooled = total / cnt - centre[:, lanes]
        y = _dot(pooled.astype(BF16), w_ref[g]) * s_ref[:, lanes]
        o_ref[:, lanes] = y.astype(o_ref.dtype)


def _pool(p, w_pool, s_pool):
    b, l, wdt = p.shape
    tp = min(POOL_TILE, l // 2)
    return pl.pallas_call(
        functools.partial(_pool_kernel, l=l, tp=tp),
        grid=(b, l // tp),
        in_specs=[
            pl.BlockSpec((None, l, wdt), lambda bi, i: (bi, 0, 0)),
            _resident(w_pool.shape),
            _resident((1, wdt)),
        ],
        out_specs=pl.BlockSpec((None, tp, wdt), lambda bi, i: (bi, i, 0)),
        out_shape=jax.ShapeDtypeStruct((b, l, wdt), BF16),
        compiler_params=_params(2, 4 * l * wdt * 4 + (16 << 20)),
        name="pool",
    )(p, w_pool, s_pool.reshape(1, wdt))


def _nbr_attn_kernel(*refs, n_rows, local):
    if local:
        q_ref, k_ref, v_ref, kc_ref, vc_ref, bias_ref, o_ref = refs
    else:
        q_ref, kc_ref, vc_ref, o_ref = refs
    lane = lax.broadcasted_iota(jnp.int32, (1, V7X_LANES), 1)
    m_first = (lane < C_DH).astype(BF16)
    m_second = (lane >= C_DH).astype(BF16)
    lane_out = lax.broadcasted_iota(jnp.int32, (GRID_W, V7X_LANES), 1)
    scale = C_DH ** -0.5
    n_loc = NA_ROWS * GRID_W

    def body(r, carry):
        q0 = pl.multiple_of(r * GRID_W, GRID_W)
        qr = q_ref[pl.ds(q0, GRID_W), :]
        qbd = jnp.concatenate([qr * m_first, qr * m_second], axis=0)
        s_ctx = _dot_nt(kc_ref[...], qbd) * scale
        m = jnp.max(s_ctx, axis=0, keepdims=True)
        if local:
            rs = jnp.clip(r - NA_ROWS // 2, 0, n_rows - NA_ROWS)
            k0 = pl.multiple_of(rs * GRID_W, GRID_W)
            bias = jnp.concatenate(
                [bias_ref[rs + i - r + NA_ROWS - 1] for i in range(NA_ROWS)], axis=0)
            s_loc = _dot_nt(k_ref[pl.ds(k0, n_loc), :], qbd) * scale + bias
            m = jnp.maximum(m, jnp.max(s_loc, axis=0, keepdims=True))
            e_loc = jnp.exp(s_loc - m)
        e_ctx = jnp.exp(s_ctx - m)
        denom = jnp.sum(e_ctx, axis=0, keepdims=True)
        if local:
            denom = denom + jnp.sum(e_loc, axis=0, keepdims=True)
        inv = 1.0 / denom
        full = _dot_tn((e_ctx * inv).astype(BF16), vc_ref[...])
        if local:
            full = full + _dot_tn((e_loc * inv).astype(BF16), v_ref[pl.ds(k0, n_loc), :])
        out = jnp.where(lane_out < C_DH, full[:GRID_W], full[GRID_W:])
        o_ref[pl.ds(q0, GRID_W), :] = out.astype(o_ref.dtype)
        return carry

    lax.fori_loop(0, n_rows, body, 0)


def _nbr_attn(q, k, v, kc, vc, bias_t):
    b, lq, wdt = q.shape
    lc = kc.shape[1]
    n_rows = lq // GRID_W
    local = k is not None
    pair = lambda n: pl.BlockSpec((None, n, V7X_LANES), lambda bi, h: (bi, 0, h))
    if local:
        in_specs = [pair(lq), pair(lq), pair(lq), pair(lc), pair(lc),
                    pl.BlockSpec((None,) + bias_t.shape[1:], lambda bi, h: (h, 0, 0, 0))]
        args = [q, k, v, kc, vc, bias_t]
    else:
        in_specs = [pair(lq), pair(lc), pair(lc)]
        args = [q, kc, vc]
    return pl.pallas_call(
        functools.partial(_nbr_attn_kernel, n_rows=n_rows, local=local),
        grid=(b, wdt // V7X_LANES),
        in_specs=in_specs,
        out_specs=pair(lq),
        out_shape=jax.ShapeDtypeStruct((b, lq, wdt), BF16),
        compiler_params=_params(2, 32 * 1024 * 1024),
        name="nbr_attn" if local else "ctx_attn",
    )(*args)


def _nbr_bias_table(rpb):
    qc = np.arange(GRID_W)[:, None]
    kc = np.arange(GRID_W)[None, :]
    win0 = np.clip(qc - NA_COLS // 2, 0, GRID_W - NA_COLS)
    valid = (kc >= win0) & (kc < win0 + NA_COLS)
    dc = np.clip(kc - qc + NA_COLS - 1, 0, 2 * NA_COLS - 2)
    t = jnp.where(valid[None, None], rpb[:, :, dc], -jnp.inf)
    h, n_dr = t.shape[:2]
    t = t.reshape(h // 2, 2, n_dr, GRID_W, GRID_W).transpose(0, 2, 4, 1, 3)
    return t.reshape(h // 2, n_dr, GRID_W, 2 * GRID_W)


def _gating_kernel(u_ref, v_ref, w_ref, b_ref, o_ref, *, tg):
    n_groups = w_ref.shape[0]
    for n in range(tg // CHUNK):
        rows = slice(n * CHUNK, (n + 1) * CHUNK)
        for g in range(n_groups):
            lanes = slice(g * V7X_LANES, (g + 1) * V7X_LANES)
            sv = _dot(w_ref[g], v_ref[rows, lanes]) + b_ref[:, lanes]
            o_ref[rows, lanes] = (u_ref[rows, lanes] * sv).astype(o_ref.dtype)


def _gating(u, v, w_s, b_full):
    b, l, wdt = u.shape
    tg = min(ROW_TILE, l)
    tile = pl.BlockSpec((None, tg, wdt), lambda bi, i: (bi, i, 0))
    return pl.pallas_call(
        functools.partial(_gating_kernel, tg=tg),
        grid=(b, l // tg),
        in_specs=[tile, tile, _resident(w_s.shape), _resident(b_full.shape)],
        out_specs=tile,
        out_shape=jax.ShapeDtypeStruct((b, l, wdt), BF16),
        compiler_params=_params(2, 32 * 1024 * 1024),
        name="gating",
    )(u, v, w_s, b_full)


def _merge_kernel(x_ref, sh_ref, sc_ref, g1_ref, ng_ref, ya_ref, yb_ref, yc_ref, yd_ref,
                  wg_ref, wb_ref, wo_ref, o_ref):
    x = x_ref[...]
    d = x.shape[-1]
    hb = _norm_modulate(x, ng_ref[...], sh_ref[...], sc_ref[...])
    acc = None
    for i, y_ref in enumerate((ya_ref, yb_ref, yc_ref, yd_ref)):
        gate = _dot(hb, wg_ref[:, i * d:(i + 1) * d])
        term = jax.nn.sigmoid(gate) * _dot(y_ref[...], wb_ref[i * SEG_W:(i + 1) * SEG_W, :])
        acc = term if acc is None else acc + term
    o_ref[...] = x + g1_ref[...] * _dot(acc.astype(BF16), wo_ref[...])


def _merge(x, shift, scale, g1, norm_g, ys, w_gate, w_branch, w_out):
    b, l, d = x.shape
    tm = min(ROW_TILE, l)
    batched = shift.shape[0] != 1
    mod_map = (lambda bi, i: (bi, 0, 0)) if batched else (lambda bi, i: (0, 0, 0))
    x_tile = pl.BlockSpec((None, tm, d), lambda bi, i: (bi, i, 0))
    y_tile = pl.BlockSpec((None, tm, SEG_W), lambda bi, i: (bi, i, 0))
    mod = pl.BlockSpec((None, 1, d), mod_map)
    weights = (w_gate.size + w_branch.size + w_out.size) * 2
    vmem = weights + 2 * (2 * tm * d * 4 + 4 * tm * SEG_W * 2) + 8 * tm * d * 4 + (8 << 20)
    return pl.pallas_call(
        _merge_kernel,
        grid=(b, l // tm),
        in_specs=[x_tile, mod, mod, mod, _resident((1, d)), y_tile, y_tile, y_tile, y_tile,
                  _resident(w_gate.shape), _resident(w_branch.shape), _resident(w_out.shape)],
        out_specs=x_tile,
        out_shape=jax.ShapeDtypeStruct(x.shape, F32),
        compiler_params=_params(2, vmem),
        name="merge",
    )(x, shift, scale, g1, norm_g, *ys, w_gate, w_branch, w_out)


def _ffn_kernel(x_ref, sh_ref, sc_ref, g2_ref, ng_ref, wgu_ref, wd_ref, o_ref, u_ref):
    x = x_ref[...]
    hb = _norm_modulate(x, ng_ref[...], sh_ref[...], sc_ref[...])
    hidden = wd_ref.shape[0]
    for c in range(hidden // FFN_CHUNK):
        cols = slice(c * FFN_CHUNK, (c + 1) * FFN_CHUNK)
        a = _dot(hb, wgu_ref[:, cols])
        bb = _dot(hb, wgu_ref[:, hidden + c * FFN_CHUNK:hidden + (c + 1) * FFN_CHUNK])
        u_ref[:, cols] = (a * jax.nn.sigmoid(a) * bb).astype(BF16)
    o_ref[...] = x + g2_ref[...] * _dot(u_ref[...], wd_ref[...])


def _ffn(x, shift, scale, g2, norm_g, w_gu, w_down):
    b, l, d = x.shape
    tm = min(ROW_TILE, l)
    hidden = w_down.shape[0]
    batched = shift.shape[0] != 1
    mod_map = (lambda bi, i: (bi, 0, 0)) if batched else (lambda bi, i: (0, 0, 0))
    x_tile = pl.BlockSpec((None, tm, d), lambda bi, i: (bi, i, 0))
    mod = pl.BlockSpec((None, 1, d), mod_map)
    vmem = (w_gu.size + w_down.size) * 2 + 4 * tm * d * 4 + tm * hidden * 2 + 8 * tm * d * 4 + (8 << 20)
    return pl.pallas_call(
        _ffn_kernel,
        grid=(b, l // tm),
        in_specs=[x_tile, mod, mod, mod, _resident((1, d)),
                  _resident(w_gu.shape), _resident(w_down.shape)],
        out_specs=x_tile,
        out_shape=jax.ShapeDtypeStruct(x.shape, F32),
        scratch_shapes=[pltpu.VMEM((tm, hidden), BF16)],
        compiler_params=_params(2, vmem),
        name="ffn",
    )(x, shift, scale, g2, norm_g, w_gu, w_down)


def _rope_tables(n_tok):
    nf = A_DH // 4
    t = jnp.arange(n_tok)
    row = (t // GRID_W).astype(F32)
    col = (t % GRID_W).astype(F32)
    inv = ROPE_THETA ** (-jnp.arange(nf, dtype=F32) / nf)
    ar = row[:, None] * inv
    ac = col[:, None] * inv
    ang = jnp.concatenate([ar, ar, ac, ac], axis=-1)
    cos, sin = jnp.cos(ang), jnp.sin(ang)
    quarter = (np.arange(A_DH) // nf) % 2
    sin_up = jnp.where(quarter == 0, -sin, 0.0)
    sin_dn = jnp.where(quarter == 1, sin, 0.0)
    rep = V7X_LANES // A_DH
    return tuple(jnp.tile(a, (1, rep)) for a in (cos, sin_up, sin_dn))


def _group_matrix():
    g = np.arange(SEG_W) // A_DH
    return jnp.asarray(g[:, None] == g[None, :], dtype=BF16)


def kernel(x, c, ctx, c_ctx, w_mod, b_mod, norm1_g, w_in, a_qk_g, a_lambda, a_subln_g, b_pool_w,
           b_pool_s, c_qk_g, c_rpb, d_vn_g, d_ws, d_bs, w_branch, w_out, norm2_g, w_gu, w_down):
    b, s, d = x.shape
    depth = w_mod.shape[0]
    rope_tabs = _rope_tables(s)
    grp = _group_matrix()

    rows = -(-(b + 1) // 8) * 8
    c_all = jnp.zeros((rows, d), F32).at[:b].set(c).at[b].set(c_ctx)
    mod = _modulation(c_all, w_mod, b_mod)

    for l in range(depth):
        last = l == depth - 1
        lam_init = 0.8 - 0.6 * math.exp(-0.3 * l)
        mx = [mod[l, :b, k * d:(k + 1) * d].reshape(b, 1, d) for k in range(6)]
        mc = [mod[l, b:b + 1, k * d:(k + 1) * d].reshape(1, 1, d) for k in range(6)]
        w_mix = w_in[l][:, :MIX_COLS].astype(BF16)
        w_gate = w_in[l][:, MIX_COLS:].astype(BF16)
        w_br = w_branch[l].astype(BF16)
        w_o = w_out[l].astype(BF16)
        w_gu_l = w_gu[l].astype(BF16)
        w_dn = w_down[l].astype(BF16)
        w_pool = b_pool_w[l].astype(BF16)
        w_s = d_ws[l].astype(BF16)
        n1 = norm1_g[l].reshape(1, d)
        n2 = norm2_g[l].reshape(1, d)
        rep = SEG_W // A_DH
        qk_gains = jnp.stack([jnp.tile(a_qk_g[l, 0], rep), jnp.tile(a_qk_g[l, 1], rep),
                              jnp.tile(c_qk_g[l, 0], rep), jnp.tile(c_qk_g[l, 1], rep)])
        vn_g = d_vn_g[l].reshape(1, SEG_W)
        bs_full = jnp.repeat(d_bs[l].T, V7X_LANES, axis=1)
        bias_t = _nbr_bias_table(c_rpb[l])

        px = _in_proj(x, mx[0], mx[1], n1, w_mix, grp, qk_gains, vn_g, rope_tabs)
        pc = _in_proj(ctx, mc[0], mc[1], n1, w_mix, grp, qk_gains, vn_g, None)

        ya = _diff_attn(px[SEG_AQ], [(px[SEG_AK], px[SEG_AV]), (pc[SEG_AK], pc[SEG_AV])],
                        a_lambda[l], a_subln_g[l], lam_init)
        yb = _pool(px[SEG_B], w_pool, b_pool_s[l])
        yc = _nbr_attn(px[SEG_CQ], px[SEG_CK], px[SEG_CV], pc[SEG_CK], pc[SEG_CV], bias_t)
        yd = _gating(px[SEG_DU], px[SEG_DV], w_s, bs_full)
        x = _merge(x, mx[0], mx[1], mx[2], n1, (ya, yb, yc, yd), w_gate, w_br, w_o)
        x = _ffn(x, mx[3], mx[4], mx[5], n2, w_gu_l, w_dn)

        if not last:
            ya_c = _diff_attn(pc[SEG_AQ], [(pc[SEG_AK], pc[SEG_AV])], a_lambda[l], a_subln_g[l], lam_init)
            yb_c = _pool(pc[SEG_B], w_pool, b_pool_s[l])
            yc_c = _nbr_attn(pc[SEG_CQ], None, None, pc[SEG_CK], pc[SEG_CV], None)
            yd_c = _gating(pc[SEG_DU], pc[SEG_DV], w_s, bs_full)
            ctx = _merge(ctx, mc[0], mc[1], mc[2], n1, (ya_c, yb_c, yc_c, yd_c), w_gate, w_br, w_o)
            ctx = _ffn(ctx, mc[3], mc[4], mc[5], n2, w_gu_l, w_dn)
    return x
```

```python
import functools
import math

import jax
import jax.numpy as jnp
import numpy as np
from jax import lax
from jax.experimental import pallas as pl
from jax.experimental.pallas import tpu as pltpu

F32 = jnp.float32
BF16 = jnp.bfloat16

D_MODEL = 1024
GRID_W = 64
EPS = 1e-6
ROPE_THETA = 10000.0
A_HEADS = 4
A_DH = 64
POOL_WINDOWS = (2, 4, 8, 16)
C_DH = 64
NA_ROWS = 8
NA_COLS = 16
CHUNK = 128
N_BRANCH = 4
SEG_W = 512
N_SEG = 9
MIX_COLS = N_SEG * SEG_W
FFN_HIDDEN = 2816
SEG_AQ, SEG_AK, SEG_AV, SEG_B, SEG_CQ, SEG_CK, SEG_CV, SEG_DU, SEG_DV = range(N_SEG)
SEG_DTYPES = (BF16, BF16, BF16, F32, BF16, BF16, BF16, F32, BF16)

V7X_LANES = 128
V7X_VMEM_BYTES = 64 * 1024 * 1024
V7X_VMEM_LIMIT = V7X_VMEM_BYTES - 8 * 1024 * 1024

ROW_TILE = 512
ATTN_Q_TILE = 256
ATTN_TILES_PER_STEP = 2
POOL_TILE = 256
POOL_HALO = 8
FFN_CHUNK = 256
NBR_ROWS_PER_STEP = 8


def _params(n_axes, vmem_bytes):
    return pltpu.CompilerParams(
        dimension_semantics=("parallel",) * n_axes,
        vmem_limit_bytes=int(min(vmem_bytes, V7X_VMEM_LIMIT)))


def _resident(shape):
    zeros = (0,) * len(shape)
    return pl.BlockSpec(shape, lambda *_: zeros, pipeline_mode=pl.Buffered(1))


def _dot(a, b):
    return jnp.dot(a, b, preferred_element_type=F32)


def _dot_nt(a, b):
    return lax.dot_general(a, b, (((1,), (1,)), ((), ())), preferred_element_type=F32)


def _dot_tn(a, b):
    return lax.dot_general(a, b, (((0,), (0,)), ((), ())), preferred_element_type=F32)


def _split_bf16(t):
    hi = t.astype(BF16)
    lo = (t - hi.astype(F32)).astype(BF16)
    return hi, lo


def _norm_modulate(x, gain, shift, scale):
    ms = jnp.mean(x * x, axis=-1, keepdims=True)
    n = x * lax.rsqrt(ms + EPS)
    return ((n * gain) * (1.0 + scale) + shift).astype(BF16)


def _mod_kernel(c_ref, w_ref, b_ref, o_ref):
    a = c_ref[...]
    a = a * jax.nn.sigmoid(a)
    a_hi, a_lo = _split_bf16(a)
    w_hi, w_lo = _split_bf16(w_ref[...])
    o_ref[...] = _dot(a_hi, w_hi) + _dot(a_hi, w_lo) + _dot(a_lo, w_hi) + b_ref[...]


def _modulation(c_all, w_mod, b_mod):
    depth, d, n = w_mod.shape
    rows = c_all.shape[0]
    tn = 768
    return pl.pallas_call(
        _mod_kernel,
        grid=(depth, n // tn),
        in_specs=[
            pl.BlockSpec((rows, d), lambda l, j: (0, 0)),
            pl.BlockSpec((None, d, tn), lambda l, j: (l, 0, j)),
            pl.BlockSpec((None, 1, tn), lambda l, j: (l, 0, j)),
        ],
        out_specs=pl.BlockSpec((None, rows, tn), lambda l, j: (l, 0, j)),
        out_shape=jax.ShapeDtypeStruct((depth, rows, n), F32),
        compiler_params=_params(2, 32 * 1024 * 1024),
        name="modulation",
    )(c_all, w_mod, b_mod.reshape(depth, 1, n))


def _group_rms(r, grp, gain):
    ms = _dot((r * r).astype(BF16), grp) * (1.0 / A_DH)
    return r * lax.rsqrt(ms + EPS) * gain


def _rope_block(t, cos, sin_up, sin_dn):
    return (t * cos + pltpu.roll(t, V7X_LANES - 16, 1) * sin_up + pltpu.roll(t, 16, 1) * sin_dn)


def _in_proj_kernel(*refs, rope):
    x_ref, sh_ref, sc_ref, ng_ref, w_ref, grp_ref, qkg_ref, vng_ref = refs[:8]
    pos = 8
    if rope:
        cos_ref, sup_ref, sdn_ref = refs[8:11]
        pos = 11
    outs = refs[pos:pos + N_SEG]
    hb = _norm_modulate(x_ref[...], ng_ref[...], sh_ref[...], sc_ref[...])
    qk_row = {SEG_AQ: 0, SEG_AK: 1, SEG_CQ: 2, SEG_CK: 3}
    for seg in range(N_SEG):
        r = _dot(hb, w_ref[:, seg * SEG_W:(seg + 1) * SEG_W])
        if seg in qk_row:
            row = qk_row[seg]
            r = _group_rms(r, grp_ref[...], qkg_ref[row:row + 1, :])
            if rope and seg in (SEG_AQ, SEG_AK):
                cos, sup, sdn = cos_ref[...], sup_ref[...], sdn_ref[...]
                r = jnp.concatenate(
                    [_rope_block(r[:, j * V7X_LANES:(j + 1) * V7X_LANES], cos, sup, sdn)
                     for j in range(SEG_W // V7X_LANES)], axis=1)
        elif seg == SEG_DV:
            ms = jnp.mean(r * r, axis=-1, keepdims=True)
            r = r * lax.rsqrt(ms + EPS) * vng_ref[...]
        outs[seg][...] = r.astype(outs[seg].dtype)


def _in_proj(x, shift, scale, norm_g, w_mix, grp, qk_gains, vn_g, rope_tabs):
    b, l, d = x.shape
    tm = min(ROW_TILE, l)
    batched = shift.shape[0] != 1
    mod_map = (lambda bi, i: (bi, 0, 0)) if batched else (lambda bi, i: (0, 0, 0))
    rope = rope_tabs is not None
    in_specs = [
        pl.BlockSpec((None, tm, d), lambda bi, i: (bi, i, 0)),
        pl.BlockSpec((None, 1, d), mod_map),
        pl.BlockSpec((None, 1, d), mod_map),
        _resident((1, d)),
        _resident((d, MIX_COLS)),
        _resident((SEG_W, SEG_W)),
        _resident((4, SEG_W)),
        _resident((1, SEG_W)),
    ]
    args = [x, shift, scale, norm_g, w_mix, grp, qk_gains, vn_g]
    if rope:
        in_specs += [pl.BlockSpec((tm, V7X_LANES), lambda bi, i: (i, 0))] * 3
        args += list(rope_tabs)
    out_bytes = sum(tm * SEG_W * jnp.dtype(t).itemsize for t in SEG_DTYPES)
    vmem = d * MIX_COLS * 2 + 2 * (tm * d * 4 + out_bytes) + 8 * tm * SEG_W * 4 + (8 << 20)
    return pl.pallas_call(
        functools.partial(_in_proj_kernel, rope=rope),
        grid=(b, l // tm),
        in_specs=in_specs,
        out_specs=[pl.BlockSpec((None, tm, SEG_W), lambda bi, i: (bi, i, 0))] * N_SEG,
        out_shape=[jax.ShapeDtypeStruct((b, l, SEG_W), t) for t in SEG_DTYPES],
        compiler_params=_params(2, vmem),
        name="in_proj_rope" if rope else "in_proj",
    )(*args)


def _diff_attn_kernel(*refs, n_parts, lq, tq, tiles_per_step, lam_init):
    al_ref, sg_ref, q_ref = refs[:3]
    kv = refs[3:3 + 2 * n_parts]
    o_ref = refs[3 + 2 * n_parts]
    al = al_ref[...]
    lam = (jnp.exp(jnp.sum(al[0:1] * al[1:2], axis=-1, keepdims=True))
           - jnp.exp(jnp.sum(al[2:3] * al[3:4], axis=-1, keepdims=True)) + lam_init)
    lane = lax.broadcasted_iota(jnp.int32, (1, V7X_LANES), 1)
    scale = A_DH ** -0.5
    m_first = jnp.where(lane < A_DH, scale, 0.0).astype(BF16)
    m_second = jnp.where(lane >= A_DH, scale, 0.0).astype(BF16)
    sub_gain = sg_ref[...] * (1.0 - lam_init)

    def scores(r0):
        qc = q_ref[pl.ds(r0, tq), :]
        return [[_dot_nt(qc * mask, kv[2 * p][...]) for p in range(n_parts)]
                for mask in (m_first, m_second)]

    def softmax_parts(s):
        m = functools.reduce(jnp.maximum, [jnp.max(t, axis=-1, keepdims=True) for t in s])
        e = [jnp.exp(t - m) for t in s]
        denom = functools.reduce(jnp.add, [jnp.sum(t, axis=-1, keepdims=True) for t in e])
        return e, 1.0 / denom

    def finish(r0, s):
        (e1, r1), (e2, r2) = softmax_parts(s[0]), softmax_parts(s[1])
        r2 = r2 * lam
        o = None
        for p in range(n_parts):
            a = (e1[p] * r1 - e2[p] * r2).astype(BF16)
            t = _dot(a, kv[2 * p + 1][...])
            o = t if o is None else o + t
        ms = jnp.mean(o * o, axis=-1, keepdims=True)
        o_ref[pl.ds(r0, tq), :] = (o * lax.rsqrt(ms + EPS) * sub_gain).astype(o_ref.dtype)

    def body(it, carry):
        starts = [pl.multiple_of((it * tiles_per_step + j) * tq, tq) for j in range(tiles_per_step)]
        s_next = scores(starts[0])
        for j in range(tiles_per_step):
            s_cur = s_next
            if j + 1 < tiles_per_step:
                s_next = scores(starts[j + 1])
            finish(starts[j], s_cur)
        return carry

    lax.fori_loop(0, lq // (tq * tiles_per_step), body, 0)


def _diff_attn(q, kv_parts, a_lambda, subln_g, lam_init):
    b, lq, _ = q.shape
    tq = min(ATTN_Q_TILE, lq)
    hw = 2 * A_DH
    head_spec = lambda n: pl.BlockSpec((None, n, hw), lambda bi, h: (bi, 0, h))
    in_specs = [_resident(a_lambda.shape), _resident((1, hw)), head_spec(lq)]
    args = [a_lambda, subln_g.reshape(1, hw), q]
    lk = 0
    for k, v in kv_parts:
        in_specs += [head_spec(k.shape[1]), head_spec(v.shape[1])]
        args += [k, v]
        lk += k.shape[1]
    vmem = 2 * (2 * lq + 4 * lk) * hw * 2 + 10 * tq * lk * 4 + (8 << 20)
    return pl.pallas_call(
        functools.partial(_diff_attn_kernel, n_parts=len(kv_parts), lq=lq, tq=tq,
                          tiles_per_step=min(ATTN_TILES_PER_STEP, lq // tq), lam_init=lam_init),
        grid=(b, A_HEADS),
        in_specs=in_specs,
        out_specs=head_spec(lq),
        out_shape=jax.ShapeDtypeStruct((b, lq, A_HEADS * hw), BF16),
        compiler_params=_params(2, vmem),
        name="diff_attn",
    )(*args)


def _pool_kernel(p_ref, w_ref, s_ref, o_ref, *, l, tp):
    i = pl.program_id(1)
    t0 = pl.multiple_of(i * tp, tp)
    span = tp + 2 * POOL_HALO
    base = pl.multiple_of(jnp.clip(t0 - POOL_HALO, 0, l - span), POOL_HALO)
    win_hi, win_lo = _split_bf16(p_ref[pl.ds(base, span), :])
    centre = p_ref[pl.ds(t0, tp), :]
    t_band = t0 + lax.broadcasted_iota(jnp.int32, (tp, span), 0)
    j_band = base + lax.broadcasted_iota(jnp.int32, (tp, span), 1)
    t_row = t0 + lax.broadcasted_iota(jnp.int32, (tp, V7X_LANES), 0)
    for g, w in enumerate(POOL_WINDOWS):
        lanes = slice(g * V7X_LANES, (g + 1) * V7X_LANES)
        lo = jnp.maximum(t_band - w // 2, 0)
        hi = jnp.minimum(t_band + w // 2, l)
        band = jnp.where((j_band >= lo) & (j_band < hi), 1.0, 0.0).astype(BF16)
        total = _dot(band, win_hi[:, lanes]) + _dot(band, win_lo[:, lanes])
        cnt = (jnp.minimum(t_row + w // 2, l) - jnp.maximum(t_row - w // 2, 0)).astype(F32)
        pooled = total / cnt - centre[:, lanes]
        y = _dot(pooled.astype(BF16), w_ref[g]) * s_ref[:, lanes]
        o_ref[:, lanes] = y.astype(o_ref.dtype)


def _pool(p, w_pool, s_pool):
    b, l, wdt = p.shape
    tp = min(POOL_TILE, l // 2)
    return pl.pallas_call(
        functools.partial(_pool_kernel, l=l, tp=tp),
        grid=(b, l // tp),
        in_specs=[
            pl.BlockSpec((None, l, wdt), lambda bi, i: (bi, 0, 0)),
            _resident(w_pool.shape),
            _resident((1, wdt)),
        ],
        out_specs=pl.BlockSpec((None, tp, wdt), lambda bi, i: (bi, i, 0)),
        out_shape=jax.ShapeDtypeStruct((b, l, wdt), BF16),
        compiler_params=_params(2, 4 * l * wdt * 4 + (16 << 20)),
        name="pool",
    )(p, w_pool, s_pool.reshape(1, wdt))


def _nbr_attn_kernel(*refs, n_rows, g_rows, local):
    if local:
        q_ref, k_ref, v_ref, kc_ref, vc_ref, bias_ref, o_ref = refs
    else:
        q_ref, kc_ref, vc_ref, o_ref = refs
    lane = lax.broadcasted_iota(jnp.int32, (1, V7X_LANES), 1)
    scale = C_DH ** -0.5
    m_first = jnp.where(lane < C_DH, scale, 0.0).astype(BF16)
    m_second = jnp.where(lane >= C_DH, scale, 0.0).astype(BF16)
    lane_out = lax.broadcasted_iota(jnp.int32, (GRID_W, V7X_LANES), 1)
    n_loc = NA_ROWS * GRID_W
    hq = 2 * GRID_W

    def body(it, carry):
        r_base = it * g_rows
        q0 = pl.multiple_of(r_base * GRID_W, g_rows * GRID_W)
        qg = q_ref[pl.ds(q0, g_rows * GRID_W), :]
        blocks = []
        for gi in range(g_rows):
            qr = qg[gi * GRID_W:(gi + 1) * GRID_W]
            blocks += [qr * m_first, qr * m_second]
        qbd = jnp.concatenate(blocks, axis=0)
        s_ctx = _dot_nt(qbd, kc_ref[...])
        m = jnp.max(s_ctx, axis=-1, keepdims=True)
        if local:
            k0s, parts = [], []
            for gi in range(g_rows):
                r = r_base + gi
                rs = jnp.clip(r - NA_ROWS // 2, 0, n_rows - NA_ROWS)
                k0s.append(pl.multiple_of(rs * GRID_W, GRID_W))
                parts.append(_dot_nt(qbd[gi * hq:(gi + 1) * hq], k_ref[pl.ds(k0s[gi], n_loc), :])
                             + bias_ref[rs - r + NA_ROWS - 1])
            s_loc = jnp.concatenate(parts, axis=0)
            m = jnp.maximum(m, jnp.max(s_loc, axis=-1, keepdims=True))
            e_loc = jnp.exp(s_loc - m)
        e_ctx = jnp.exp(s_ctx - m)
        denom = jnp.sum(e_ctx, axis=-1, keepdims=True)
        full = _dot(e_ctx.astype(BF16), vc_ref[...])
        if local:
            denom = denom + jnp.sum(e_loc, axis=-1, keepdims=True)
            e_loc = e_loc.astype(BF16)
            full = full + jnp.concatenate(
                [_dot(e_loc[gi * hq:(gi + 1) * hq], v_ref[pl.ds(k0s[gi], n_loc), :])
                 for gi in range(g_rows)], axis=0)
        full = full * (1.0 / denom)
        out = jnp.concatenate(
            [jnp.where(lane_out < C_DH, full[gi * hq:gi * hq + GRID_W], full[gi * hq + GRID_W:(gi + 1) * hq])
             for gi in range(g_rows)], axis=0)
        o_ref[pl.ds(q0, g_rows * GRID_W), :] = out.astype(o_ref.dtype)
        return carry

    lax.fori_loop(0, n_rows // g_rows, body, 0)


def _nbr_attn(q, k, v, kc, vc, bias_t):
    b, lq, wdt = q.shape
    lc = kc.shape[1]
    n_rows = lq // GRID_W
    local = k is not None
    pair = lambda n: pl.BlockSpec((None, n, V7X_LANES), lambda h, bi: (bi, 0, h))
    if local:
        in_specs = [pair(lq), pair(lq), pair(lq), pair(lc), pair(lc),
                    pl.BlockSpec((None,) + bias_t.shape[1:], lambda h, bi: (h, 0, 0, 0))]
        args = [q, k, v, kc, vc, bias_t]
    else:
        in_specs = [pair(lq), pair(lc), pair(lc)]
        args = [q, kc, vc]
    return pl.pallas_call(
        functools.partial(_nbr_attn_kernel, n_rows=n_rows, g_rows=min(NBR_ROWS_PER_STEP, n_rows), local=local),
        grid=(wdt // V7X_LANES, b),
        in_specs=in_specs,
        out_specs=pair(lq),
        out_shape=jax.ShapeDtypeStruct((b, lq, wdt), BF16),
        compiler_params=_params(2, 32 * 1024 * 1024),
        name="nbr_attn" if local else "ctx_attn",
    )(*args)


def _nbr_bias_table(rpb):
    qc = np.arange(GRID_W)[:, None]
    kc = np.arange(GRID_W)[None, :]
    win0 = np.clip(qc - NA_COLS // 2, 0, GRID_W - NA_COLS)
    valid = (kc >= win0) & (kc < win0 + NA_COLS)
    dc = np.clip(kc - qc + NA_COLS - 1, 0, 2 * NA_COLS - 2)
    t = jnp.where(valid[None, None], rpb[:, :, dc], -jnp.inf)
    h = t.shape[0]
    dr = np.arange(NA_ROWS)[:, None] + np.arange(NA_ROWS)[None, :]
    t = t[:, dr]
    t = t.reshape(h // 2, 2, NA_ROWS, NA_ROWS, GRID_W, GRID_W).transpose(0, 2, 1, 4, 3, 5)
    return t.reshape(h // 2, NA_ROWS, 2 * GRID_W, NA_ROWS * GRID_W)


def _gating_kernel(u_ref, v_ref, w_ref, b_ref, o_ref, *, tg):
    n_groups = w_ref.shape[0]
    for n in range(tg // CHUNK):
        rows = slice(n * CHUNK, (n + 1) * CHUNK)
        for g in range(n_groups):
            lanes = slice(g * V7X_LANES, (g + 1) * V7X_LANES)
            sv = _dot(w_ref[g], v_ref[rows, lanes]) + b_ref[:, lanes]
            o_ref[rows, lanes] = (u_ref[rows, lanes] * sv).astype(o_ref.dtype)


def _gating(u, v, w_s, b_full):
    b, l, wdt = u.shape
    tg = min(ROW_TILE, l)
    tile = pl.BlockSpec((None, tg, wdt), lambda bi, i: (bi, i, 0))
    return pl.pallas_call(
        functools.partial(_gating_kernel, tg=tg),
        grid=(b, l // tg),
        in_specs=[tile, tile, _resident(w_s.shape), _resident(b_full.shape)],
        out_specs=tile,
        out_shape=jax.ShapeDtypeStruct((b, l, wdt), BF16),
        compiler_params=_params(2, 32 * 1024 * 1024),
        name="gating",
    )(u, v, w_s, b_full)


def _merge_kernel(x_ref, sh_ref, sc_ref, g1_ref, ng_ref, ya_ref, yb_ref, yc_ref, yd_ref,
                  wg_ref, wb_ref, wo_ref, o_ref):
    x = x_ref[...]
    d = x.shape[-1]
    hb = _norm_modulate(x, ng_ref[...], sh_ref[...], sc_ref[...])
    acc = None
    for i, y_ref in enumerate((ya_ref, yb_ref, yc_ref, yd_ref)):
        gate = _dot(hb, wg_ref[:, i * d:(i + 1) * d])
        term = jax.nn.sigmoid(gate) * _dot(y_ref[...], wb_ref[i * SEG_W:(i + 1) * SEG_W, :])
        acc = term if acc is None else acc + term
    o_ref[...] = x + g1_ref[...] * _dot(acc.astype(BF16), wo_ref[...])


def _merge(x, shift, scale, g1, norm_g, ys, w_gate, w_branch, w_out):
    b, l, d = x.shape
    tm = min(ROW_TILE, l)
    batched = shift.shape[0] != 1
    mod_map = (lambda bi, i: (bi, 0, 0)) if batched else (lambda bi, i: (0, 0, 0))
    x_tile = pl.BlockSpec((None, tm, d), lambda bi, i: (bi, i, 0))
    y_tile = pl.BlockSpec((None, tm, SEG_W), lambda bi, i: (bi, i, 0))
    mod = pl.BlockSpec((None, 1, d), mod_map)
    weights = (w_gate.size + w_branch.size + w_out.size) * 2
    vmem = weights + 2 * (2 * tm * d * 4 + 4 * tm * SEG_W * 2) + 8 * tm * d * 4 + (8 << 20)
    return pl.pallas_call(
        _merge_kernel,
        grid=(b, l // tm),
        in_specs=[x_tile, mod, mod, mod, _resident((1, d)), y_tile, y_tile, y_tile, y_tile,
                  _resident(w_gate.shape), _resident(w_branch.shape), _resident(w_out.shape)],
        out_specs=x_tile,
        out_shape=jax.ShapeDtypeStruct(x.shape, F32),
        compiler_params=_params(2, vmem),
        name="merge",
    )(x, shift, scale, g1, norm_g, *ys, w_gate, w_branch, w_out)


def _ffn_kernel(x_ref, sh_ref, sc_ref, g2_ref, ng_ref, wgu_ref, wd_ref, o_ref, u_ref):
    x = x_ref[...]
    hb = _norm_modulate(x, ng_ref[...], sh_ref[...], sc_ref[...])
    hidden = wd_ref.shape[0]
    for c in range(hidden // FFN_CHUNK):
        cols = slice(c * FFN_CHUNK, (c + 1) * FFN_CHUNK)
        a = _dot(hb, wgu_ref[:, cols])
        bb = _dot(hb, wgu_ref[:, hidden + c * FFN_CHUNK:hidden + (c + 1) * FFN_CHUNK])
        u_ref[:, cols] = (a * jax.nn.sigmoid(a) * bb).astype(BF16)
    o_ref[...] = x + g2_ref[...] * _dot(u_ref[...], wd_ref[...])


def _ffn(x, shift, scale, g2, norm_g, w_gu, w_down):
    b, l, d = x.shape
    tm = min(ROW_TILE, l)
    hidden = w_down.shape[0]
    batched = shift.shape[0] != 1
    mod_map = (lambda bi, i: (bi, 0, 0)) if batched else (lambda bi, i: (0, 0, 0))
    x_tile = pl.BlockSpec((None, tm, d), lambda bi, i: (bi, i, 0))
    mod = pl.BlockSpec((None, 1, d), mod_map)
    vmem = (w_gu.size + w_down.size) * 2 + 4 * tm * d * 4 + tm * hidden * 2 + 8 * tm * d * 4 + (8 << 20)
    return pl.pallas_call(
        _ffn_kernel,
        grid=(b, l // tm),
        in_specs=[x_tile, mod, mod, mod, _resident((1, d)),
                  _resident(w_gu.shape), _resident(w_down.shape)],
        out_specs=x_tile,
        out_shape=jax.ShapeDtypeStruct(x.shape, F32),
        scratch_shapes=[pltpu.VMEM((tm, hidden), BF16)],
        compiler_params=_params(2, vmem),
        name="ffn",
    )(x, shift, scale, g2, norm_g, w_gu, w_down)


def _rope_tables(n_tok):
    nf = A_DH // 4
    t = jnp.arange(n_tok)
    row = (t // GRID_W).astype(F32)
    col = (t % GRID_W).astype(F32)
    inv = ROPE_THETA ** (-jnp.arange(nf, dtype=F32) / nf)
    ar = row[:, None] * inv
    ac = col[:, None] * inv
    ang = jnp.concatenate([ar, ar, ac, ac], axis=-1)
    cos, sin = jnp.cos(ang), jnp.sin(ang)
    quarter = (np.arange(A_DH) // nf) % 2
    sin_up = jnp.where(quarter == 0, -sin, 0.0)
    sin_dn = jnp.where(quarter == 1, sin, 0.0)
    rep = V7X_LANES // A_DH
    return tuple(jnp.tile(a, (1, rep)) for a in (cos, sin_up, sin_dn))


def _group_matrix():
    g = np.arange(SEG_W) // A_DH
    return jnp.asarray(g[:, None] == g[None, :], dtype=BF16)


def kernel(x, c, ctx, c_ctx, w_mod, b_mod, norm1_g, w_in, a_qk_g, a_lambda, a_subln_g, b_pool_w,
           b_pool_s, c_qk_g, c_rpb, d_vn_g, d_ws, d_bs, w_branch, w_out, norm2_g, w_gu, w_down):
    b, s, d = x.shape
    depth = w_mod.shape[0]
    rope_tabs = _rope_tables(s)
    grp = _group_matrix()

    rows = -(-(b + 1) // 8) * 8
    c_all = jnp.zeros((rows, d), F32).at[:b].set(c).at[b].set(c_ctx)
    mod = _modulation(c_all, w_mod, b_mod)

    for l in range(depth):
        last = l == depth - 1
        lam_init = 0.8 - 0.6 * math.exp(-0.3 * l)
        mx = [mod[l, :b, k * d:(k + 1) * d].reshape(b, 1, d) for k in range(6)]
        mc = [mod[l, b:b + 1, k * d:(k + 1) * d].reshape(1, 1, d) for k in range(6)]
        w_mix = w_in[l][:, :MIX_COLS].astype(BF16)
        w_gate = w_in[l][:, MIX_COLS:].astype(BF16)
        w_br = w_branch[l].astype(BF16)
        w_o = w_out[l].astype(BF16)
        w_gu_l = w_gu[l].astype(BF16)
        w_dn = w_down[l].astype(BF16)
        w_pool = b_pool_w[l].astype(BF16)
        w_s = d_ws[l].astype(BF16)
        n1 = norm1_g[l].reshape(1, d)
        n2 = norm2_g[l].reshape(1, d)
        rep = SEG_W // A_DH
        qk_gains = jnp.stack([jnp.tile(a_qk_g[l, 0], rep), jnp.tile(a_qk_g[l, 1], rep),
                              jnp.tile(c_qk_g[l, 0], rep), jnp.tile(c_qk_g[l, 1], rep)])
        vn_g = d_vn_g[l].reshape(1, SEG_W)
        bs_full = jnp.repeat(d_bs[l].T, V7X_LANES, axis=1)
        bias_t = _nbr_bias_table(c_rpb[l])

        px = _in_proj(x, mx[0], mx[1], n1, w_mix, grp, qk_gains, vn_g, rope_tabs)
        pc = _in_proj(ctx, mc[0], mc[1], n1, w_mix, grp, qk_gains, vn_g, None)

        ya = _diff_attn(px[SEG_AQ], [(px[SEG_AK], px[SEG_AV]), (pc[SEG_AK], pc[SEG_AV])],
                        a_lambda[l], a_subln_g[l], lam_init)
        yb = _pool(px[SEG_B], w_pool, b_pool_s[l])
        yc = _nbr_attn(px[SEG_CQ], px[SEG_CK], px[SEG_CV], pc[SEG_CK], pc[SEG_CV], bias_t)
        yd = _gating(px[SEG_DU], px[SEG_DV], w_s, bs_full)
        x = _merge(x, mx[0], mx[1], mx[2], n1, (ya, yb, yc, yd), w_gate, w_br, w_o)
        x = _ffn(x, mx[3], mx[4], mx[5], n2, w_gu_l, w_dn)

        if not last:
            ya_c = _diff_attn(pc[SEG_AQ], [(pc[SEG_AK], pc[SEG_AV])], a_lambda[l], a_subln_g[l], lam_init)
            yb_c = _pool(pc[SEG_B], w_pool, b_pool_s[l])
            yc_c = _nbr_attn(pc[SEG_CQ], None, None, pc[SEG_CK], pc[SEG_CV], None)
            yd_c = _gating(pc[SEG_DU], pc[SEG_DV], w_s, bs_full)
            ctx = _merge(ctx, mc[0], mc[1], mc[2], n1, (ya_c, yb_c, yc_c, yd_c), w_gate, w_br, w_o)
            ctx = _ffn(ctx, mc[3], mc[4], mc[5], n2, w_gu_l, w_dn)
    return x
```

```python
import functools
import math

import jax
import jax.numpy as jnp
import numpy as np
from jax import lax
from jax.experimental import pallas as pl
from jax.experimental.pallas import tpu as pltpu

F32 = jnp.float32
BF16 = jnp.bfloat16

D_MODEL = 1024
GRID_W = 64
EPS = 1e-6
ROPE_THETA = 10000.0
A_HEADS = 4
A_DH = 64
POOL_WINDOWS = (2, 4, 8, 16)
C_DH = 64
NA_ROWS = 8
NA_COLS = 16
CHUNK = 128
N_BRANCH = 4
SEG_W = 512
N_SEG = 9
MIX_COLS = N_SEG * SEG_W
FFN_HIDDEN = 2816
SEG_AQ, SEG_AK, SEG_AV, SEG_B, SEG_CQ, SEG_CK, SEG_CV, SEG_DU, SEG_DV = range(N_SEG)
SEG_DTYPES = (BF16, BF16, BF16, F32, BF16, BF16, BF16, F32, BF16)

V7X_LANES = 128
V7X_VMEM_BYTES = 64 * 1024 * 1024
V7X_VMEM_LIMIT = V7X_VMEM_BYTES - 8 * 1024 * 1024

ROW_TILE = 512
ATTN_Q_TILE = 256
ATTN_TILES_PER_STEP = 4
POOL_TILE = 256
POOL_HALO = 8
FFN_CHUNK = 256
NBR_ROWS_PER_STEP = 8


def _params(n_axes, vmem_bytes):
    return pltpu.CompilerParams(
        dimension_semantics=("parallel",) * n_axes,
        vmem_limit_bytes=int(min(vmem_bytes, V7X_VMEM_LIMIT)))


def _resident(shape):
    zeros = (0,) * len(shape)
    return pl.BlockSpec(shape, lambda *_: zeros, pipeline_mode=pl.Buffered(1))


def _dot(a, b):
    return jnp.dot(a, b, preferred_element_type=F32)


def _dot_nt(a, b):
    return lax.dot_general(a, b, (((1,), (1,)), ((), ())), preferred_element_type=F32)


def _dot_tn(a, b):
    return lax.dot_general(a, b, (((0,), (0,)), ((), ())), preferred_element_type=F32)


def _split_bf16(t):
    hi = t.astype(BF16)
    lo = (t - hi.astype(F32)).astype(BF16)
    return hi, lo


def _norm_modulate(x, gain, shift, scale):
    ms = jnp.mean(x * x, axis=-1, keepdims=True)
    n = x * lax.rsqrt(ms + EPS)
    return ((n * gain) * (1.0 + scale) + shift).astype(BF16)


def _mod_kernel(c_ref, w_ref, b_ref, o_ref):
    a = c_ref[...]
    a = a * jax.nn.sigmoid(a)
    a_hi, a_lo = _split_bf16(a)
    w_hi, w_lo = _split_bf16(w_ref[...])
    o_ref[...] = _dot(a_hi, w_hi) + _dot(a_hi, w_lo) + _dot(a_lo, w_hi) + b_ref[...]


def _modulation(c_all, w_mod, b_mod):
    depth, d, n = w_mod.shape
    rows = c_all.shape[0]
    tn = 768
    return pl.pallas_call(
        _mod_kernel,
        grid=(depth, n // tn),
        in_specs=[
            pl.BlockSpec((rows, d), lambda l, j: (0, 0)),
            pl.BlockSpec((None, d, tn), lambda l, j: (l, 0, j)),
            pl.BlockSpec((None, 1, tn), lambda l, j: (l, 0, j)),
        ],
        out_specs=pl.BlockSpec((None, rows, tn), lambda l, j: (l, 0, j)),
        out_shape=jax.ShapeDtypeStruct((depth, rows, n), F32),
        compiler_params=_params(2, 32 * 1024 * 1024),
        name="modulation",
    )(c_all, w_mod, b_mod.reshape(depth, 1, n))


def _group_rms(r, grp, gain):
    ms = _dot((r * r).astype(BF16), grp) * (1.0 / A_DH)
    return r * lax.rsqrt(ms + EPS) * gain


def _rope_block(t, cos, sin_up, sin_dn):
    return (t * cos + pltpu.roll(t, V7X_LANES - 16, 1) * sin_up + pltpu.roll(t, 16, 1) * sin_dn)


def _in_proj_kernel(*refs, rope):
    x_ref, sh_ref, sc_ref, ng_ref, w_ref, grp_ref, qkg_ref, vng_ref = refs[:8]
    pos = 8
    if rope:
        cos_ref, sup_ref, sdn_ref = refs[8:11]
        pos = 11
    outs = refs[pos:pos + N_SEG]
    hb = _norm_modulate(x_ref[...], ng_ref[...], sh_ref[...], sc_ref[...])
    qk_row = {SEG_AQ: 0, SEG_AK: 1, SEG_CQ: 2, SEG_CK: 3}
    for seg in range(N_SEG):
        r = _dot(hb, w_ref[:, seg * SEG_W:(seg + 1) * SEG_W])
        if seg in qk_row:
            row = qk_row[seg]
            r = _group_rms(r, grp_ref[...], qkg_ref[row:row + 1, :])
            if rope and seg in (SEG_AQ, SEG_AK):
                cos, sup, sdn = cos_ref[...], sup_ref[...], sdn_ref[...]
                r = jnp.concatenate(
                    [_rope_block(r[:, j * V7X_LANES:(j + 1) * V7X_LANES], cos, sup, sdn)
                     for j in range(SEG_W // V7X_LANES)], axis=1)
        elif seg == SEG_DV:
            ms = jnp.mean(r * r, axis=-1, keepdims=True)
            r = r * lax.rsqrt(ms + EPS) * vng_ref[...]
        outs[seg][...] = r.astype(outs[seg].dtype)


def _in_proj(x, shift, scale, norm_g, w_mix, grp, qk_gains, vn_g, rope_tabs):
    b, l, d = x.shape
    tm = min(ROW_TILE, l)
    batched = shift.shape[0] != 1
    mod_map = (lambda bi, i: (bi, 0, 0)) if batched else (lambda bi, i: (0, 0, 0))
    rope = rope_tabs is not None
    in_specs = [
        pl.BlockSpec((None, tm, d), lambda bi, i: (bi, i, 0)),
        pl.BlockSpec((None, 1, d), mod_map),
        pl.BlockSpec((None, 1, d), mod_map),
        _resident((1, d)),
        _resident((d, MIX_COLS)),
        _resident((SEG_W, SEG_W)),
        _resident((4, SEG_W)),
        _resident((1, SEG_W)),
    ]
    args = [x, shift, scale, norm_g, w_mix, grp, qk_gains, vn_g]
    if rope:
        in_specs += [pl.BlockSpec((tm, V7X_LANES), lambda bi, i: (i, 0))] * 3
        args += list(rope_tabs)
    out_bytes = sum(tm * SEG_W * jnp.dtype(t).itemsize for t in SEG_DTYPES)
    vmem = d * MIX_COLS * 2 + 2 * (tm * d * 4 + out_bytes) + 8 * tm * SEG_W * 4 + (8 << 20)
    return pl.pallas_call(
        functools.partial(_in_proj_kernel, rope=rope),
        grid=(b, l // tm),
        in_specs=in_specs,
        out_specs=[pl.BlockSpec((None, tm, SEG_W), lambda bi, i: (bi, i, 0))] * N_SEG,
        out_shape=[jax.ShapeDtypeStruct((b, l, SEG_W), t) for t in SEG_DTYPES],
        compiler_params=_params(2, vmem),
        name="in_proj_rope" if rope else "in_proj",
    )(*args)


def _diff_attn_kernel(*refs, n_parts, lq, tq, tiles_per_step, lam_init):
    al_ref, sg_ref, q_ref = refs[:3]
    kv = refs[3:3 + 2 * n_parts]
    o_ref, kcat_ref, vt_ref = refs[3 + 2 * n_parts:]
    al = al_ref[...]
    lam = (jnp.exp(jnp.sum(al[0:1] * al[1:2], axis=-1, keepdims=True))
           - jnp.exp(jnp.sum(al[2:3] * al[3:4], axis=-1, keepdims=True)) + lam_init)
    lane = lax.broadcasted_iota(jnp.int32, (1, V7X_LANES), 1)
    scale = A_DH ** -0.5
    masks = (jnp.where(lane < A_DH, scale, 0.0).astype(BF16),
             jnp.where(lane >= A_DH, scale, 0.0).astype(BF16))
    sub_gain = sg_ref[...] * (1.0 - lam_init)

    off = 0
    for p in range(n_parts):
        n = kv[2 * p].shape[0]
        kcat_ref[off:off + n, :] = kv[2 * p][...]
        vt_ref[:, off:off + n] = kv[2 * p + 1][...].astype(F32).T.astype(BF16)
        off += n

    def scores(r0):
        qc = q_ref[pl.ds(r0, tq), :]
        return [_dot_nt(kcat_ref[...], qc * mask) for mask in masks]

    def finish(r0, s):
        acc, inv = [], []
        for sc in s:
            e = jnp.exp(sc - jnp.max(sc, axis=0, keepdims=True))
            inv.append(1.0 / jnp.sum(e, axis=0, keepdims=True))
            acc.append(_dot(vt_ref[...], e.astype(BF16)))
        o = (acc[0] * inv[0] - acc[1] * (inv[1] * lam)).T
        ms = jnp.mean(o * o, axis=-1, keepdims=True)
        o_ref[pl.ds(r0, tq), :] = (o * lax.rsqrt(ms + EPS) * sub_gain).astype(o_ref.dtype)

    def body(it, carry):
        starts = [pl.multiple_of((it * tiles_per_step + j) * tq, tq) for j in range(tiles_per_step)]
        s_next = scores(starts[0])
        for j in range(tiles_per_step):
            s_cur = s_next
            if j + 1 < tiles_per_step:
                s_next = scores(starts[j + 1])
            finish(starts[j], s_cur)
        return carry

    lax.fori_loop(0, lq // (tq * tiles_per_step), body, 0)


def _diff_attn(q, kv_parts, a_lambda, subln_g, lam_init):
    b, lq, _ = q.shape
    tq = min(ATTN_Q_TILE, lq)
    hw = 2 * A_DH
    head_spec = lambda n: pl.BlockSpec((None, n, hw), lambda bi, h: (bi, 0, h))
    in_specs = [_resident(a_lambda.shape), _resident((1, hw)), head_spec(lq)]
    args = [a_lambda, subln_g.reshape(1, hw), q]
    lk = 0
    for k, v in kv_parts:
        in_specs += [head_spec(k.shape[1]), head_spec(v.shape[1])]
        args += [k, v]
        lk += k.shape[1]
    vmem = 2 * (2 * lq + 4 * lk) * hw * 2 + 2 * lk * hw * 2 + 10 * tq * lk * 4 + (8 << 20)
    return pl.pallas_call(
        functools.partial(_diff_attn_kernel, n_parts=len(kv_parts), lq=lq, tq=tq,
                          tiles_per_step=min(ATTN_TILES_PER_STEP, lq // tq), lam_init=lam_init),
        grid=(b, A_HEADS),
        in_specs=in_specs,
        out_specs=head_spec(lq),
        out_shape=jax.ShapeDtypeStruct((b, lq, A_HEADS * hw), BF16),
        scratch_shapes=[pltpu.VMEM((lk, hw), BF16), pltpu.VMEM((hw, lk), BF16)],
        compiler_params=_params(2, vmem),
        name="diff_attn",
    )(*args)


def _tree_sum(terms):
    while len(terms) > 1:
        terms = [terms[i] + terms[i + 1] for i in range(0, len(terms), 2)]
    return terms[0]


def _pool_kernel(p_ref, w_ref, s_ref, o_ref, pad_ref, *, l, tp):
    wdt = p_ref.shape[-1]
    pad_ref[0:POOL_HALO, :] = jnp.zeros((POOL_HALO, wdt), F32)
    pad_ref[POOL_HALO + l:, :] = jnp.zeros((POOL_HALO, wdt), F32)
    pad_ref[POOL_HALO:POOL_HALO + l, :] = p_ref[...]
    n_tiles = l // tp
    for ti in range(n_tiles):
        t0 = ti * tp
        edge = ti == 0 or ti == n_tiles - 1
        if edge:
            t_row = t0 + lax.broadcasted_iota(jnp.int32, (tp, V7X_LANES), 0)
        for g, w in enumerate(POOL_WINDOWS):
            lanes = slice(g * V7X_LANES, (g + 1) * V7X_LANES)
            shifted = [pad_ref[POOL_HALO + t0 + j:POOL_HALO + t0 + j + tp, lanes]
                       for j in range(-(w // 2), w // 2)]
            centre = shifted[w // 2]
            total = _tree_sum(shifted)
            if edge:
                cnt = (jnp.minimum(t_row + w // 2, l) - jnp.maximum(t_row - w // 2, 0)).astype(F32)
                mean = total / cnt
            else:
                mean = total * (1.0 / w)
            y = _dot((mean - centre).astype(BF16), w_ref[g]) * s_ref[:, lanes]
            o_ref[t0:t0 + tp, lanes] = y.astype(o_ref.dtype)


def _pool(p, w_pool, s_pool):
    b, l, wdt = p.shape
    tp = min(POOL_TILE, l)
    seq = pl.BlockSpec((None, l, wdt), lambda bi: (bi, 0, 0))
    return pl.pallas_call(
        functools.partial(_pool_kernel, l=l, tp=tp),
        grid=(b,),
        in_specs=[seq, _resident(w_pool.shape), _resident((1, wdt))],
        out_specs=seq,
        out_shape=jax.ShapeDtypeStruct((b, l, wdt), BF16),
        scratch_shapes=[pltpu.VMEM((l + 2 * POOL_HALO, wdt), F32)],
        compiler_params=_params(1, 5 * l * wdt * 4 + (16 << 20)),
        name="pool",
    )(p, w_pool, s_pool.reshape(1, wdt))


def _nbr_attn_kernel(*refs, n_rows, g_rows, local):
    if local:
        q_ref, k_ref, v_ref, kc_ref, vc_ref, bias_ref, o_ref = refs
    else:
        q_ref, kc_ref, vc_ref, o_ref = refs
    lane = lax.broadcasted_iota(jnp.int32, (1, V7X_LANES), 1)
    scale = C_DH ** -0.5
    m_first = jnp.where(lane < C_DH, scale, 0.0).astype(BF16)
    m_second = jnp.where(lane >= C_DH, scale, 0.0).astype(BF16)
    lane_out = lax.broadcasted_iota(jnp.int32, (GRID_W, V7X_LANES), 1)
    n_loc = NA_ROWS * GRID_W
    hq = 2 * GRID_W

    def body(it, carry):
        r_base = it * g_rows
        q0 = pl.multiple_of(r_base * GRID_W, g_rows * GRID_W)
        qg = q_ref[pl.ds(q0, g_rows * GRID_W), :]
        blocks = []
        for gi in range(g_rows):
            qr = qg[gi * GRID_W:(gi + 1) * GRID_W]
            blocks += [qr * m_first, qr * m_second]
        qbd = jnp.concatenate(blocks, axis=0)
        s_ctx = _dot_nt(qbd, kc_ref[...])
        m = jnp.max(s_ctx, axis=-1, keepdims=True)
        if local:
            k0s, parts = [], []
            for gi in range(g_rows):
                r = r_base + gi
                rs = jnp.clip(r - NA_ROWS // 2, 0, n_rows - NA_ROWS)
                k0s.append(pl.multiple_of(rs * GRID_W, GRID_W))
                parts.append(_dot_nt(qbd[gi * hq:(gi + 1) * hq], k_ref[pl.ds(k0s[gi], n_loc), :])
                             + bias_ref[rs - r + NA_ROWS - 1])
            s_loc = jnp.concatenate(parts, axis=0)
            m = jnp.maximum(m, jnp.max(s_loc, axis=-1, keepdims=True))
            e_loc = jnp.exp(s_loc - m)
        e_ctx = jnp.exp(s_ctx - m)
        denom = jnp.sum(e_ctx, axis=-1, keepdims=True)
        full = _dot(e_ctx.astype(BF16), vc_ref[...])
        if local:
            denom = denom + jnp.sum(e_loc, axis=-1, keepdims=True)
            e_loc = e_loc.astype(BF16)
            full = full + jnp.concatenate(
                [_dot(e_loc[gi * hq:(gi + 1) * hq], v_ref[pl.ds(k0s[gi], n_loc), :])
                 for gi in range(g_rows)], axis=0)
        full = full * (1.0 / denom)
        out = jnp.concatenate(
            [jnp.where(lane_out < C_DH, full[gi * hq:gi * hq + GRID_W], full[gi * hq + GRID_W:(gi + 1) * hq])
             for gi in range(g_rows)], axis=0)
        o_ref[pl.ds(q0, g_rows * GRID_W), :] = out.astype(o_ref.dtype)
        return carry

    lax.fori_loop(0, n_rows // g_rows, body, 0)


def _nbr_attn(q, k, v, kc, vc, bias_t):
    b, lq, wdt = q.shape
    lc = kc.shape[1]
    n_rows = lq // GRID_W
    local = k is not None
    pair = lambda n: pl.BlockSpec((None, n, V7X_LANES), lambda h, bi: (bi, 0, h))
    if local:
        in_specs = [pair(lq), pair(lq), pair(lq), pair(lc), pair(lc),
                    pl.BlockSpec((None,) + bias_t.shape[1:], lambda h, bi: (h, 0, 0, 0))]
        args = [q, k, v, kc, vc, bias_t]
    else:
        in_specs = [pair(lq), pair(lc), pair(lc)]
        args = [q, kc, vc]
    return pl.pallas_call(
        functools.partial(_nbr_attn_kernel, n_rows=n_rows, g_rows=min(NBR_ROWS_PER_STEP, n_rows), local=local),
        grid=(wdt // V7X_LANES, b),
        in_specs=in_specs,
        out_specs=pair(lq),
        out_shape=jax.ShapeDtypeStruct((b, lq, wdt), BF16),
        compiler_params=_params(2, 32 * 1024 * 1024),
        name="nbr_attn" if local else "ctx_attn",
    )(*args)


def _nbr_bias_table(rpb):
    qc = np.arange(GRID_W)[:, None]
    kc = np.arange(GRID_W)[None, :]
    win0 = np.clip(qc - NA_COLS // 2, 0, GRID_W - NA_COLS)
    valid = (kc >= win0) & (kc < win0 + NA_COLS)
    dc = np.clip(kc - qc + NA_COLS - 1, 0, 2 * NA_COLS - 2)
    t = jnp.where(valid[None, None], rpb[:, :, dc], -jnp.inf)
    h = t.shape[0]
    dr = np.arange(NA_ROWS)[:, None] + np.arange(NA_ROWS)[None, :]
    t = t[:, dr]
    t = t.reshape(h // 2, 2, NA_ROWS, NA_ROWS, GRID_W, GRID_W).transpose(0, 2, 1, 4, 3, 5)
    return t.reshape(h // 2, NA_ROWS, 2 * GRID_W, NA_ROWS * GRID_W)


def _gating_kernel(u_ref, v_ref, w_ref, b_ref, o_ref, *, tg):
    n_groups = w_ref.shape[0]
    for n in range(tg // CHUNK):
        rows = slice(n * CHUNK, (n + 1) * CHUNK)
        for g in range(n_groups):
            lanes = slice(g * V7X_LANES, (g + 1) * V7X_LANES)
            sv = _dot(w_ref[g], v_ref[rows, lanes]) + b_ref[:, lanes]
            o_ref[rows, lanes] = (u_ref[rows, lanes] * sv).astype(o_ref.dtype)


def _gating(u, v, w_s, b_full):
    b, l, wdt = u.shape
    tg = min(ROW_TILE, l)
    tile = pl.BlockSpec((None, tg, wdt), lambda bi, i: (bi, i, 0))
    return pl.pallas_call(
        functools.partial(_gating_kernel, tg=tg),
        grid=(b, l // tg),
        in_specs=[tile, tile, _resident(w_s.shape), _resident(b_full.shape)],
        out_specs=tile,
        out_shape=jax.ShapeDtypeStruct((b, l, wdt), BF16),
        compiler_params=_params(2, 32 * 1024 * 1024),
        name="gating",
    )(u, v, w_s, b_full)


def _merge_kernel(x_ref, sh_ref, sc_ref, g1_ref, ng_ref, ya_ref, yb_ref, yc_ref, yd_ref,
                  wg_ref, wb_ref, wo_ref, o_ref):
    x = x_ref[...]
    d = x.shape[-1]
    hb = _norm_modulate(x, ng_ref[...], sh_ref[...], sc_ref[...])
    acc = None
    for i, y_ref in enumerate((ya_ref, yb_ref, yc_ref, yd_ref)):
        gate = _dot(hb, wg_ref[:, i * d:(i + 1) * d])
        term = jax.nn.sigmoid(gate) * _dot(y_ref[...], wb_ref[i * SEG_W:(i + 1) * SEG_W, :])
        acc = term if acc is None else acc + term
    o_ref[...] = x + g1_ref[...] * _dot(acc.astype(BF16), wo_ref[...])


def _merge(x, shift, scale, g1, norm_g, ys, w_gate, w_branch, w_out):
    b, l, d = x.shape
    tm = min(ROW_TILE, l)
    batched = shift.shape[0] != 1
    mod_map = (lambda bi, i: (bi, 0, 0)) if batched else (lambda bi, i: (0, 0, 0))
    x_tile = pl.BlockSpec((None, tm, d), lambda bi, i: (bi, i, 0))
    y_tile = pl.BlockSpec((None, tm, SEG_W), lambda bi, i: (bi, i, 0))
    mod = pl.BlockSpec((None, 1, d), mod_map)
    weights = (w_gate.size + w_branch.size + w_out.size) * 2
    vmem = weights + 2 * (2 * tm * d * 4 + 4 * tm * SEG_W * 2) + 8 * tm * d * 4 + (8 << 20)
    return pl.pallas_call(
        _merge_kernel,
        grid=(b, l // tm),
        in_specs=[x_tile, mod, mod, mod, _resident((1, d)), y_tile, y_tile, y_tile, y_tile,
                  _resident(w_gate.shape), _resident(w_branch.shape), _resident(w_out.shape)],
        out_specs=x_tile,
        out_shape=jax.ShapeDtypeStruct(x.shape, F32),
        compiler_params=_params(2, vmem),
        name="merge",
    )(x, shift, scale, g1, norm_g, *ys, w_gate, w_branch, w_out)


def _ffn_kernel(x_ref, sh_ref, sc_ref, g2_ref, ng_ref, wgu_ref, wd_ref, o_ref, u_ref):
    x = x_ref[...]
    hb = _norm_modulate(x, ng_ref[...], sh_ref[...], sc_ref[...])
    hidden = wd_ref.shape[0]
    for c in range(hidden // FFN_CHUNK):
        cols = slice(c * FFN_CHUNK, (c + 1) * FFN_CHUNK)
        a = _dot(hb, wgu_ref[:, cols])
        bb = _dot(hb, wgu_ref[:, hidden + c * FFN_CHUNK:hidden + (c + 1) * FFN_CHUNK])
        u_ref[:, cols] = (a * jax.nn.sigmoid(a) * bb).astype(BF16)
    o_ref[...] = x + g2_ref[...] * _dot(u_ref[...], wd_ref[...])


def _ffn(x, shift, scale, g2, norm_g, w_gu, w_down):
    b, l, d = x.shape
    tm = min(ROW_TILE, l)
    hidden = w_down.shape[0]
    batched = shift.shape[0] != 1
    mod_map = (lambda bi, i: (bi, 0, 0)) if batched else (lambda bi, i: (0, 0, 0))
    x_tile = pl.BlockSpec((None, tm, d), lambda bi, i: (bi, i, 0))
    mod = pl.BlockSpec((None, 1, d), mod_map)
    vmem = (w_gu.size + w_down.size) * 2 + 4 * tm * d * 4 + tm * hidden * 2 + 8 * tm * d * 4 + (8 << 20)
    return pl.pallas_call(
        _ffn_kernel,
        grid=(b, l // tm),
        in_specs=[x_tile, mod, mod, mod, _resident((1, d)),
                  _resident(w_gu.shape), _resident(w_down.shape)],
        out_specs=x_tile,
        out_shape=jax.ShapeDtypeStruct(x.shape, F32),
        scratch_shapes=[pltpu.VMEM((tm, hidden), BF16)],
        compiler_params=_params(2, vmem),
        name="ffn",
    )(x, shift, scale, g2, norm_g, w_gu, w_down)


def _rope_tables(n_tok):
    nf = A_DH // 4
    t = jnp.arange(n_tok)
    row = (t // GRID_W).astype(F32)
    col = (t % GRID_W).astype(F32)
    inv = ROPE_THETA ** (-jnp.arange(nf, dtype=F32) / nf)
    ar = row[:, None] * inv
    ac = col[:, None] * inv
    ang = jnp.concatenate([ar, ar, ac, ac], axis=-1)
    cos, sin = jnp.cos(ang), jnp.sin(ang)
    quarter = (np.arange(A_DH) // nf) % 2
    sin_up = jnp.where(quarter == 0, -sin, 0.0)
    sin_dn = jnp.where(quarter == 1, sin, 0.0)
    rep = V7X_LANES // A_DH
    return tuple(jnp.tile(a, (1, rep)) for a in (cos, sin_up, sin_dn))


def _group_matrix():
    g = np.arange(SEG_W) // A_DH
    return jnp.asarray(g[:, None] == g[None, :], dtype=BF16)


def kernel(x, c, ctx, c_ctx, w_mod, b_mod, norm1_g, w_in, a_qk_g, a_lambda, a_subln_g, b_pool_w,
           b_pool_s, c_qk_g, c_rpb, d_vn_g, d_ws, d_bs, w_branch, w_out, norm2_g, w_gu, w_down):
    b, s, d = x.shape
    depth = w_mod.shape[0]
    rope_tabs = _rope_tables(s)
    grp = _group_matrix()

    rows = -(-(b + 1) // 8) * 8
    c_all = jnp.zeros((rows, d), F32).at[:b].set(c).at[b].set(c_ctx)
    mod = _modulation(c_all, w_mod, b_mod)

    for l in range(depth):
        last = l == depth - 1
        lam_init = 0.8 - 0.6 * math.exp(-0.3 * l)
        mx = [mod[l, :b, k * d:(k + 1) * d].reshape(b, 1, d) for k in range(6)]
        mc = [mod[l, b:b + 1, k * d:(k + 1) * d].reshape(1, 1, d) for k in range(6)]
        w_mix = w_in[l][:, :MIX_COLS].astype(BF16)
        w_gate = w_in[l][:, MIX_COLS:].astype(BF16)
        w_br = w_branch[l].astype(BF16)
        w_o = w_out[l].astype(BF16)
        w_gu_l = w_gu[l].astype(BF16)
        w_dn = w_down[l].astype(BF16)
        w_pool = b_pool_w[l].astype(BF16)
        w_s = d_ws[l].astype(BF16)
        n1 = norm1_g[l].reshape(1, d)
        n2 = norm2_g[l].reshape(1, d)
        rep = SEG_W // A_DH
        qk_gains = jnp.stack([jnp.tile(a_qk_g[l, 0], rep), jnp.tile(a_qk_g[l, 1], rep),
                              jnp.tile(c_qk_g[l, 0], rep), jnp.tile(c_qk_g[l, 1], rep)])
        vn_g = d_vn_g[l].reshape(1, SEG_W)
        bs_full = jnp.repeat(d_bs[l].T, V7X_LANES, axis=1)
        bias_t = _nbr_bias_table(c_rpb[l])

        px = _in_proj(x, mx[0], mx[1], n1, w_mix, grp, qk_gains, vn_g, rope_tabs)
        pc = _in_proj(ctx, mc[0], mc[1], n1, w_mix, grp, qk_gains, vn_g, None)

        ya = _diff_attn(px[SEG_AQ], [(px[SEG_AK], px[SEG_AV]), (pc[SEG_AK], pc[SEG_AV])],
                        a_lambda[l], a_subln_g[l], lam_init)
        yb = _pool(px[SEG_B], w_pool, b_pool_s[l])
        yc = _nbr_attn(px[SEG_CQ], px[SEG_CK], px[SEG_CV], pc[SEG_CK], pc[SEG_CV], bias_t)
        yd = _gating(px[SEG_DU], px[SEG_DV], w_s, bs_full)
        x = _merge(x, mx[0], mx[1], mx[2], n1, (ya, yb, yc, yd), w_gate, w_br, w_o)
        x = _ffn(x, mx[3], mx[4], mx[5], n2, w_gu_l, w_dn)

        if not last:
            ya_c = _diff_attn(pc[SEG_AQ], [(pc[SEG_AK], pc[SEG_AV])], a_lambda[l], a_subln_g[l], lam_init)
            yb_c = _pool(pc[SEG_B], w_pool, b_pool_s[l])
            yc_c = _nbr_attn(pc[SEG_CQ], None, None, pc[SEG_CK], pc[SEG_CV], None)
            yd_c = _gating(pc[SEG_DU], pc[SEG_DV], w_s, bs_full)
            ctx = _merge(ctx, mc[0], mc[1], mc[2], n1, (ya_c, yb_c, yc_c, yd_c), w_gate, w_br, w_o)
            ctx = _ffn(ctx, mc[3], mc[4], mc[5], n2, w_gu_l, w_dn)
    return x
```

```python
import functools
import math

import jax
import jax.numpy as jnp
import numpy as np
from jax import lax
from jax.experimental import pallas as pl
from jax.experimental.pallas import tpu as pltpu

F32 = jnp.float32
BF16 = jnp.bfloat16

D_MODEL = 1024
GRID_W = 64
EPS = 1e-6
ROPE_THETA = 10000.0
A_HEADS = 4
A_DH = 64
POOL_WINDOWS = (2, 4, 8, 16)
C_DH = 64
NA_ROWS = 8
NA_COLS = 16
CHUNK = 128
N_BRANCH = 4
SEG_W = 512
N_SEG = 9
MIX_COLS = N_SEG * SEG_W
FFN_HIDDEN = 2816
SEG_AQ, SEG_AK, SEG_AV, SEG_B, SEG_CQ, SEG_CK, SEG_CV, SEG_DU, SEG_DV = range(N_SEG)
SEG_DTYPES = (BF16, BF16, BF16, F32, BF16, BF16, BF16, F32, BF16)

V7X_LANES = 128
V7X_VMEM_BYTES = 64 * 1024 * 1024
V7X_VMEM_LIMIT = V7X_VMEM_BYTES - 8 * 1024 * 1024

ROW_TILE = 1024
ATTN_Q_TILE = 256
ATTN_TILES_PER_STEP = 4
POOL_TILE = 256
POOL_HALO = 8
FFN_CHUNK = 256
NBR_ROWS_PER_STEP = 8


def _params(n_axes, vmem_bytes):
    return pltpu.CompilerParams(
        dimension_semantics=("parallel",) * n_axes,
        vmem_limit_bytes=int(min(vmem_bytes, V7X_VMEM_LIMIT)))


def _resident(shape):
    zeros = (0,) * len(shape)
    return pl.BlockSpec(shape, lambda *_: zeros, pipeline_mode=pl.Buffered(1))


def _dot(a, b):
    return jnp.dot(a, b, preferred_element_type=F32)


def _dot_nt(a, b):
    return lax.dot_general(a, b, (((1,), (1,)), ((), ())), preferred_element_type=F32)


def _dot_tn(a, b):
    return lax.dot_general(a, b, (((0,), (0,)), ((), ())), preferred_element_type=F32)


def _split_bf16(t):
    hi = t.astype(BF16)
    lo = (t - hi.astype(F32)).astype(BF16)
    return hi, lo


def _norm_modulate(x, gain, shift, scale):
    ms = jnp.mean(x * x, axis=-1, keepdims=True)
    n = x * lax.rsqrt(ms + EPS)
    return ((n * gain) * (1.0 + scale) + shift).astype(BF16)


def _mod_kernel(c_ref, w_ref, b_ref, o_ref):
    a = c_ref[...]
    a = a * jax.nn.sigmoid(a)
    a_hi, a_lo = _split_bf16(a)
    w_hi, w_lo = _split_bf16(w_ref[...])
    o_ref[...] = _dot(a_hi, w_hi) + _dot(a_hi, w_lo) + _dot(a_lo, w_hi) + b_ref[...]


def _modulation(c_all, w_mod, b_mod):
    depth, d, n = w_mod.shape
    rows = c_all.shape[0]
    tn = 768
    return pl.pallas_call(
        _mod_kernel,
        grid=(depth, n // tn),
        in_specs=[
            pl.BlockSpec((rows, d), lambda l, j: (0, 0)),
            pl.BlockSpec((None, d, tn), lambda l, j: (l, 0, j)),
            pl.BlockSpec((None, 1, tn), lambda l, j: (l, 0, j)),
        ],
        out_specs=pl.BlockSpec((None, rows, tn), lambda l, j: (l, 0, j)),
        out_shape=jax.ShapeDtypeStruct((depth, rows, n), F32),
        compiler_params=_params(2, 32 * 1024 * 1024),
        name="modulation",
    )(c_all, w_mod, b_mod.reshape(depth, 1, n))


def _group_rms(r, grp, gain):
    ms = _dot((r * r).astype(BF16), grp) * (1.0 / A_DH)
    return r * lax.rsqrt(ms + EPS) * gain


def _rope_block(t, cos, sin_up, sin_dn):
    return (t * cos + pltpu.roll(t, V7X_LANES - 16, 1) * sin_up + pltpu.roll(t, 16, 1) * sin_dn)


def _in_proj_kernel(*refs, rope):
    x_ref, sh_ref, sc_ref, ng_ref, w_ref, grp_ref, qkg_ref, vng_ref = refs[:8]
    pos = 8
    if rope:
        cos_ref, sup_ref, sdn_ref = refs[8:11]
        pos = 11
    outs = refs[pos:pos + N_SEG]
    hb = _norm_modulate(x_ref[...], ng_ref[...], sh_ref[...], sc_ref[...])
    qk_row = {SEG_AQ: 0, SEG_AK: 1, SEG_CQ: 2, SEG_CK: 3}
    for seg in range(N_SEG):
        r = _dot(hb, w_ref[:, seg * SEG_W:(seg + 1) * SEG_W])
        if seg in qk_row:
            row = qk_row[seg]
            r = _group_rms(r, grp_ref[...], qkg_ref[row:row + 1, :])
            if rope and seg in (SEG_AQ, SEG_AK):
                cos, sup, sdn = cos_ref[...], sup_ref[...], sdn_ref[...]
                r = jnp.concatenate(
                    [_rope_block(r[:, j * V7X_LANES:(j + 1) * V7X_LANES], cos, sup, sdn)
                     for j in range(SEG_W // V7X_LANES)], axis=1)
        elif seg == SEG_DV:
            ms = jnp.mean(r * r, axis=-1, keepdims=True)
            r = r * lax.rsqrt(ms + EPS) * vng_ref[...]
        outs[seg][...] = r.astype(outs[seg].dtype)


def _in_proj(x, shift, scale, norm_g, w_mix, grp, qk_gains, vn_g, rope_tabs):
    b, l, d = x.shape
    tm = min(ROW_TILE, l)
    batched = shift.shape[0] != 1
    mod_map = (lambda bi, i: (bi, 0, 0)) if batched else (lambda bi, i: (0, 0, 0))
    rope = rope_tabs is not None
    in_specs = [
        pl.BlockSpec((None, tm, d), lambda bi, i: (bi, i, 0)),
        pl.BlockSpec((None, 1, d), mod_map),
        pl.BlockSpec((None, 1, d), mod_map),
        _resident((1, d)),
        _resident((d, MIX_COLS)),
        _resident((SEG_W, SEG_W)),
        _resident((4, SEG_W)),
        _resident((1, SEG_W)),
    ]
    args = [x, shift, scale, norm_g, w_mix, grp, qk_gains, vn_g]
    if rope:
        in_specs += [pl.BlockSpec((tm, V7X_LANES), lambda bi, i: (i, 0))] * 3
        args += list(rope_tabs)
    out_bytes = sum(tm * SEG_W * jnp.dtype(t).itemsize for t in SEG_DTYPES)
    vmem = d * MIX_COLS * 2 + 2 * (tm * d * 4 + out_bytes) + 8 * tm * SEG_W * 4 + (8 << 20)
    return pl.pallas_call(
        functools.partial(_in_proj_kernel, rope=rope),
        grid=(b, l // tm),
        in_specs=in_specs,
        out_specs=[pl.BlockSpec((None, tm, SEG_W), lambda bi, i: (bi, i, 0))] * N_SEG,
        out_shape=[jax.ShapeDtypeStruct((b, l, SEG_W), t) for t in SEG_DTYPES],
        compiler_params=_params(2, vmem),
        name="in_proj_rope" if rope else "in_proj",
    )(*args)


def _diff_attn_kernel(*refs, n_parts, lq, tq, tiles_per_step, lam_init):
    al_ref, sg_ref, q_ref = refs[:3]
    kv = refs[3:3 + 2 * n_parts]
    o_ref, kcat_ref, vt_ref = refs[3 + 2 * n_parts:]
    al = al_ref[...]
    lam = (jnp.exp(jnp.sum(al[0:1] * al[1:2], axis=-1, keepdims=True))
           - jnp.exp(jnp.sum(al[2:3] * al[3:4], axis=-1, keepdims=True)) + lam_init)
    lane = lax.broadcasted_iota(jnp.int32, (1, V7X_LANES), 1)
    scale = A_DH ** -0.5
    masks = (jnp.where(lane < A_DH, scale, 0.0).astype(BF16),
             jnp.where(lane >= A_DH, scale, 0.0).astype(BF16))
    sub_gain = sg_ref[...] * (1.0 - lam_init)

    off = 0
    for p in range(n_parts):
        n = kv[2 * p].shape[0]
        kcat_ref[off:off + n, :] = kv[2 * p][...]
        vt_ref[:, off:off + n] = kv[2 * p + 1][...].astype(F32).T.astype(BF16)
        off += n

    def scores(r0):
        qc = q_ref[pl.ds(r0, tq), :]
        return [_dot_nt(kcat_ref[...], qc * mask) for mask in masks]

    def finish(r0, s):
        acc, inv = [], []
        for sc in s:
            e = jnp.exp(sc - jnp.max(sc, axis=0, keepdims=True))
            inv.append(1.0 / jnp.sum(e, axis=0, keepdims=True))
            acc.append(_dot(vt_ref[...], e.astype(BF16)))
        o = (acc[0] * inv[0] - acc[1] * (inv[1] * lam)).T
        ms = jnp.mean(o * o, axis=-1, keepdims=True)
        o_ref[pl.ds(r0, tq), :] = (o * lax.rsqrt(ms + EPS) * sub_gain).astype(o_ref.dtype)

    def body(it, carry):
        starts = [pl.multiple_of((it * tiles_per_step + j) * tq, tq) for j in range(tiles_per_step)]
        s_next = scores(starts[0])
        for j in range(tiles_per_step):
            s_cur = s_next
            if j + 1 < tiles_per_step:
                s_next = scores(starts[j + 1])
            finish(starts[j], s_cur)
        return carry

    lax.fori_loop(0, lq // (tq * tiles_per_step), body, 0)


def _diff_attn(q, kv_parts, a_lambda, subln_g, lam_init):
    b, lq, _ = q.shape
    tq = min(ATTN_Q_TILE, lq)
    hw = 2 * A_DH
    head_spec = lambda n: pl.BlockSpec((None, n, hw), lambda bi, h: (bi, 0, h))
    in_specs = [_resident(a_lambda.shape), _resident((1, hw)), head_spec(lq)]
    args = [a_lambda, subln_g.reshape(1, hw), q]
    lk = 0
    for k, v in kv_parts:
        in_specs += [head_spec(k.shape[1]), head_spec(v.shape[1])]
        args += [k, v]
        lk += k.shape[1]
    vmem = 2 * (2 * lq + 4 * lk) * hw * 2 + 2 * lk * hw * 2 + 10 * tq * lk * 4 + (8 << 20)
    return pl.pallas_call(
        functools.partial(_diff_attn_kernel, n_parts=len(kv_parts), lq=lq, tq=tq,
                          tiles_per_step=min(ATTN_TILES_PER_STEP, lq // tq), lam_init=lam_init),
        grid=(b, A_HEADS),
        in_specs=in_specs,
        out_specs=head_spec(lq),
        out_shape=jax.ShapeDtypeStruct((b, lq, A_HEADS * hw), BF16),
        scratch_shapes=[pltpu.VMEM((lk, hw), BF16), pltpu.VMEM((hw, lk), BF16)],
        compiler_params=_params(2, vmem),
        name="diff_attn",
    )(*args)


def _tree_sum(terms):
    while len(terms) > 1:
        terms = [terms[i] + terms[i + 1] for i in range(0, len(terms), 2)]
    return terms[0]


def _pool_kernel(p_ref, w_ref, s_ref, o_ref, pad_ref, *, l, tp):
    wdt = p_ref.shape[-1]
    pad_ref[0:POOL_HALO, :] = jnp.zeros((POOL_HALO, wdt), F32)
    pad_ref[POOL_HALO + l:, :] = jnp.zeros((POOL_HALO, wdt), F32)
    pad_ref[POOL_HALO:POOL_HALO + l, :] = p_ref[...]
    n_tiles = l // tp
    for ti in range(n_tiles):
        t0 = ti * tp
        edge = ti == 0 or ti == n_tiles - 1
        if edge:
            t_row = t0 + lax.broadcasted_iota(jnp.int32, (tp, V7X_LANES), 0)
        for g, w in enumerate(POOL_WINDOWS):
            lanes = slice(g * V7X_LANES, (g + 1) * V7X_LANES)
            shifted = [pad_ref[POOL_HALO + t0 + j:POOL_HALO + t0 + j + tp, lanes]
                       for j in range(-(w // 2), w // 2)]
            centre = shifted[w // 2]
            total = _tree_sum(shifted)
            if edge:
                cnt = (jnp.minimum(t_row + w // 2, l) - jnp.maximum(t_row - w // 2, 0)).astype(F32)
                mean = total / cnt
            else:
                mean = total * (1.0 / w)
            y = _dot((mean - centre).astype(BF16), w_ref[g]) * s_ref[:, lanes]
            o_ref[t0:t0 + tp, lanes] = y.astype(o_ref.dtype)


def _pool(p, w_pool, s_pool):
    b, l, wdt = p.shape
    tp = min(POOL_TILE, l)
    seq = pl.BlockSpec((None, l, wdt), lambda bi: (bi, 0, 0))
    return pl.pallas_call(
        functools.partial(_pool_kernel, l=l, tp=tp),
        grid=(b,),
        in_specs=[seq, _resident(w_pool.shape), _resident((1, wdt))],
        out_specs=seq,
        out_shape=jax.ShapeDtypeStruct((b, l, wdt), BF16),
        scratch_shapes=[pltpu.VMEM((l + 2 * POOL_HALO, wdt), F32)],
        compiler_params=_params(1, 5 * l * wdt * 4 + (16 << 20)),
        name="pool",
    )(p, w_pool, s_pool.reshape(1, wdt))


def _nbr_attn_kernel(*refs, n_rows, g_rows, local):
    if local:
        q_ref, k_ref, v_ref, kc_ref, vc_ref, bias_ref, o_ref = refs
    else:
        q_ref, kc_ref, vc_ref, o_ref = refs
    lane = lax.broadcasted_iota(jnp.int32, (1, V7X_LANES), 1)
    scale = C_DH ** -0.5
    m_first = jnp.where(lane < C_DH, scale, 0.0).astype(BF16)
    m_second = jnp.where(lane >= C_DH, scale, 0.0).astype(BF16)
    lane_out = lax.broadcasted_iota(jnp.int32, (GRID_W, V7X_LANES), 1)
    n_loc = NA_ROWS * GRID_W
    hq = 2 * GRID_W

    def body(it, carry):
        r_base = it * g_rows
        q0 = pl.multiple_of(r_base * GRID_W, g_rows * GRID_W)
        qg = q_ref[pl.ds(q0, g_rows * GRID_W), :]
        blocks = []
        for gi in range(g_rows):
            qr = qg[gi * GRID_W:(gi + 1) * GRID_W]
            blocks += [qr * m_first, qr * m_second]
        qbd = jnp.concatenate(blocks, axis=0)
        s_ctx = _dot_nt(qbd, kc_ref[...])
        m = jnp.max(s_ctx, axis=-1, keepdims=True)
        if local:
            k0s, parts = [], []
            for gi in range(g_rows):
                r = r_base + gi
                rs = jnp.clip(r - NA_ROWS // 2, 0, n_rows - NA_ROWS)
                k0s.append(pl.multiple_of(rs * GRID_W, GRID_W))
                parts.append(_dot_nt(qbd[gi * hq:(gi + 1) * hq], k_ref[pl.ds(k0s[gi], n_loc), :])
                             + bias_ref[rs - r + NA_ROWS - 1])
            s_loc = jnp.concatenate(parts, axis=0)
            m = jnp.maximum(m, jnp.max(s_loc, axis=-1, keepdims=True))
            e_loc = jnp.exp(s_loc - m)
        e_ctx = jnp.exp(s_ctx - m)
        denom = jnp.sum(e_ctx, axis=-1, keepdims=True)
        full = _dot(e_ctx.astype(BF16), vc_ref[...])
        if local:
            denom = denom + jnp.sum(e_loc, axis=-1, keepdims=True)
            e_loc = e_loc.astype(BF16)
            full = full + jnp.concatenate(
                [_dot(e_loc[gi * hq:(gi + 1) * hq], v_ref[pl.ds(k0s[gi], n_loc), :])
                 for gi in range(g_rows)], axis=0)
        full = full * (1.0 / denom)
        out = jnp.concatenate(
            [jnp.where(lane_out < C_DH, full[gi * hq:gi * hq + GRID_W], full[gi * hq + GRID_W:(gi + 1) * hq])
             for gi in range(g_rows)], axis=0)
        o_ref[pl.ds(q0, g_rows * GRID_W), :] = out.astype(o_ref.dtype)
        return carry

    lax.fori_loop(0, n_rows // g_rows, body, 0)


def _nbr_attn(q, k, v, kc, vc, bias_t):
    b, lq, wdt = q.shape
    lc = kc.shape[1]
    n_rows = lq // GRID_W
    local = k is not None
    pair = lambda n: pl.BlockSpec((None, n, V7X_LANES), lambda h, bi: (bi, 0, h))
    if local:
        in_specs = [pair(lq), pair(lq), pair(lq), pair(lc), pair(lc),
                    pl.BlockSpec((None,) + bias_t.shape[1:], lambda h, bi: (h, 0, 0, 0))]
        args = [q, k, v, kc, vc, bias_t]
    else:
        in_specs = [pair(lq), pair(lc), pair(lc)]
        args = [q, kc, vc]
    return pl.pallas_call(
        functools.partial(_nbr_attn_kernel, n_rows=n_rows, g_rows=min(NBR_ROWS_PER_STEP, n_rows), local=local),
        grid=(wdt // V7X_LANES, b),
        in_specs=in_specs,
        out_specs=pair(lq),
        out_shape=jax.ShapeDtypeStruct((b, lq, wdt), BF16),
        compiler_params=_params(2, 32 * 1024 * 1024),
        name="nbr_attn" if local else "ctx_attn",
    )(*args)


def _nbr_bias_table(rpb):
    qc = np.arange(GRID_W)[:, None]
    kc = np.arange(GRID_W)[None, :]
    win0 = np.clip(qc - NA_COLS // 2, 0, GRID_W - NA_COLS)
    valid = (kc >= win0) & (kc < win0 + NA_COLS)
    dc = np.clip(kc - qc + NA_COLS - 1, 0, 2 * NA_COLS - 2)
    t = jnp.where(valid[None, None], rpb[:, :, dc], -jnp.inf)
    h = t.shape[0]
    dr = np.arange(NA_ROWS)[:, None] + np.arange(NA_ROWS)[None, :]
    t = t[:, dr]
    t = t.reshape(h // 2, 2, NA_ROWS, NA_ROWS, GRID_W, GRID_W).transpose(0, 2, 1, 4, 3, 5)
    return t.reshape(h // 2, NA_ROWS, 2 * GRID_W, NA_ROWS * GRID_W)


def _gating_kernel(u_ref, v_ref, w_ref, b_ref, o_ref, *, tg):
    n_groups = w_ref.shape[0]
    for n in range(tg // CHUNK):
        rows = slice(n * CHUNK, (n + 1) * CHUNK)
        for g in range(n_groups):
            lanes = slice(g * V7X_LANES, (g + 1) * V7X_LANES)
            sv = _dot(w_ref[g], v_ref[rows, lanes]) + b_ref[:, lanes]
            o_ref[rows, lanes] = (u_ref[rows, lanes] * sv).astype(o_ref.dtype)


def _gating(u, v, w_s, b_full):
    b, l, wdt = u.shape
    tg = min(ROW_TILE, l)
    tile = pl.BlockSpec((None, tg, wdt), lambda bi, i: (bi, i, 0))
    return pl.pallas_call(
        functools.partial(_gating_kernel, tg=tg),
        grid=(b, l // tg),
        in_specs=[tile, tile, _resident(w_s.shape), _resident(b_full.shape)],
        out_specs=tile,
        out_shape=jax.ShapeDtypeStruct((b, l, wdt), BF16),
        compiler_params=_params(2, 32 * 1024 * 1024),
        name="gating",
    )(u, v, w_s, b_full)


def _merge_kernel(x_ref, sh_ref, sc_ref, g1_ref, ng_ref, ya_ref, yb_ref, yc_ref, yd_ref,
                  wg_ref, wb_ref, wo_ref, o_ref):
    x = x_ref[...]
    d = x.shape[-1]
    hb = _norm_modulate(x, ng_ref[...], sh_ref[...], sc_ref[...])
    acc = None
    for i, y_ref in enumerate((ya_ref, yb_ref, yc_ref, yd_ref)):
        gate = _dot(hb, wg_ref[:, i * d:(i + 1) * d])
        term = jax.nn.sigmoid(gate) * _dot(y_ref[...], wb_ref[i * SEG_W:(i + 1) * SEG_W, :])
        acc = term if acc is None else acc + term
    o_ref[...] = x + g1_ref[...] * _dot(acc.astype(BF16), wo_ref[...])


def _merge(x, shift, scale, g1, norm_g, ys, w_gate, w_branch, w_out):
    b, l, d = x.shape
    tm = min(ROW_TILE, l)
    batched = shift.shape[0] != 1
    mod_map = (lambda bi, i: (bi, 0, 0)) if batched else (lambda bi, i: (0, 0, 0))
    x_tile = pl.BlockSpec((None, tm, d), lambda bi, i: (bi, i, 0))
    y_tile = pl.BlockSpec((None, tm, SEG_W), lambda bi, i: (bi, i, 0))
    mod = pl.BlockSpec((None, 1, d), mod_map)
    weights = (w_gate.size + w_branch.size + w_out.size) * 2
    vmem = weights + 2 * (2 * tm * d * 4 + 4 * tm * SEG_W * 2) + 8 * tm * d * 4 + (8 << 20)
    return pl.pallas_call(
        _merge_kernel,
        grid=(b, l // tm),
        in_specs=[x_tile, mod, mod, mod, _resident((1, d)), y_tile, y_tile, y_tile, y_tile,
                  _resident(w_gate.shape), _resident(w_branch.shape), _resident(w_out.shape)],
        out_specs=x_tile,
        out_shape=jax.ShapeDtypeStruct(x.shape, F32),
        compiler_params=_params(2, vmem),
        name="merge",
    )(x, shift, scale, g1, norm_g, *ys, w_gate, w_branch, w_out)


def _ffn_kernel(x_ref, sh_ref, sc_ref, g2_ref, ng_ref, wgu_ref, wd_ref, o_ref, u_ref):
    x = x_ref[...]
    hb = _norm_modulate(x, ng_ref[...], sh_ref[...], sc_ref[...])
    hidden = wd_ref.shape[0]
    for c in range(hidden // FFN_CHUNK):
        cols = slice(c * FFN_CHUNK, (c + 1) * FFN_CHUNK)
        a = _dot(hb, wgu_ref[:, cols])
        bb = _dot(hb, wgu_ref[:, hidden + c * FFN_CHUNK:hidden + (c + 1) * FFN_CHUNK])
        u_ref[:, cols] = (a * jax.nn.sigmoid(a) * bb).astype(BF16)
    o_ref[...] = x + g2_ref[...] * _dot(u_ref[...], wd_ref[...])


def _ffn(x, shift, scale, g2, norm_g, w_gu, w_down):
    b, l, d = x.shape
    tm = min(ROW_TILE, l)
    hidden = w_down.shape[0]
    batched = shift.shape[0] != 1
    mod_map = (lambda bi, i: (bi, 0, 0)) if batched else (lambda bi, i: (0, 0, 0))
    x_tile = pl.BlockSpec((None, tm, d), lambda bi, i: (bi, i, 0))
    mod = pl.BlockSpec((None, 1, d), mod_map)
    vmem = (w_gu.size + w_down.size) * 2 + 4 * tm * d * 4 + tm * hidden * 2 + 8 * tm * d * 4 + (8 << 20)
    return pl.pallas_call(
        _ffn_kernel,
        grid=(b, l // tm),
        in_specs=[x_tile, mod, mod, mod, _resident((1, d)),
                  _resident(w_gu.shape), _resident(w_down.shape)],
        out_specs=x_tile,
        out_shape=jax.ShapeDtypeStruct(x.shape, F32),
        scratch_shapes=[pltpu.VMEM((tm, hidden), BF16)],
        compiler_params=_params(2, vmem),
        name="ffn",
    )(x, shift, scale, g2, norm_g, w_gu, w_down)


def _rope_tables(n_tok):
    nf = A_DH // 4
    t = jnp.arange(n_tok)
    row = (t // GRID_W).astype(F32)
    col = (t % GRID_W).astype(F32)
    inv = ROPE_THETA ** (-jnp.arange(nf, dtype=F32) / nf)
    ar = row[:, None] * inv
    ac = col[:, None] * inv
    ang = jnp.concatenate([ar, ar, ac, ac], axis=-1)
    cos, sin = jnp.cos(ang), jnp.sin(ang)
    quarter = (np.arange(A_DH) // nf) % 2
    sin_up = jnp.where(quarter == 0, -sin, 0.0)
    sin_dn = jnp.where(quarter == 1, sin, 0.0)
    rep = V7X_LANES // A_DH
    return tuple(jnp.tile(a, (1, rep)) for a in (cos, sin_up, sin_dn))


def _group_matrix():
    g = np.arange(SEG_W) // A_DH
    return jnp.asarray(g[:, None] == g[None, :], dtype=BF16)


def kernel(x, c, ctx, c_ctx, w_mod, b_mod, norm1_g, w_in, a_qk_g, a_lambda, a_subln_g, b_pool_w,
           b_pool_s, c_qk_g, c_rpb, d_vn_g, d_ws, d_bs, w_branch, w_out, norm2_g, w_gu, w_down):
    b, s, d = x.shape
    depth = w_mod.shape[0]
    rope_tabs = _rope_tables(s)
    grp = _group_matrix()

    rows = -(-(b + 1) // 8) * 8
    c_all = jnp.zeros((rows, d), F32).at[:b].set(c).at[b].set(c_ctx)
    mod = _modulation(c_all, w_mod, b_mod)

    for l in range(depth):
        last = l == depth - 1
        lam_init = 0.8 - 0.6 * math.exp(-0.3 * l)
        mx = [mod[l, :b, k * d:(k + 1) * d].reshape(b, 1, d) for k in range(6)]
        mc = [mod[l, b:b + 1, k * d:(k + 1) * d].reshape(1, 1, d) for k in range(6)]
        w_mix = w_in[l][:, :MIX_COLS].astype(BF16)
        w_gate = w_in[l][:, MIX_COLS:].astype(BF16)
        w_br = w_branch[l].astype(BF16)
        w_o = w_out[l].astype(BF16)
        w_gu_l = w_gu[l].astype(BF16)
        w_dn = w_down[l].astype(BF16)
        w_pool = b_pool_w[l].astype(BF16)
        w_s = d_ws[l].astype(BF16)
        n1 = norm1_g[l].reshape(1, d)
        n2 = norm2_g[l].reshape(1, d)
        rep = SEG_W // A_DH
        qk_gains = jnp.stack([jnp.tile(a_qk_g[l, 0], rep), jnp.tile(a_qk_g[l, 1], rep),
                              jnp.tile(c_qk_g[l, 0], rep), jnp.tile(c_qk_g[l, 1], rep)])
        vn_g = d_vn_g[l].reshape(1, SEG_W)
        bs_full = jnp.repeat(d_bs[l].T, V7X_LANES, axis=1)
        bias_t = _nbr_bias_table(c_rpb[l])

        px = _in_proj(x, mx[0], mx[1], n1, w_mix, grp, qk_gains, vn_g, rope_tabs)
        pc = _in_proj(ctx, mc[0], mc[1], n1, w_mix, grp, qk_gains, vn_g, None)

        ya = _diff_attn(px[SEG_AQ], [(px[SEG_AK], px[SEG_AV]), (pc[SEG_AK], pc[SEG_AV])],
                        a_lambda[l], a_subln_g[l], lam_init)
        yb = _pool(px[SEG_B], w_pool, b_pool_s[l])
        yc = _nbr_attn(px[SEG_CQ], px[SEG_CK], px[SEG_CV], pc[SEG_CK], pc[SEG_CV], bias_t)
        yd = _gating(px[SEG_DU], px[SEG_DV], w_s, bs_full)
        x = _merge(x, mx[0], mx[1], mx[2], n1, (ya, yb, yc, yd), w_gate, w_br, w_o)
        x = _ffn(x, mx[3], mx[4], mx[5], n2, w_gu_l, w_dn)

        if not last:
            ya_c = _diff_attn(pc[SEG_AQ], [(pc[SEG_AK], pc[SEG_AV])], a_lambda[l], a_subln_g[l], lam_init)
            yb_c = _pool(pc[SEG_B], w_pool, b_pool_s[l])
            yc_c = _nbr_attn(pc[SEG_CQ], None, None, pc[SEG_CK], pc[SEG_CV], None)
            yd_c = _gating(pc[SEG_DU], pc[SEG_DV], w_s, bs_full)
            ctx = _merge(ctx, mc[0], mc[1], mc[2], n1, (ya_c, yb_c, yc_c, yd_c), w_gate, w_br, w_o)
            ctx = _ffn(ctx, mc[3], mc[4], mc[5], n2, w_gu_l, w_dn)
    return x
```

```python
import functools
import math

import jax
import jax.numpy as jnp
import numpy as np
from jax import lax
from jax.experimental import pallas as pl
from jax.experimental.pallas import tpu as pltpu

F32 = jnp.float32
BF16 = jnp.bfloat16

D_MODEL = 1024
GRID_W = 64
EPS = 1e-6
ROPE_THETA = 10000.0
A_HEADS = 4
A_DH = 64
POOL_WINDOWS = (2, 4, 8, 16)
C_DH = 64
NA_ROWS = 8
NA_COLS = 16
CHUNK = 128
N_BRANCH = 4
SEG_W = 512
N_SEG = 9
MIX_COLS = N_SEG * SEG_W
FFN_HIDDEN = 2816
SEG_AQ, SEG_AK, SEG_AV, SEG_B, SEG_CQ, SEG_CK, SEG_CV, SEG_DU, SEG_DV = range(N_SEG)
SEG_DTYPES = (BF16, BF16, BF16, F32, BF16, BF16, BF16, F32, BF16)

V7X_LANES = 128
V7X_VMEM_BYTES = 64 * 1024 * 1024
V7X_VMEM_LIMIT = V7X_VMEM_BYTES - 8 * 1024 * 1024

ROW_TILE = 1024
ATTN_Q_TILE = 256
ATTN_TILES_PER_STEP = 4
ATTN_KEY_CHUNK = 256
POOL_TILE = 256
POOL_HALO = 8
FFN_CHUNK = 256
NBR_ROWS_PER_STEP = 8


def _params(n_axes, vmem_bytes):
    return pltpu.CompilerParams(
        dimension_semantics=("parallel",) * n_axes,
        vmem_limit_bytes=int(min(vmem_bytes, V7X_VMEM_LIMIT)))


def _resident(shape):
    zeros = (0,) * len(shape)
    return pl.BlockSpec(shape, lambda *_: zeros, pipeline_mode=pl.Buffered(1))


def _dot(a, b):
    return jnp.dot(a, b, preferred_element_type=F32)


def _dot_nt(a, b):
    return lax.dot_general(a, b, (((1,), (1,)), ((), ())), preferred_element_type=F32)


def _dot_tn(a, b):
    return lax.dot_general(a, b, (((0,), (0,)), ((), ())), preferred_element_type=F32)


def _split_bf16(t):
    hi = t.astype(BF16)
    lo = (t - hi.astype(F32)).astype(BF16)
    return hi, lo


def _norm_modulate(x, gain, shift, scale):
    ms = jnp.mean(x * x, axis=-1, keepdims=True)
    n = x * lax.rsqrt(ms + EPS)
    return ((n * gain) * (1.0 + scale) + shift).astype(BF16)


def _mod_kernel(c_ref, w_ref, b_ref, o_ref):
    a = c_ref[...]
    a = a * jax.nn.sigmoid(a)
    a_hi, a_lo = _split_bf16(a)
    w_hi, w_lo = _split_bf16(w_ref[...])
    o_ref[...] = _dot(a_hi, w_hi) + _dot(a_hi, w_lo) + _dot(a_lo, w_hi) + b_ref[...]


def _modulation(c_all, w_mod, b_mod):
    depth, d, n = w_mod.shape
    rows = c_all.shape[0]
    tn = 768
    return pl.pallas_call(
        _mod_kernel,
        grid=(depth, n // tn),
        in_specs=[
            pl.BlockSpec((rows, d), lambda l, j: (0, 0)),
            pl.BlockSpec((None, d, tn), lambda l, j: (l, 0, j)),
            pl.BlockSpec((None, 1, tn), lambda l, j: (l, 0, j)),
        ],
        out_specs=pl.BlockSpec((None, rows, tn), lambda l, j: (l, 0, j)),
        out_shape=jax.ShapeDtypeStruct((depth, rows, n), F32),
        compiler_params=_params(2, 32 * 1024 * 1024),
        name="modulation",
    )(c_all, w_mod, b_mod.reshape(depth, 1, n))


def _half_block_rms(t, first_half, gain):
    sq = t * t
    s_first = jnp.sum(jnp.where(first_half, sq, 0.0), axis=-1, keepdims=True)
    s_second = jnp.sum(jnp.where(first_half, 0.0, sq), axis=-1, keepdims=True)
    ms = jnp.where(first_half, s_first, s_second) * (1.0 / A_DH)
    return t * lax.rsqrt(ms + EPS) * gain


def _rope_block(t, cos, sin_up, sin_dn):
    return (t * cos + pltpu.roll(t, V7X_LANES - 16, 1) * sin_up + pltpu.roll(t, 16, 1) * sin_dn)


def _in_proj_kernel(*refs, rope):
    x_ref, sh_ref, sc_ref, ng_ref, w_ref, qkg_ref, vng_ref = refs[:7]
    pos = 7
    if rope:
        cos_ref, sup_ref, sdn_ref = refs[7:10]
        pos = 10
    outs = refs[pos:pos + N_SEG]
    hb = _norm_modulate(x_ref[...], ng_ref[...], sh_ref[...], sc_ref[...])
    qk_row = {SEG_AQ: 0, SEG_AK: 1, SEG_CQ: 2, SEG_CK: 3}
    first_half = lax.broadcasted_iota(jnp.int32, (1, V7X_LANES), 1) < A_DH
    for seg in range(N_SEG):
        r = _dot(hb, w_ref[:, seg * SEG_W:(seg + 1) * SEG_W])
        if seg in qk_row:
            row = qk_row[seg]
            blocks = []
            for j in range(SEG_W // V7X_LANES):
                lanes = slice(j * V7X_LANES, (j + 1) * V7X_LANES)
                t = _half_block_rms(r[:, lanes], first_half, qkg_ref[row:row + 1, lanes])
                if rope and seg in (SEG_AQ, SEG_AK):
                    t = _rope_block(t, cos_ref[...], sup_ref[...], sdn_ref[...])
                blocks.append(t)
            r = jnp.concatenate(blocks, axis=1)
        elif seg == SEG_DV:
            ms = jnp.mean(r * r, axis=-1, keepdims=True)
            r = r * lax.rsqrt(ms + EPS) * vng_ref[...]
        outs[seg][...] = r.astype(outs[seg].dtype)


def _in_proj(x, shift, scale, norm_g, w_mix, qk_gains, vn_g, rope_tabs):
    b, l, d = x.shape
    tm = min(ROW_TILE, l)
    batched = shift.shape[0] != 1
    mod_map = (lambda bi, i: (bi, 0, 0)) if batched else (lambda bi, i: (0, 0, 0))
    rope = rope_tabs is not None
    in_specs = [
        pl.BlockSpec((None, tm, d), lambda bi, i: (bi, i, 0)),
        pl.BlockSpec((None, 1, d), mod_map),
        pl.BlockSpec((None, 1, d), mod_map),
        _resident((1, d)),
        _resident((d, MIX_COLS)),
        _resident((4, SEG_W)),
        _resident((1, SEG_W)),
    ]
    args = [x, shift, scale, norm_g, w_mix, qk_gains, vn_g]
    if rope:
        in_specs += [pl.BlockSpec((tm, V7X_LANES), lambda bi, i: (i, 0))] * 3
        args += list(rope_tabs)
    out_bytes = sum(tm * SEG_W * jnp.dtype(t).itemsize for t in SEG_DTYPES)
    vmem = d * MIX_COLS * 2 + 2 * (tm * d * 4 + out_bytes) + 8 * tm * SEG_W * 4 + (8 << 20)
    return pl.pallas_call(
        functools.partial(_in_proj_kernel, rope=rope),
        grid=(b, l // tm),
        in_specs=in_specs,
        out_specs=[pl.BlockSpec((None, tm, SEG_W), lambda bi, i: (bi, i, 0))] * N_SEG,
        out_shape=[jax.ShapeDtypeStruct((b, l, SEG_W), t) for t in SEG_DTYPES],
        compiler_params=_params(2, vmem),
        name="in_proj_rope" if rope else "in_proj",
    )(*args)


def _diff_attn_kernel(*refs, n_parts, lq, tq, tiles_per_step, lam_init):
    al_ref, sg_ref, q_ref = refs[:3]
    kv = refs[3:3 + 2 * n_parts]
    o_ref, kcat_ref, vt_ref, s_ref, e_ref = refs[3 + 2 * n_parts:]
    al = al_ref[...]
    lam = (jnp.exp(jnp.sum(al[0:1] * al[1:2], axis=-1, keepdims=True))
           - jnp.exp(jnp.sum(al[2:3] * al[3:4], axis=-1, keepdims=True)) + lam_init)
    lane = lax.broadcasted_iota(jnp.int32, (1, V7X_LANES), 1)
    scale = A_DH ** -0.5
    masks = (jnp.where(lane < A_DH, scale, 0.0).astype(BF16),
             jnp.where(lane >= A_DH, scale, 0.0).astype(BF16))
    sub_gain = sg_ref[...] * (1.0 - lam_init)
    hw = 2 * A_DH
    n_keys = kcat_ref.shape[0]
    chunks = range(0, n_keys, ATTN_KEY_CHUNK)

    off = 0
    for p in range(n_parts):
        n = kv[2 * p].shape[0]
        kcat_ref[off:off + n, :] = kv[2 * p][...]
        vt_ref[0:hw, off:off + n] = kv[2 * p + 1][...].astype(F32).T.astype(BF16)
        off += n
    vt_ref[hw:, :] = jnp.ones((vt_ref.shape[0] - hw, n_keys), BF16)

    def score_stage(slot, r0):
        qc = q_ref[pl.ds(r0, tq), :]
        maxima = []
        for c, mask in enumerate(masks):
            qm = qc * mask
            m8 = None
            for k0 in chunks:
                s = _dot_nt(kcat_ref[k0:k0 + ATTN_KEY_CHUNK, :], qm)
                s_ref[slot, c, k0:k0 + ATTN_KEY_CHUNK, :] = s
                cm = jnp.max(s.reshape(ATTN_KEY_CHUNK // 8, 8, tq), axis=0)
                m8 = cm if m8 is None else jnp.maximum(m8, cm)
            maxima.append(jnp.max(m8, axis=0, keepdims=True))
        return maxima

    def value_stage(slot, r0, maxima):
        acc = []
        for c in range(2):
            for k0 in chunks:
                rows = slice(k0, k0 + ATTN_KEY_CHUNK)
                e_ref[slot, c, rows, :] = jnp.exp(s_ref[slot, c, rows, :] - maxima[c]).astype(BF16)
            acc.append(_dot(vt_ref[...], e_ref[slot, c]))
        o = (acc[0][:hw] * (1.0 / acc[0][hw:hw + 1]) - acc[1][:hw] * (lam / acc[1][hw:hw + 1])).T
        ms = jnp.mean(o * o, axis=-1, keepdims=True)
        o_ref[pl.ds(r0, tq), :] = (o * lax.rsqrt(ms + EPS) * sub_gain).astype(o_ref.dtype)

    def body(it, carry):
        starts = [pl.multiple_of((it * tiles_per_step + j) * tq, tq) for j in range(tiles_per_step)]
        m_next = score_stage(0, starts[0])
        for j in range(tiles_per_step):
            m_cur = m_next
            if j + 1 < tiles_per_step:
                m_next = score_stage((j + 1) % 2, starts[j + 1])
            value_stage(j % 2, starts[j], m_cur)
        return carry

    lax.fori_loop(0, lq // (tq * tiles_per_step), body, 0)


def _diff_attn(q, kv_parts, a_lambda, subln_g, lam_init):
    b, lq, _ = q.shape
    tq = min(ATTN_Q_TILE, lq)
    hw = 2 * A_DH
    head_spec = lambda n: pl.BlockSpec((None, n, hw), lambda bi, h: (bi, 0, h))
    in_specs = [_resident(a_lambda.shape), _resident((1, hw)), head_spec(lq)]
    args = [a_lambda, subln_g.reshape(1, hw), q]
    lk = 0
    for k, v in kv_parts:
        in_specs += [head_spec(k.shape[1]), head_spec(v.shape[1])]
        args += [k, v]
        lk += k.shape[1]
    ones_rows = 16
    scratch = [pltpu.VMEM((lk, hw), BF16), pltpu.VMEM((hw + ones_rows, lk), BF16),
               pltpu.VMEM((2, 2, lk, tq), F32), pltpu.VMEM((2, 2, lk, tq), BF16)]
    vmem = (2 * (2 * lq + 4 * lk) * hw * 2 + 2 * lk * (hw + ones_rows) * 2 + 4 * lk * tq * 6
            + 6 * ATTN_KEY_CHUNK * tq * 4 + (8 << 20))
    return pl.pallas_call(
        functools.partial(_diff_attn_kernel, n_parts=len(kv_parts), lq=lq, tq=tq,
                          tiles_per_step=min(ATTN_TILES_PER_STEP, lq // tq), lam_init=lam_init),
        grid=(b, A_HEADS),
        in_specs=in_specs,
        out_specs=head_spec(lq),
        out_shape=jax.ShapeDtypeStruct((b, lq, A_HEADS * hw), BF16),
        scratch_shapes=scratch,
        compiler_params=_params(2, vmem),
        name="diff_attn",
    )(*args)


def _tree_sum(terms):
    while len(terms) > 1:
        terms = [terms[i] + terms[i + 1] for i in range(0, len(terms), 2)]
    return terms[0]


def _pool_kernel(p_ref, w_ref, s_ref, o_ref, pad_ref, *, l, tp):
    wdt = p_ref.shape[-1]
    pad_ref[0:POOL_HALO, :] = jnp.zeros((POOL_HALO, wdt), F32)
    pad_ref[POOL_HALO + l:, :] = jnp.zeros((POOL_HALO, wdt), F32)
    pad_ref[POOL_HALO:POOL_HALO + l, :] = p_ref[...]
    n_tiles = l // tp
    for ti in range(n_tiles):
        t0 = ti * tp
        edge = ti == 0 or ti == n_tiles - 1
        if edge:
            t_row = t0 + lax.broadcasted_iota(jnp.int32, (tp, V7X_LANES), 0)
        for g, w in enumerate(POOL_WINDOWS):
            lanes = slice(g * V7X_LANES, (g + 1) * V7X_LANES)
            shifted = [pad_ref[POOL_HALO + t0 + j:POOL_HALO + t0 + j + tp, lanes]
                       for j in range(-(w // 2), w // 2)]
            centre = shifted[w // 2]
            total = _tree_sum(shifted)
            if edge:
                cnt = (jnp.minimum(t_row + w // 2, l) - jnp.maximum(t_row - w // 2, 0)).astype(F32)
                mean = total / cnt
            else:
                mean = total * (1.0 / w)
            y = _dot((mean - centre).astype(BF16), w_ref[g]) * s_ref[:, lanes]
            o_ref[t0:t0 + tp, lanes] = y.astype(o_ref.dtype)


def _pool(p, w_pool, s_pool):
    b, l, wdt = p.shape
    tp = min(POOL_TILE, l)
    seq = pl.BlockSpec((None, l, wdt), lambda bi: (bi, 0, 0))
    return pl.pallas_call(
        functools.partial(_pool_kernel, l=l, tp=tp),
        grid=(b,),
        in_specs=[seq, _resident(w_pool.shape), _resident((1, wdt))],
        out_specs=seq,
        out_shape=jax.ShapeDtypeStruct((b, l, wdt), BF16),
        scratch_shapes=[pltpu.VMEM((l + 2 * POOL_HALO, wdt), F32)],
        compiler_params=_params(1, 5 * l * wdt * 4 + (16 << 20)),
        name="pool",
    )(p, w_pool, s_pool.reshape(1, wdt))


def _nbr_attn_kernel(*refs, n_rows, g_rows, local):
    if local:
        q_ref, k_ref, v_ref, kc_ref, vc_ref, bias_ref, o_ref = refs
    else:
        q_ref, kc_ref, vc_ref, o_ref = refs
    lane = lax.broadcasted_iota(jnp.int32, (1, V7X_LANES), 1)
    scale = C_DH ** -0.5
    m_first = jnp.where(lane < C_DH, scale, 0.0).astype(BF16)
    m_second = jnp.where(lane >= C_DH, scale, 0.0).astype(BF16)
    lane_out = lax.broadcasted_iota(jnp.int32, (GRID_W, V7X_LANES), 1)
    n_loc = NA_ROWS * GRID_W
    hq = 2 * GRID_W

    def body(it, carry):
        r_base = it * g_rows
        q0 = pl.multiple_of(r_base * GRID_W, g_rows * GRID_W)
        qg = q_ref[pl.ds(q0, g_rows * GRID_W), :]
        blocks = []
        for gi in range(g_rows):
            qr = qg[gi * GRID_W:(gi + 1) * GRID_W]
            blocks += [qr * m_first, qr * m_second]
        qbd = jnp.concatenate(blocks, axis=0)
        s_ctx = _dot_nt(qbd, kc_ref[...])
        m = jnp.max(s_ctx, axis=-1, keepdims=True)
        if local:
            k0s, parts = [], []
            for gi in range(g_rows):
                r = r_base + gi
                rs = jnp.clip(r - NA_ROWS // 2, 0, n_rows - NA_ROWS)
                k0s.append(pl.multiple_of(rs * GRID_W, GRID_W))
                parts.append(_dot_nt(qbd[gi * hq:(gi + 1) * hq], k_ref[pl.ds(k0s[gi], n_loc), :])
                             + bias_ref[rs - r + NA_ROWS - 1])
            s_loc = jnp.concatenate(parts, axis=0)
            m = jnp.maximum(m, jnp.max(s_loc, axis=-1, keepdims=True))
            e_loc = jnp.exp(s_loc - m)
        e_ctx = jnp.exp(s_ctx - m)
        with_ones = lambda v: jnp.concatenate([v, jnp.ones_like(v)], axis=1)
        full = _dot(e_ctx.astype(BF16), with_ones(vc_ref[...]))
        if local:
            e_loc = e_loc.astype(BF16)
            full = full + jnp.concatenate(
                [_dot(e_loc[gi * hq:(gi + 1) * hq], with_ones(v_ref[pl.ds(k0s[gi], n_loc), :]))
                 for gi in range(g_rows)], axis=0)
        full = full[:, :V7X_LANES] * (1.0 / full[:, V7X_LANES:V7X_LANES + 1])
        out = jnp.concatenate(
            [jnp.where(lane_out < C_DH, full[gi * hq:gi * hq + GRID_W], full[gi * hq + GRID_W:(gi + 1) * hq])
             for gi in range(g_rows)], axis=0)
        o_ref[pl.ds(q0, g_rows * GRID_W), :] = out.astype(o_ref.dtype)
        return carry

    lax.fori_loop(0, n_rows // g_rows, body, 0)


def _nbr_attn(q, k, v, kc, vc, bias_t):
    b, lq, wdt = q.shape
    lc = kc.shape[1]
    n_rows = lq // GRID_W
    local = k is not None
    pair = lambda n: pl.BlockSpec((None, n, V7X_LANES), lambda h, bi: (bi, 0, h))
    if local:
        in_specs = [pair(lq), pair(lq), pair(lq), pair(lc), pair(lc),
                    pl.BlockSpec((None,) + bias_t.shape[1:], lambda h, bi: (h, 0, 0, 0))]
        args = [q, k, v, kc, vc, bias_t]
    else:
        in_specs = [pair(lq), pair(lc), pair(lc)]
        args = [q, kc, vc]
    return pl.pallas_call(
        functools.partial(_nbr_attn_kernel, n_rows=n_rows, g_rows=min(NBR_ROWS_PER_STEP, n_rows), local=local),
        grid=(wdt // V7X_LANES, b),
        in_specs=in_specs,
        out_specs=pair(lq),
        out_shape=jax.ShapeDtypeStruct((b, lq, wdt), BF16),
        compiler_params=_params(2, 32 * 1024 * 1024),
        name="nbr_attn" if local else "ctx_attn",
    )(*args)


def _nbr_bias_table(rpb):
    qc = np.arange(GRID_W)[:, None]
    kc = np.arange(GRID_W)[None, :]
    win0 = np.clip(qc - NA_COLS // 2, 0, GRID_W - NA_COLS)
    valid = (kc >= win0) & (kc < win0 + NA_COLS)
    dc = np.clip(kc - qc + NA_COLS - 1, 0, 2 * NA_COLS - 2)
    t = jnp.where(valid[None, None], rpb[:, :, dc], -jnp.inf)
    h = t.shape[0]
    dr = np.arange(NA_ROWS)[:, None] + np.arange(NA_ROWS)[None, :]
    t = t[:, dr]
    t = t.reshape(h // 2, 2, NA_ROWS, NA_ROWS, GRID_W, GRID_W).transpose(0, 2, 1, 4, 3, 5)
    return t.reshape(h // 2, NA_ROWS, 2 * GRID_W, NA_ROWS * GRID_W)


def _gating_kernel(u_ref, v_ref, w_ref, b_ref, o_ref, *, tg):
    n_groups = w_ref.shape[0]
    for n in range(tg // CHUNK):
        rows = slice(n * CHUNK, (n + 1) * CHUNK)
        for g in range(n_groups):
            lanes = slice(g * V7X_LANES, (g + 1) * V7X_LANES)
            sv = _dot(w_ref[g], v_ref[rows, lanes]) + b_ref[:, lanes]
            o_ref[rows, lanes] = (u_ref[rows, lanes] * sv).astype(o_ref.dtype)


def _gating(u, v, w_s, b_full):
    b, l, wdt = u.shape
    tg = min(ROW_TILE, l)
    tile = pl.BlockSpec((None, tg, wdt), lambda bi, i: (bi, i, 0))
    return pl.pallas_call(
        functools.partial(_gating_kernel, tg=tg),
        grid=(b, l // tg),
        in_specs=[tile, tile, _resident(w_s.shape), _resident(b_full.shape)],
        out_specs=tile,
        out_shape=jax.ShapeDtypeStruct((b, l, wdt), BF16),
        compiler_params=_params(2, 32 * 1024 * 1024),
        name="gating",
    )(u, v, w_s, b_full)


def _merge_kernel(x_ref, sh_ref, sc_ref, g1_ref, ng_ref, ya_ref, yb_ref, yc_ref, yd_ref,
                  wg_ref, wb_ref, wo_ref, o_ref):
    x = x_ref[...]
    d = x.shape[-1]
    hb = _norm_modulate(x, ng_ref[...], sh_ref[...], sc_ref[...])
    acc = None
    for i, y_ref in enumerate((ya_ref, yb_ref, yc_ref, yd_ref)):
        gate = _dot(hb, wg_ref[:, i * d:(i + 1) * d])
        term = jax.nn.sigmoid(gate) * _dot(y_ref[...], wb_ref[i * SEG_W:(i + 1) * SEG_W, :])
        acc = term if acc is None else acc + term
    o_ref[...] = x + g1_ref[...] * _dot(acc.astype(BF16), wo_ref[...])


def _merge(x, shift, scale, g1, norm_g, ys, w_gate, w_branch, w_out):
    b, l, d = x.shape
    tm = min(ROW_TILE, l)
    batched = shift.shape[0] != 1
    mod_map = (lambda bi, i: (bi, 0, 0)) if batched else (lambda bi, i: (0, 0, 0))
    x_tile = pl.BlockSpec((None, tm, d), lambda bi, i: (bi, i, 0))
    y_tile = pl.BlockSpec((None, tm, SEG_W), lambda bi, i: (bi, i, 0))
    mod = pl.BlockSpec((None, 1, d), mod_map)
    weights = (w_gate.size + w_branch.size + w_out.size) * 2
    vmem = weights + 2 * (2 * tm * d * 4 + 4 * tm * SEG_W * 2) + 8 * tm * d * 4 + (8 << 20)
    return pl.pallas_call(
        _merge_kernel,
        grid=(b, l // tm),
        in_specs=[x_tile, mod, mod, mod, _resident((1, d)), y_tile, y_tile, y_tile, y_tile,
                  _resident(w_gate.shape), _resident(w_branch.shape), _resident(w_out.shape)],
        out_specs=x_tile,
        out_shape=jax.ShapeDtypeStruct(x.shape, F32),
        compiler_params=_params(2, vmem),
        name="merge",
    )(x, shift, scale, g1, norm_g, *ys, w_gate, w_branch, w_out)


def _ffn_kernel(x_ref, sh_ref, sc_ref, g2_ref, ng_ref, wgu_ref, wd_ref, o_ref, u_ref):
    x = x_ref[...]
    hb = _norm_modulate(x, ng_ref[...], sh_ref[...], sc_ref[...])
    hidden = wd_ref.shape[0]
    for c in range(hidden // FFN_CHUNK):
        cols = slice(c * FFN_CHUNK, (c + 1) * FFN_CHUNK)
        a = _dot(hb, wgu_ref[:, cols])
        bb = _dot(hb, wgu_ref[:, hidden + c * FFN_CHUNK:hidden + (c + 1) * FFN_CHUNK])
        u_ref[:, cols] = (a * jax.nn.sigmoid(a) * bb).astype(BF16)
    o_ref[...] = x + g2_ref[...] * _dot(u_ref[...], wd_ref[...])


def _ffn(x, shift, scale, g2, norm_g, w_gu, w_down):
    b, l, d = x.shape
    tm = min(ROW_TILE, l)
    hidden = w_down.shape[0]
    batched = shift.shape[0] != 1
    mod_map = (lambda bi, i: (bi, 0, 0)) if batched else (lambda bi, i: (0, 0, 0))
    x_tile = pl.BlockSpec((None, tm, d), lambda bi, i: (bi, i, 0))
    mod = pl.BlockSpec((None, 1, d), mod_map)
    vmem = (w_gu.size + w_down.size) * 2 + 4 * tm * d * 4 + tm * hidden * 2 + 8 * tm * d * 4 + (8 << 20)
    return pl.pallas_call(
        _ffn_kernel,
        grid=(b, l // tm),
        in_specs=[x_tile, mod, mod, mod, _resident((1, d)),
                  _resident(w_gu.shape), _resident(w_down.shape)],
        out_specs=x_tile,
        out_shape=jax.ShapeDtypeStruct(x.shape, F32),
        scratch_shapes=[pltpu.VMEM((tm, hidden), BF16)],
        compiler_params=_params(2, vmem),
        name="ffn",
    )(x, shift, scale, g2, norm_g, w_gu, w_down)


def _rope_tables(n_tok):
    nf = A_DH // 4
    t = jnp.arange(n_tok)
    row = (t // GRID_W).astype(F32)
    col = (t % GRID_W).astype(F32)
    inv = ROPE_THETA ** (-jnp.arange(nf, dtype=F32) / nf)
    ar = row[:, None] * inv
    ac = col[:, None] * inv
    ang = jnp.concatenate([ar, ar, ac, ac], axis=-1)
    cos, sin = jnp.cos(ang), jnp.sin(ang)
    quarter = (np.arange(A_DH) // nf) % 2
    sin_up = jnp.where(quarter == 0, -sin, 0.0)
    sin_dn = jnp.where(quarter == 1, sin, 0.0)
    rep = V7X_LANES // A_DH
    return tuple(jnp.tile(a, (1, rep)) for a in (cos, sin_up, sin_dn))


def kernel(x, c, ctx, c_ctx, w_mod, b_mod, norm1_g, w_in, a_qk_g, a_lambda, a_subln_g, b_pool_w,
           b_pool_s, c_qk_g, c_rpb, d_vn_g, d_ws, d_bs, w_branch, w_out, norm2_g, w_gu, w_down):
    b, s, d = x.shape
    depth = w_mod.shape[0]
    rope_tabs = _rope_tables(s)

    rows = -(-(b + 1) // 8) * 8
    c_all = jnp.zeros((rows, d), F32).at[:b].set(c).at[b].set(c_ctx)
    mod = _modulation(c_all, w_mod, b_mod)

    for l in range(depth):
        last = l == depth - 1
        lam_init = 0.8 - 0.6 * math.exp(-0.3 * l)
        mx = [mod[l, :b, k * d:(k + 1) * d].reshape(b, 1, d) for k in range(6)]
        mc = [mod[l, b:b + 1, k * d:(k + 1) * d].reshape(1, 1, d) for k in range(6)]
        w_mix = w_in[l][:, :MIX_COLS].astype(BF16)
        w_gate = w_in[l][:, MIX_COLS:].astype(BF16)
        w_br = w_branch[l].astype(BF16)
        w_o = w_out[l].astype(BF16)
        w_gu_l = w_gu[l].astype(BF16)
        w_dn = w_down[l].astype(BF16)
        w_pool = b_pool_w[l].astype(BF16)
        w_s = d_ws[l].astype(BF16)
        n1 = norm1_g[l].reshape(1, d)
        n2 = norm2_g[l].reshape(1, d)
        rep = SEG_W // A_DH
        qk_gains = jnp.stack([jnp.tile(a_qk_g[l, 0], rep), jnp.tile(a_qk_g[l, 1], rep),
                              jnp.tile(c_qk_g[l, 0], rep), jnp.tile(c_qk_g[l, 1], rep)])
        vn_g = d_vn_g[l].reshape(1, SEG_W)
        bs_full = jnp.repeat(d_bs[l].T, V7X_LANES, axis=1)
        bias_t = _nbr_bias_table(c_rpb[l])

        px = _in_proj(x, mx[0], mx[1], n1, w_mix, qk_gains, vn_g, rope_tabs)
        pc = _in_proj(ctx, mc[0], mc[1], n1, w_mix, qk_gains, vn_g, None)

        ya = _diff_attn(px[SEG_AQ], [(px[SEG_AK], px[SEG_AV]), (pc[SEG_AK], pc[SEG_AV])],
                        a_lambda[l], a_subln_g[l], lam_init)
        yb = _pool(px[SEG_B], w_pool, b_pool_s[l])
        yc = _nbr_attn(px[SEG_CQ], px[SEG_CK], px[SEG_CV], pc[SEG_CK], pc[SEG_CV], bias_t)
        yd = _gating(px[SEG_DU], px[SEG_DV], w_s, bs_full)
        x = _merge(x, mx[0], mx[1], mx[2], n1, (ya, yb, yc, yd), w_gate, w_br, w_o)
        x = _ffn(x, mx[3], mx[4], mx[5], n2, w_gu_l, w_dn)

        if not last:
            ya_c = _diff_attn(pc[SEG_AQ], [(pc[SEG_AK], pc[SEG_AV])], a_lambda[l], a_subln_g[l], lam_init)
            yb_c = _pool(pc[SEG_B], w_pool, b_pool_s[l])
            yc_c = _nbr_attn(pc[SEG_CQ], None, None, pc[SEG_CK], pc[SEG_CV], None)
            yd_c = _gating(pc[SEG_DU], pc[SEG_DV], w_s, bs_full)
            ctx = _merge(ctx, mc[0], mc[1], mc[2], n1, (ya_c, yb_c, yc_c, yd_c), w_gate, w_br, w_o)
            ctx = _ffn(ctx, mc[3], mc[4], mc[5], n2, w_gu_l, w_dn)
    return x
```

```python
import functools
import math

import jax
import jax.numpy as jnp
import numpy as np
from jax import lax
from jax.experimental import pallas as pl
from jax.experimental.pallas import tpu as pltpu

F32 = jnp.float32
BF16 = jnp.bfloat16

D_MODEL = 1024
GRID_W = 64
EPS = 1e-6
ROPE_THETA = 10000.0
A_HEADS = 4
A_DH = 64
POOL_WINDOWS = (2, 4, 8, 16)
C_DH = 64
NA_ROWS = 8
NA_COLS = 16
CHUNK = 128
N_BRANCH = 4
SEG_W = 512
N_SEG = 9
MIX_COLS = N_SEG * SEG_W
FFN_HIDDEN = 2816
SEG_AQ, SEG_AK, SEG_AV, SEG_B, SEG_CQ, SEG_CK, SEG_CV, SEG_DU, SEG_DV = range(N_SEG)
SEG_YD = N_SEG
SEG_DTYPES = (BF16, BF16, BF16, F32, BF16, BF16, BF16, F32, BF16, BF16)
MIXER_SEGS = (SEG_AQ, SEG_AK, SEG_AV, SEG_B, SEG_CQ, SEG_CK, SEG_CV, SEG_YD)
CTX_KV_SEGS = (SEG_AK, SEG_AV, SEG_CK, SEG_CV)

V7X_LANES = 128
V7X_VMEM_BYTES = 64 * 1024 * 1024
V7X_VMEM_LIMIT = V7X_VMEM_BYTES - 8 * 1024 * 1024

ROW_TILE = 1024
ATTN_Q_TILE = 256
ATTN_TILES_PER_STEP = 8
ATTN_KEY_CHUNK = 256
POOL_TILE = 256
POOL_HALO = 8
FFN_CHUNK = 256
NBR_ROWS_PER_STEP = 32


def _params(n_axes, vmem_bytes):
    return pltpu.CompilerParams(
        dimension_semantics=("parallel",) * n_axes,
        vmem_limit_bytes=int(min(vmem_bytes, V7X_VMEM_LIMIT)))


def _resident(shape):
    zeros = (0,) * len(shape)
    return pl.BlockSpec(shape, lambda *_: zeros, pipeline_mode=pl.Buffered(1))


def _dot(a, b):
    return jnp.dot(a, b, preferred_element_type=F32)


def _dot_nt(a, b):
    return lax.dot_general(a, b, (((1,), (1,)), ((), ())), preferred_element_type=F32)


def _dot_tn(a, b):
    return lax.dot_general(a, b, (((0,), (0,)), ((), ())), preferred_element_type=F32)


def _split_bf16(t):
    hi = t.astype(BF16)
    lo = (t - hi.astype(F32)).astype(BF16)
    return hi, lo


def _norm_modulate(x, gain, shift, scale):
    ms = jnp.mean(x * x, axis=-1, keepdims=True)
    n = x * lax.rsqrt(ms + EPS)
    return ((n * gain) * (1.0 + scale) + shift).astype(BF16)


def _mod_kernel(c_ref, w_ref, b_ref, o_ref):
    a = c_ref[...]
    a = a * jax.nn.sigmoid(a)
    a_hi, a_lo = _split_bf16(a)
    w_hi, w_lo = _split_bf16(w_ref[...])
    o_ref[...] = _dot(a_hi, w_hi) + _dot(a_hi, w_lo) + _dot(a_lo, w_hi) + b_ref[...]


def _modulation(c_all, w_mod, b_mod):
    depth, d, n = w_mod.shape
    rows = c_all.shape[0]
    tn = 768
    return pl.pallas_call(
        _mod_kernel,
        grid=(depth, n // tn),
        in_specs=[
            pl.BlockSpec((rows, d), lambda l, j: (0, 0)),
            pl.BlockSpec((None, d, tn), lambda l, j: (l, 0, j)),
            pl.BlockSpec((None, 1, tn), lambda l, j: (l, 0, j)),
        ],
        out_specs=pl.BlockSpec((None, rows, tn), lambda l, j: (l, 0, j)),
        out_shape=jax.ShapeDtypeStruct((depth, rows, n), F32),
        compiler_params=_params(2, 32 * 1024 * 1024),
        name="modulation",
    )(c_all, w_mod, b_mod.reshape(depth, 1, n))


def _half_block_rms(t, first_half, gain):
    sq = t * t
    s_first = jnp.sum(jnp.where(first_half, sq, 0.0), axis=-1, keepdims=True)
    s_second = jnp.sum(jnp.where(first_half, 0.0, sq), axis=-1, keepdims=True)
    ms = jnp.where(first_half, s_first, s_second) * (1.0 / A_DH)
    return t * lax.rsqrt(ms + EPS) * gain


def _rope_block(t, cos, sin_up, sin_dn):
    return (t * cos + pltpu.roll(t, V7X_LANES - 16, 1) * sin_up + pltpu.roll(t, 16, 1) * sin_dn)


def _in_proj_kernel(*refs, rope, out_segs):
    x_ref, sh_ref, sc_ref, ng_ref, w_ref, qkg_ref, vng_ref, ws_ref, bs_ref = refs[:9]
    pos = 9
    if rope:
        cos_ref, sup_ref, sdn_ref = refs[9:12]
        pos = 12
    outs = dict(zip(out_segs, refs[pos:pos + len(out_segs)]))
    gated = SEG_YD in outs
    needed = set(out_segs) | ({SEG_DU, SEG_DV} if gated else set())
    hb = _norm_modulate(x_ref[...], ng_ref[...], sh_ref[...], sc_ref[...])
    qk_row = {SEG_AQ: 0, SEG_AK: 1, SEG_CQ: 2, SEG_CK: 3}
    first_half = lax.broadcasted_iota(jnp.int32, (1, V7X_LANES), 1) < A_DH
    for seg in range(N_SEG):
        if seg not in needed:
            continue
        r = _dot(hb, w_ref[:, seg * SEG_W:(seg + 1) * SEG_W])
        if seg in qk_row:
            row = qk_row[seg]
            blocks = []
            for j in range(SEG_W // V7X_LANES):
                lanes = slice(j * V7X_LANES, (j + 1) * V7X_LANES)
                t = _half_block_rms(r[:, lanes], first_half, qkg_ref[row:row + 1, lanes])
                if rope and seg in (SEG_AQ, SEG_AK):
                    t = _rope_block(t, cos_ref[...], sup_ref[...], sdn_ref[...])
                blocks.append(t)
            r = jnp.concatenate(blocks, axis=1)
        elif seg == SEG_DU and gated:
            gate_u = r
        elif seg == SEG_DV:
            ms = jnp.mean(r * r, axis=-1, keepdims=True)
            r = r * lax.rsqrt(ms + EPS) * vng_ref[...]
            if gated:
                vb = r.astype(BF16)
                for n in range(r.shape[0] // CHUNK):
                    rows = slice(n * CHUNK, (n + 1) * CHUNK)
                    for g in range(ws_ref.shape[0]):
                        lanes = slice(g * V7X_LANES, (g + 1) * V7X_LANES)
                        sv = _dot(ws_ref[g], vb[rows, lanes]) + bs_ref[:, lanes]
                        outs[SEG_YD][rows, lanes] = (gate_u[rows, lanes] * sv).astype(outs[SEG_YD].dtype)
        if seg in outs:
            outs[seg][...] = r.astype(outs[seg].dtype)


def _in_proj(x, shift, scale, norm_g, w_mix, qk_gains, vn_g, w_s, b_full, rope_tabs, out_segs):
    b, l, d = x.shape
    tm = min(ROW_TILE, l)
    batched = shift.shape[0] != 1
    mod_map = (lambda bi, i: (bi, 0, 0)) if batched else (lambda bi, i: (0, 0, 0))
    rope = rope_tabs is not None
    in_specs = [
        pl.BlockSpec((None, tm, d), lambda bi, i: (bi, i, 0)),
        pl.BlockSpec((None, 1, d), mod_map),
        pl.BlockSpec((None, 1, d), mod_map),
        _resident((1, d)),
        _resident((d, MIX_COLS)),
        _resident((4, SEG_W)),
        _resident((1, SEG_W)),
        _resident(w_s.shape),
        _resident(b_full.shape),
    ]
    args = [x, shift, scale, norm_g, w_mix, qk_gains, vn_g, w_s, b_full]
    if rope:
        in_specs += [pl.BlockSpec((tm, V7X_LANES), lambda bi, i: (i, 0))] * 3
        args += list(rope_tabs)
    dtypes = [SEG_DTYPES[s] for s in out_segs]
    out_bytes = sum(tm * SEG_W * jnp.dtype(t).itemsize for t in dtypes)
    vmem = d * MIX_COLS * 2 + 2 * (tm * d * 4 + out_bytes) + 8 * tm * SEG_W * 4 + (8 << 20)
    outs = pl.pallas_call(
        functools.partial(_in_proj_kernel, rope=rope, out_segs=tuple(out_segs)),
        grid=(b, l // tm),
        in_specs=in_specs,
        out_specs=[pl.BlockSpec((None, tm, SEG_W), lambda bi, i: (bi, i, 0))] * len(out_segs),
        out_shape=[jax.ShapeDtypeStruct((b, l, SEG_W), t) for t in dtypes],
        compiler_params=_params(2, vmem),
        name="in_proj_rope" if rope else "in_proj",
    )(*args)
    return dict(zip(out_segs, outs))


def _diff_attn_kernel(*refs, n_parts, lq, tq, tiles_per_step, lam_init):
    al_ref, sg_ref, q_ref = refs[:3]
    kv = refs[3:3 + 2 * n_parts]
    o_ref, kcat_ref, vt_ref, s_ref, e_ref = refs[3 + 2 * n_parts:]
    al = al_ref[...]
    lam = (jnp.exp(jnp.sum(al[0:1] * al[1:2], axis=-1, keepdims=True))
           - jnp.exp(jnp.sum(al[2:3] * al[3:4], axis=-1, keepdims=True)) + lam_init)
    lane = lax.broadcasted_iota(jnp.int32, (1, V7X_LANES), 1)
    scale = A_DH ** -0.5
    masks = (jnp.where(lane < A_DH, scale, 0.0).astype(BF16),
             jnp.where(lane >= A_DH, scale, 0.0).astype(BF16))
    sub_gain = sg_ref[...] * (1.0 - lam_init)
    hw = 2 * A_DH
    n_keys = kcat_ref.shape[0]
    chunks = range(0, n_keys, ATTN_KEY_CHUNK)

    off = 0
    for p in range(n_parts):
        n = kv[2 * p].shape[0]
        kcat_ref[off:off + n, :] = kv[2 * p][...]
        vt_ref[0:hw, off:off + n] = kv[2 * p + 1][...].astype(F32).T.astype(BF16)
        off += n
    vt_ref[hw:, :] = jnp.ones((vt_ref.shape[0] - hw, n_keys), BF16)

    def score_stage(slot, r0):
        qc = q_ref[pl.ds(r0, tq), :]
        maxima = []
        for c, mask in enumerate(masks):
            qm = qc * mask
            m8 = None
            for k0 in chunks:
                s = _dot_nt(kcat_ref[k0:k0 + ATTN_KEY_CHUNK, :], qm)
                s_ref[slot, c, k0:k0 + ATTN_KEY_CHUNK, :] = s
                cm = jnp.max(s.reshape(ATTN_KEY_CHUNK // 8, 8, tq), axis=0)
                m8 = cm if m8 is None else jnp.maximum(m8, cm)
            maxima.append(jnp.max(m8, axis=0, keepdims=True))
        return maxima

    def value_stage(slot, r0, maxima):
        acc = []
        for c in range(2):
            for k0 in chunks:
                rows = slice(k0, k0 + ATTN_KEY_CHUNK)
                e_ref[slot, c, rows, :] = jnp.exp(s_ref[slot, c, rows, :] - maxima[c]).astype(BF16)
            acc.append(_dot(vt_ref[...], e_ref[slot, c]))
        o = (acc[0][:hw] * (1.0 / acc[0][hw:hw + 1]) - acc[1][:hw] * (lam / acc[1][hw:hw + 1])).T
        ms = jnp.mean(o * o, axis=-1, keepdims=True)
        o_ref[pl.ds(r0, tq), :] = (o * lax.rsqrt(ms + EPS) * sub_gain).astype(o_ref.dtype)

    def body(it, carry):
        starts = [pl.multiple_of((it * tiles_per_step + j) * tq, tq) for j in range(tiles_per_step)]
        m_next = score_stage(0, starts[0])
        for j in range(tiles_per_step):
            m_cur = m_next
            if j + 1 < tiles_per_step:
                m_next = score_stage((j + 1) % 2, starts[j + 1])
            value_stage(j % 2, starts[j], m_cur)
        return carry

    lax.fori_loop(0, lq // (tq * tiles_per_step), body, 0)


def _diff_attn(q, kv_parts, a_lambda, subln_g, lam_init):
    b, lq, _ = q.shape
    tq = min(ATTN_Q_TILE, lq)
    hw = 2 * A_DH
    head_spec = lambda n: pl.BlockSpec((None, n, hw), lambda bi, h: (bi, 0, h))
    in_specs = [_resident(a_lambda.shape), _resident((1, hw)), head_spec(lq)]
    args = [a_lambda, subln_g.reshape(1, hw), q]
    lk = 0
    for k, v in kv_parts:
        in_specs += [head_spec(k.shape[1]), head_spec(v.shape[1])]
        args += [k, v]
        lk += k.shape[1]
    ones_rows = 16
    scratch = [pltpu.VMEM((lk, hw), BF16), pltpu.VMEM((hw + ones_rows, lk), BF16),
               pltpu.VMEM((2, 2, lk, tq), F32), pltpu.VMEM((2, 2, lk, tq), BF16)]
    vmem = (2 * (2 * lq + 4 * lk) * hw * 2 + 2 * lk * (hw + ones_rows) * 2 + 4 * lk * tq * 6
            + 6 * ATTN_KEY_CHUNK * tq * 4 + (8 << 20))
    return pl.pallas_call(
        functools.partial(_diff_attn_kernel, n_parts=len(kv_parts), lq=lq, tq=tq,
                          tiles_per_step=min(ATTN_TILES_PER_STEP, lq // tq), lam_init=lam_init),
        grid=(b, A_HEADS),
        in_specs=in_specs,
        out_specs=head_spec(lq),
        out_shape=jax.ShapeDtypeStruct((b, lq, A_HEADS * hw), BF16),
        scratch_shapes=scratch,
        compiler_params=_params(2, vmem),
        name="diff_attn",
    )(*args)


def _tree_sum(terms):
    while len(terms) > 1:
        terms = [terms[i] + terms[i + 1] for i in range(0, len(terms), 2)]
    return terms[0]


def _pool_kernel(p_ref, w_ref, s_ref, o_ref, pad_ref, *, l, tp):
    wdt = p_ref.shape[-1]
    pad_ref[0:POOL_HALO, :] = jnp.zeros((POOL_HALO, wdt), F32)
    pad_ref[POOL_HALO + l:, :] = jnp.zeros((POOL_HALO, wdt), F32)
    pad_ref[POOL_HALO:POOL_HALO + l, :] = p_ref[...]
    n_tiles = l // tp
    for ti in range(n_tiles):
        t0 = ti * tp
        edge = ti == 0 or ti == n_tiles - 1
        if edge:
            t_row = t0 + lax.broadcasted_iota(jnp.int32, (tp, V7X_LANES), 0)
        for g, w in enumerate(POOL_WINDOWS):
            lanes = slice(g * V7X_LANES, (g + 1) * V7X_LANES)
            shifted = [pad_ref[POOL_HALO + t0 + j:POOL_HALO + t0 + j + tp, lanes]
                       for j in range(-(w // 2), w // 2)]
            centre = shifted[w // 2]
            total = _tree_sum(shifted)
            if edge:
                cnt = (jnp.minimum(t_row + w // 2, l) - jnp.maximum(t_row - w // 2, 0)).astype(F32)
                mean = total / cnt
            else:
                mean = total * (1.0 / w)
            y = _dot((mean - centre).astype(BF16), w_ref[g]) * s_ref[:, lanes]
            o_ref[t0:t0 + tp, lanes] = y.astype(o_ref.dtype)


def _pool(p, w_pool, s_pool):
    b, l, wdt = p.shape
    tp = min(POOL_TILE, l)
    seq = pl.BlockSpec((None, l, wdt), lambda bi: (bi, 0, 0))
    return pl.pallas_call(
        functools.partial(_pool_kernel, l=l, tp=tp),
        grid=(b,),
        in_specs=[seq, _resident(w_pool.shape), _resident((1, wdt))],
        out_specs=seq,
        out_shape=jax.ShapeDtypeStruct((b, l, wdt), BF16),
        scratch_shapes=[pltpu.VMEM((l + 2 * POOL_HALO, wdt), F32)],
        compiler_params=_params(1, 5 * l * wdt * 4 + (16 << 20)),
        name="pool",
    )(p, w_pool, s_pool.reshape(1, wdt))


def _nbr_attn_kernel(*refs, n_rows, g_rows, local):
    if local:
        q_ref, k_ref, v_ref, kc_ref, vc_ref, bias_ref, o_ref = refs
    else:
        q_ref, kc_ref, vc_ref, o_ref = refs
    lane = lax.broadcasted_iota(jnp.int32, (1, V7X_LANES), 1)
    scale = C_DH ** -0.5
    m_first = jnp.where(lane < C_DH, scale, 0.0).astype(BF16)
    m_second = jnp.where(lane >= C_DH, scale, 0.0).astype(BF16)
    lane_out = lax.broadcasted_iota(jnp.int32, (GRID_W, V7X_LANES), 1)
    n_loc = NA_ROWS * GRID_W
    hq = 2 * GRID_W

    def body(it, carry):
        r_base = it * g_rows
        q0 = pl.multiple_of(r_base * GRID_W, g_rows * GRID_W)
        qg = q_ref[pl.ds(q0, g_rows * GRID_W), :]
        blocks = []
        for gi in range(g_rows):
            qr = qg[gi * GRID_W:(gi + 1) * GRID_W]
            blocks += [qr * m_first, qr * m_second]
        qbd = jnp.concatenate(blocks, axis=0)
        s_ctx = _dot_nt(qbd, kc_ref[...])
        m = jnp.max(s_ctx, axis=-1, keepdims=True)
        if local:
            k0s, parts = [], []
            for gi in range(g_rows):
                r = r_base + gi
                rs = jnp.clip(r - NA_ROWS // 2, 0, n_rows - NA_ROWS)
                k0s.append(pl.multiple_of(rs * GRID_W, GRID_W))
                parts.append(_dot_nt(qbd[gi * hq:(gi + 1) * hq], k_ref[pl.ds(k0s[gi], n_loc), :])
                             + bias_ref[rs - r + NA_ROWS - 1])
            s_loc = jnp.concatenate(parts, axis=0)
            m = jnp.maximum(m, jnp.max(s_loc, axis=-1, keepdims=True))
            e_loc = jnp.exp(s_loc - m)
        e_ctx = jnp.exp(s_ctx - m)
        with_ones = lambda v: jnp.concatenate([v, jnp.ones_like(v)], axis=1)
        full = _dot(e_ctx.astype(BF16), with_ones(vc_ref[...]))
        if local:
            e_loc = e_loc.astype(BF16)
            full = full + jnp.concatenate(
                [_dot(e_loc[gi * hq:(gi + 1) * hq], with_ones(v_ref[pl.ds(k0s[gi], n_loc), :]))
                 for gi in range(g_rows)], axis=0)
        full = full[:, :V7X_LANES] * (1.0 / full[:, V7X_LANES:V7X_LANES + 1])
        out = jnp.concatenate(
            [jnp.where(lane_out < C_DH, full[gi * hq:gi * hq + GRID_W], full[gi * hq + GRID_W:(gi + 1) * hq])
             for gi in range(g_rows)], axis=0)
        o_ref[pl.ds(q0, g_rows * GRID_W), :] = out.astype(o_ref.dtype)
        return carry

    lax.fori_loop(0, n_rows // g_rows, body, 0)


def _nbr_attn(q, k, v, kc, vc, bias_t):
    b, lq, wdt = q.shape
    lc = kc.shape[1]
    n_rows = lq // GRID_W
    local = k is not None
    pair = lambda n: pl.BlockSpec((None, n, V7X_LANES), lambda h, bi: (bi, 0, h))
    if local:
        in_specs = [pair(lq), pair(lq), pair(lq), pair(lc), pair(lc),
                    pl.BlockSpec((None,) + bias_t.shape[1:], lambda h, bi: (h, 0, 0, 0))]
        args = [q, k, v, kc, vc, bias_t]
    else:
        in_specs = [pair(lq), pair(lc), pair(lc)]
        args = [q, kc, vc]
    return pl.pallas_call(
        functools.partial(_nbr_attn_kernel, n_rows=n_rows, g_rows=min(NBR_ROWS_PER_STEP, n_rows), local=local),
        grid=(wdt // V7X_LANES, b),
        in_specs=in_specs,
        out_specs=pair(lq),
        out_shape=jax.ShapeDtypeStruct((b, lq, wdt), BF16),
        compiler_params=_params(2, 32 * 1024 * 1024),
        name="nbr_attn" if local else "ctx_attn",
    )(*args)


def _nbr_bias_table(rpb):
    qc = np.arange(GRID_W)[:, None]
    kc = np.arange(GRID_W)[None, :]
    win0 = np.clip(qc - NA_COLS // 2, 0, GRID_W - NA_COLS)
    valid = (kc >= win0) & (kc < win0 + NA_COLS)
    dc = np.clip(kc - qc + NA_COLS - 1, 0, 2 * NA_COLS - 2)
    t = jnp.where(valid[None, None], rpb[:, :, dc], -jnp.inf)
    h = t.shape[0]
    dr = np.arange(NA_ROWS)[:, None] + np.arange(NA_ROWS)[None, :]
    t = t[:, dr]
    t = t.reshape(h // 2, 2, NA_ROWS, NA_ROWS, GRID_W, GRID_W).transpose(0, 2, 1, 4, 3, 5)
    return t.reshape(h // 2, NA_ROWS, 2 * GRID_W, NA_ROWS * GRID_W)


def _merge_kernel(x_ref, sh_ref, sc_ref, g1_ref, ng_ref, ya_ref, yb_ref, yc_ref, yd_ref,
                  wg_ref, wb_ref, wo_ref, o_ref):
    x = x_ref[...]
    d = x.shape[-1]
    hb = _norm_modulate(x, ng_ref[...], sh_ref[...], sc_ref[...])
    acc = None
    for i, y_ref in enumerate((ya_ref, yb_ref, yc_ref, yd_ref)):
        gate = _dot(hb, wg_ref[:, i * d:(i + 1) * d])
        term = jax.nn.sigmoid(gate) * _dot(y_ref[...], wb_ref[i * SEG_W:(i + 1) * SEG_W, :])
        acc = term if acc is None else acc + term
    o_ref[...] = x + g1_ref[...] * _dot(acc.astype(BF16), wo_ref[...])


def _merge(x, shift, scale, g1, norm_g, ys, w_gate, w_branch, w_out):
    b, l, d = x.shape
    tm = min(ROW_TILE, l)
    batched = shift.shape[0] != 1
    mod_map = (lambda bi, i: (bi, 0, 0)) if batched else (lambda bi, i: (0, 0, 0))
    x_tile = pl.BlockSpec((None, tm, d), lambda bi, i: (bi, i, 0))
    y_tile = pl.BlockSpec((None, tm, SEG_W), lambda bi, i: (bi, i, 0))
    mod = pl.BlockSpec((None, 1, d), mod_map)
    weights = (w_gate.size + w_branch.size + w_out.size) * 2
    vmem = weights + 2 * (2 * tm * d * 4 + 4 * tm * SEG_W * 2) + 8 * tm * d * 4 + (8 << 20)
    return pl.pallas_call(
        _merge_kernel,
        grid=(b, l // tm),
        in_specs=[x_tile, mod, mod, mod, _resident((1, d)), y_tile, y_tile, y_tile, y_tile,
                  _resident(w_gate.shape), _resident(w_branch.shape), _resident(w_out.shape)],
        out_specs=x_tile,
        out_shape=jax.ShapeDtypeStruct(x.shape, F32),
        compiler_params=_params(2, vmem),
        name="merge",
    )(x, shift, scale, g1, norm_g, *ys, w_gate, w_branch, w_out)


def _ffn_kernel(x_ref, sh_ref, sc_ref, g2_ref, ng_ref, wgu_ref, wd_ref, o_ref, u_ref):
    x = x_ref[...]
    hb = _norm_modulate(x, ng_ref[...], sh_ref[...], sc_ref[...])
    hidden = wd_ref.shape[0]
    for c in range(hidden // FFN_CHUNK):
        cols = slice(c * FFN_CHUNK, (c + 1) * FFN_CHUNK)
        a = _dot(hb, wgu_ref[:, cols])
        bb = _dot(hb, wgu_ref[:, hidden + c * FFN_CHUNK:hidden + (c + 1) * FFN_CHUNK])
        u_ref[:, cols] = (a * jax.nn.sigmoid(a) * bb).astype(BF16)
    o_ref[...] = x + g2_ref[...] * _dot(u_ref[...], wd_ref[...])


def _ffn(x, shift, scale, g2, norm_g, w_gu, w_down):
    b, l, d = x.shape
    tm = min(ROW_TILE, l)
    hidden = w_down.shape[0]
    batched = shift.shape[0] != 1
    mod_map = (lambda bi, i: (bi, 0, 0)) if batched else (lambda bi, i: (0, 0, 0))
    x_tile = pl.BlockSpec((None, tm, d), lambda bi, i: (bi, i, 0))
    mod = pl.BlockSpec((None, 1, d), mod_map)
    vmem = (w_gu.size + w_down.size) * 2 + 4 * tm * d * 4 + tm * hidden * 2 + 8 * tm * d * 4 + (8 << 20)
    return pl.pallas_call(
        _ffn_kernel,
        grid=(b, l // tm),
        in_specs=[x_tile, mod, mod, mod, _resident((1, d)),
                  _resident(w_gu.shape), _resident(w_down.shape)],
        out_specs=x_tile,
        out_shape=jax.ShapeDtypeStruct(x.shape, F32),
        scratch_shapes=[pltpu.VMEM((tm, hidden), BF16)],
        compiler_params=_params(2, vmem),
        name="ffn",
    )(x, shift, scale, g2, norm_g, w_gu, w_down)


def _rope_tables(n_tok):
    nf = A_DH // 4
    t = jnp.arange(n_tok)
    row = (t // GRID_W).astype(F32)
    col = (t % GRID_W).astype(F32)
    inv = ROPE_THETA ** (-jnp.arange(nf, dtype=F32) / nf)
    ar = row[:, None] * inv
    ac = col[:, None] * inv
    ang = jnp.concatenate([ar, ar, ac, ac], axis=-1)
    cos, sin = jnp.cos(ang), jnp.sin(ang)
    quarter = (np.arange(A_DH) // nf) % 2
    sin_up = jnp.where(quarter == 0, -sin, 0.0)
    sin_dn = jnp.where(quarter == 1, sin, 0.0)
    rep = V7X_LANES // A_DH
    return tuple(jnp.tile(a, (1, rep)) for a in (cos, sin_up, sin_dn))


def kernel(x, c, ctx, c_ctx, w_mod, b_mod, norm1_g, w_in, a_qk_g, a_lambda, a_subln_g, b_pool_w,
           b_pool_s, c_qk_g, c_rpb, d_vn_g, d_ws, d_bs, w_branch, w_out, norm2_g, w_gu, w_down):
    b, s, d = x.shape
    depth = w_mod.shape[0]
    rope_tabs = _rope_tables(s)

    rows = -(-(b + 1) // 8) * 8
    c_all = jnp.zeros((rows, d), F32).at[:b].set(c).at[b].set(c_ctx)
    mod = _modulation(c_all, w_mod, b_mod)

    for l in range(depth):
        last = l == depth - 1
        lam_init = 0.8 - 0.6 * math.exp(-0.3 * l)
        mx = [mod[l, :b, k * d:(k + 1) * d].reshape(b, 1, d) for k in range(6)]
        mc = [mod[l, b:b + 1, k * d:(k + 1) * d].reshape(1, 1, d) for k in range(6)]
        w_mix = w_in[l][:, :MIX_COLS].astype(BF16)
        w_gate = w_in[l][:, MIX_COLS:].astype(BF16)
        w_br = w_branch[l].astype(BF16)
        w_o = w_out[l].astype(BF16)
        w_gu_l = w_gu[l].astype(BF16)
        w_dn = w_down[l].astype(BF16)
        w_pool = b_pool_w[l].astype(BF16)
        w_s = d_ws[l].astype(BF16)
        n1 = norm1_g[l].reshape(1, d)
        n2 = norm2_g[l].reshape(1, d)
        rep = SEG_W // A_DH
        qk_gains = jnp.stack([jnp.tile(a_qk_g[l, 0], rep), jnp.tile(a_qk_g[l, 1], rep),
                              jnp.tile(c_qk_g[l, 0], rep), jnp.tile(c_qk_g[l, 1], rep)])
        vn_g = d_vn_g[l].reshape(1, SEG_W)
        bs_full = jnp.repeat(d_bs[l].T, V7X_LANES, axis=1)
        bias_t = _nbr_bias_table(c_rpb[l])

        px = _in_proj(x, mx[0], mx[1], n1, w_mix, qk_gains, vn_g, w_s, bs_full, rope_tabs, MIXER_SEGS)
        pc = _in_proj(ctx, mc[0], mc[1], n1, w_mix, qk_gains, vn_g, w_s, bs_full, None,
                      CTX_KV_SEGS if last else MIXER_SEGS)

        ya = _diff_attn(px[SEG_AQ], [(px[SEG_AK], px[SEG_AV]), (pc[SEG_AK], pc[SEG_AV])],
                        a_lambda[l], a_subln_g[l], lam_init)
        yb = _pool(px[SEG_B], w_pool, b_pool_s[l])
        yc = _nbr_attn(px[SEG_CQ], px[SEG_CK], px[SEG_CV], pc[SEG_CK], pc[SEG_CV], bias_t)
        x = _merge(x, mx[0], mx[1], mx[2], n1, (ya, yb, yc, px[SEG_YD]), w_gate, w_br, w_o)
        x = _ffn(x, mx[3], mx[4], mx[5], n2, w_gu_l, w_dn)

        if not last:
            ya_c = _diff_attn(pc[SEG_AQ], [(pc[SEG_AK], pc[SEG_AV])], a_lambda[l], a_subln_g[l], lam_init)
            yb_c = _pool(pc[SEG_B], w_pool, b_pool_s[l])
            yc_c = _nbr_attn(pc[SEG_CQ], None, None, pc[SEG_CK], pc[SEG_CV], None)
            ctx = _merge(ctx, mc[0], mc[1], mc[2], n1, (ya_c, yb_c, yc_c, pc[SEG_YD]), w_gate, w_br, w_o)
            ctx = _ffn(ctx, mc[3], mc[4], mc[5], n2, w_gu_l, w_dn)
    return x
```

```python
import functools
import math

import jax
import jax.numpy as jnp
import numpy as np
from jax import lax
from jax.experimental import pallas as pl
from jax.experimental.pallas import tpu as pltpu

F32 = jnp.float32
BF16 = jnp.bfloat16

D_MODEL = 1024
GRID_W = 64
EPS = 1e-6
ROPE_THETA = 10000.0
A_HEADS = 4
A_DH = 64
POOL_WINDOWS = (2, 4, 8, 16)
C_DH = 64
NA_ROWS = 8
NA_COLS = 16
CHUNK = 128
N_BRANCH = 4
SEG_W = 512
N_SEG = 9
MIX_COLS = N_SEG * SEG_W
FFN_HIDDEN = 2816
SEG_AQ, SEG_AK, SEG_AV, SEG_B, SEG_CQ, SEG_CK, SEG_CV, SEG_DU, SEG_DV = range(N_SEG)
SEG_YD = N_SEG
SEG_DTYPES = (BF16, BF16, BF16, F32, BF16, BF16, BF16, F32, BF16, BF16)
MIXER_SEGS = (SEG_AQ, SEG_AK, SEG_AV, SEG_B, SEG_CQ, SEG_CK, SEG_CV, SEG_YD)
CTX_KV_SEGS = (SEG_AK, SEG_AV, SEG_CK, SEG_CV)

V7X_LANES = 128
V7X_VMEM_BYTES = 64 * 1024 * 1024
V7X_VMEM_LIMIT = V7X_VMEM_BYTES - 8 * 1024 * 1024

ROW_TILE = 1024
ATTN_Q_TILE = 256
ATTN_KEY_CHUNK = 256
POOL_TILE = 256
POOL_HALO = 8
FFN_CHUNK = 256


def _params(n_axes, vmem_bytes):
    return pltpu.CompilerParams(
        dimension_semantics=("parallel",) * n_axes,
        vmem_limit_bytes=int(min(vmem_bytes, V7X_VMEM_LIMIT)))


def _resident(shape):
    zeros = (0,) * len(shape)
    return pl.BlockSpec(shape, lambda *_: zeros, pipeline_mode=pl.Buffered(1))


def _dot(a, b):
    return jnp.dot(a, b, preferred_element_type=F32)


def _dot_nt(a, b):
    return lax.dot_general(a, b, (((1,), (1,)), ((), ())), preferred_element_type=F32)


def _dot_tn(a, b):
    return lax.dot_general(a, b, (((0,), (0,)), ((), ())), preferred_element_type=F32)


def _split_bf16(t):
    hi = t.astype(BF16)
    lo = (t - hi.astype(F32)).astype(BF16)
    return hi, lo


def _norm_modulate(x, gain, shift, scale):
    ms = jnp.mean(x * x, axis=-1, keepdims=True)
    n = x * lax.rsqrt(ms + EPS)
    return ((n * gain) * (1.0 + scale) + shift).astype(BF16)


def _mod_kernel(c_ref, w_ref, b_ref, o_ref):
    a = c_ref[...]
    a = a * jax.nn.sigmoid(a)
    a_hi, a_lo = _split_bf16(a)
    w_hi, w_lo = _split_bf16(w_ref[...])
    o_ref[...] = _dot(a_hi, w_hi) + _dot(a_hi, w_lo) + _dot(a_lo, w_hi) + b_ref[...]


def _modulation(c_all, w_mod, b_mod):
    depth, d, n = w_mod.shape
    rows = c_all.shape[0]
    tn = 768
    return pl.pallas_call(
        _mod_kernel,
        grid=(depth, n // tn),
        in_specs=[
            pl.BlockSpec((rows, d), lambda l, j: (0, 0)),
            pl.BlockSpec((None, d, tn), lambda l, j: (l, 0, j)),
            pl.BlockSpec((None, 1, tn), lambda l, j: (l, 0, j)),
        ],
        out_specs=pl.BlockSpec((None, rows, tn), lambda l, j: (l, 0, j)),
        out_shape=jax.ShapeDtypeStruct((depth, rows, n), F32),
        compiler_params=_params(2, 32 * 1024 * 1024),
        name="modulation",
    )(c_all, w_mod, b_mod.reshape(depth, 1, n))


def _half_block_rms(t, first_half, gain):
    sq = t * t
    s_first = jnp.sum(jnp.where(first_half, sq, 0.0), axis=-1, keepdims=True)
    s_second = jnp.sum(jnp.where(first_half, 0.0, sq), axis=-1, keepdims=True)
    ms = jnp.where(first_half, s_first, s_second) * (1.0 / A_DH)
    return t * lax.rsqrt(ms + EPS) * gain


def _rope_block(t, cos, sin_up, sin_dn):
    return (t * cos + pltpu.roll(t, V7X_LANES - 16, 1) * sin_up + pltpu.roll(t, 16, 1) * sin_dn)


def _in_proj_kernel(*refs, rope, out_segs):
    x_ref, sh_ref, sc_ref, ng_ref, w_ref, qkg_ref, vng_ref, ws_ref, bs_ref = refs[:9]
    pos = 9
    if rope:
        cos_ref, sup_ref, sdn_ref = refs[9:12]
        pos = 12
    outs = dict(zip(out_segs, refs[pos:pos + len(out_segs)]))
    gated = SEG_YD in outs
    needed = set(out_segs) | ({SEG_DU, SEG_DV} if gated else set())
    hb = _norm_modulate(x_ref[...], ng_ref[...], sh_ref[...], sc_ref[...])
    qk_row = {SEG_AQ: 0, SEG_AK: 1, SEG_CQ: 2, SEG_CK: 3}
    first_half = lax.broadcasted_iota(jnp.int32, (1, V7X_LANES), 1) < A_DH
    for seg in range(N_SEG):
        if seg not in needed:
            continue
        r = _dot(hb, w_ref[:, seg * SEG_W:(seg + 1) * SEG_W])
        if seg in qk_row:
            row = qk_row[seg]
            blocks = []
            for j in range(SEG_W // V7X_LANES):
                lanes = slice(j * V7X_LANES, (j + 1) * V7X_LANES)
                t = _half_block_rms(r[:, lanes], first_half, qkg_ref[row:row + 1, lanes])
                if rope and seg in (SEG_AQ, SEG_AK):
                    t = _rope_block(t, cos_ref[...], sup_ref[...], sdn_ref[...])
                blocks.append(t)
            r = jnp.concatenate(blocks, axis=1)
        elif seg == SEG_DU and gated:
            gate_u = r
        elif seg == SEG_DV:
            ms = jnp.mean(r * r, axis=-1, keepdims=True)
            r = r * lax.rsqrt(ms + EPS) * vng_ref[...]
            if gated:
                vb = r.astype(BF16)
                n_chunks = r.shape[0] // CHUNK
                for g in range(ws_ref.shape[0]):
                    lanes = slice(g * V7X_LANES, (g + 1) * V7X_LANES)
                    v_wide = jnp.concatenate([vb[n * CHUNK:(n + 1) * CHUNK, lanes] for n in range(n_chunks)], axis=1)
                    sv_wide = _dot(ws_ref[g], v_wide)
                    for n in range(n_chunks):
                        rows = slice(n * CHUNK, (n + 1) * CHUNK)
                        sv = sv_wide[:, n * V7X_LANES:(n + 1) * V7X_LANES] + bs_ref[:, lanes]
                        outs[SEG_YD][rows, lanes] = (gate_u[rows, lanes] * sv).astype(outs[SEG_YD].dtype)
        if seg in outs:
            outs[seg][...] = r.astype(outs[seg].dtype)


def _in_proj(x, shift, scale, norm_g, w_mix, qk_gains, vn_g, w_s, b_full, rope_tabs, out_segs):
    b, l, d = x.shape
    tm = min(ROW_TILE, l)
    batched = shift.shape[0] != 1
    mod_map = (lambda bi, i: (bi, 0, 0)) if batched else (lambda bi, i: (0, 0, 0))
    rope = rope_tabs is not None
    in_specs = [
        pl.BlockSpec((None, tm, d), lambda bi, i: (bi, i, 0)),
        pl.BlockSpec((None, 1, d), mod_map),
        pl.BlockSpec((None, 1, d), mod_map),
        _resident((1, d)),
        _resident((d, MIX_COLS)),
        _resident((4, SEG_W)),
        _resident((1, SEG_W)),
        _resident(w_s.shape),
        _resident(b_full.shape),
    ]
    args = [x, shift, scale, norm_g, w_mix, qk_gains, vn_g, w_s, b_full]
    if rope:
        in_specs += [pl.BlockSpec((tm, V7X_LANES), lambda bi, i: (i, 0))] * 3
        args += list(rope_tabs)
    dtypes = [SEG_DTYPES[s] for s in out_segs]
    out_bytes = sum(tm * SEG_W * jnp.dtype(t).itemsize for t in dtypes)
    vmem = d * MIX_COLS * 2 + 2 * (tm * d * 4 + out_bytes) + 8 * tm * SEG_W * 4 + (8 << 20)
    outs = pl.pallas_call(
        functools.partial(_in_proj_kernel, rope=rope, out_segs=tuple(out_segs)),
        grid=(b, l // tm),
        in_specs=in_specs,
        out_specs=[pl.BlockSpec((None, tm, SEG_W), lambda bi, i: (bi, i, 0))] * len(out_segs),
        out_shape=[jax.ShapeDtypeStruct((b, l, SEG_W), t) for t in dtypes],
        compiler_params=_params(2, vmem),
        name="in_proj_rope" if rope else "in_proj",
    )(*args)
    return dict(zip(out_segs, outs))


def _diff_attn_kernel(*refs, n_parts, lq, tq, heads, lam_init):
    al_ref, sg_ref, q_ref = refs[:3]
    kv = refs[3:3 + 2 * n_parts]
    o_ref, kcat_ref, vt_ref, s_ref, e_ref = refs[3 + 2 * n_parts:]
    al = al_ref[...]
    lam = (jnp.exp(jnp.sum(al[0:1] * al[1:2], axis=-1, keepdims=True))
           - jnp.exp(jnp.sum(al[2:3] * al[3:4], axis=-1, keepdims=True)) + lam_init)
    lane = lax.broadcasted_iota(jnp.int32, (1, V7X_LANES), 1)
    scale = A_DH ** -0.5
    masks = (jnp.where(lane < A_DH, scale, 0.0).astype(BF16),
             jnp.where(lane >= A_DH, scale, 0.0).astype(BF16))
    sub_gain = sg_ref[...] * (1.0 - lam_init)
    hw = 2 * A_DH
    n_keys = kcat_ref.shape[0]
    chunks = range(0, n_keys, ATTN_KEY_CHUNK)
    vt_ref[hw:, :] = jnp.ones((vt_ref.shape[0] - hw, n_keys), BF16)

    def score_stage(cols, slot, r0):
        qc = q_ref[r0:r0 + tq, cols]
        maxima = []
        for c, mask in enumerate(masks):
            qm = qc * mask
            m8 = None
            for k0 in chunks:
                s = _dot_nt(kcat_ref[k0:k0 + ATTN_KEY_CHUNK, :], qm)
                s_ref[slot, c, k0:k0 + ATTN_KEY_CHUNK, :] = s
                cm = jnp.max(s.reshape(ATTN_KEY_CHUNK // 8, 8, tq), axis=0)
                m8 = cm if m8 is None else jnp.maximum(m8, cm)
            maxima.append(jnp.max(m8, axis=0, keepdims=True))
        return maxima

    def value_stage(cols, slot, r0, maxima):
        acc = []
        for c in range(2):
            for k0 in chunks:
                rows = slice(k0, k0 + ATTN_KEY_CHUNK)
                e_ref[slot, c, rows, :] = jnp.exp(s_ref[slot, c, rows, :] - maxima[c]).astype(BF16)
            acc.append(_dot(vt_ref[...], e_ref[slot, c]))
        o = (acc[0][:hw] * (1.0 / acc[0][hw:hw + 1]) - acc[1][:hw] * (lam / acc[1][hw:hw + 1])).T
        ms = jnp.mean(o * o, axis=-1, keepdims=True)
        o_ref[r0:r0 + tq, cols] = (o * lax.rsqrt(ms + EPS) * sub_gain).astype(o_ref.dtype)

    starts = range(0, lq, tq)
    for h in range(heads):
        cols = slice(h * hw, (h + 1) * hw)
        off = 0
        for p in range(n_parts):
            n = kv[2 * p].shape[0]
            kcat_ref[off:off + n, :] = kv[2 * p][:, cols]
            vt_ref[0:hw, off:off + n] = kv[2 * p + 1][:, cols].astype(F32).T.astype(BF16)
            off += n
        m_next = score_stage(cols, 0, starts[0])
        for j, r0 in enumerate(starts):
            m_cur = m_next
            if j + 1 < len(starts):
                m_next = score_stage(cols, (j + 1) % 2, starts[j + 1])
            value_stage(cols, j % 2, r0, m_cur)


def _diff_attn(q, kv_parts, a_lambda, subln_g, lam_init, heads_per_step):
    b, lq, _ = q.shape
    tq = min(ATTN_Q_TILE, lq)
    hw = 2 * A_DH
    wdt = heads_per_step * hw
    head_spec = lambda n: pl.BlockSpec((None, n, wdt), lambda bi, h: (bi, 0, h))
    in_specs = [_resident(a_lambda.shape), _resident((1, hw)), head_spec(lq)]
    args = [a_lambda, subln_g.reshape(1, hw), q]
    lk = 0
    for k, v in kv_parts:
        in_specs += [head_spec(k.shape[1]), head_spec(v.shape[1])]
        args += [k, v]
        lk += k.shape[1]
    ones_rows = 16
    scratch = [pltpu.VMEM((lk, hw), BF16), pltpu.VMEM((hw + ones_rows, lk), BF16),
               pltpu.VMEM((2, 2, lk, tq), F32), pltpu.VMEM((2, 2, lk, tq), BF16)]
    vmem = (2 * (2 * lq + 4 * lk) * wdt * 2 + 2 * lk * (hw + ones_rows) * 2 + 4 * lk * tq * 6
            + 6 * ATTN_KEY_CHUNK * tq * 4 + (8 << 20))
    return pl.pallas_call(
        functools.partial(_diff_attn_kernel, n_parts=len(kv_parts), lq=lq, tq=tq,
                          heads=heads_per_step, lam_init=lam_init),
        grid=(b, A_HEADS // heads_per_step),
        in_specs=in_specs,
        out_specs=head_spec(lq),
        out_shape=jax.ShapeDtypeStruct((b, lq, A_HEADS * hw), BF16),
        scratch_shapes=scratch,
        compiler_params=_params(2, vmem),
        name="diff_attn",
    )(*args)


def _tree_sum(terms):
    while len(terms) > 1:
        terms = [terms[i] + terms[i + 1] for i in range(0, len(terms), 2)]
    return terms[0]


def _pool_kernel(p_ref, w_ref, s_ref, o_ref, pad_ref, *, l, tp):
    wdt = p_ref.shape[-1]
    pad_ref[0:POOL_HALO, :] = jnp.zeros((POOL_HALO, wdt), F32)
    pad_ref[POOL_HALO + l:, :] = jnp.zeros((POOL_HALO, wdt), F32)
    pad_ref[POOL_HALO:POOL_HALO + l, :] = p_ref[...]
    n_tiles = l // tp
    for ti in range(n_tiles):
        t0 = ti * tp
        edge = ti == 0 or ti == n_tiles - 1
        if edge:
            t_row = t0 + lax.broadcasted_iota(jnp.int32, (tp, V7X_LANES), 0)
        for g, w in enumerate(POOL_WINDOWS):
            lanes = slice(g * V7X_LANES, (g + 1) * V7X_LANES)
            shifted = [pad_ref[POOL_HALO + t0 + j:POOL_HALO + t0 + j + tp, lanes]
                       for j in range(-(w // 2), w // 2)]
            centre = shifted[w // 2]
            total = _tree_sum(shifted)
            if edge:
                cnt = (jnp.minimum(t_row + w // 2, l) - jnp.maximum(t_row - w // 2, 0)).astype(F32)
                mean = total / cnt
            else:
                mean = total * (1.0 / w)
            y = _dot((mean - centre).astype(BF16), w_ref[g]) * s_ref[:, lanes]
            o_ref[t0:t0 + tp, lanes] = y.astype(o_ref.dtype)


def _pool(p, w_pool, s_pool):
    b, l, wdt = p.shape
    tp = min(POOL_TILE, l)
    seq = pl.BlockSpec((None, l, wdt), lambda bi: (bi, 0, 0))
    return pl.pallas_call(
        functools.partial(_pool_kernel, l=l, tp=tp),
        grid=(b,),
        in_specs=[seq, _resident(w_pool.shape), _resident((1, wdt))],
        out_specs=seq,
        out_shape=jax.ShapeDtypeStruct((b, l, wdt), BF16),
        scratch_shapes=[pltpu.VMEM((l + 2 * POOL_HALO, wdt), F32)],
        compiler_params=_params(1, 5 * l * wdt * 4 + (16 << 20)),
        name="pool",
    )(p, w_pool, s_pool.reshape(1, wdt))


def _nbr_attn_kernel(*refs, n_rows, pairs, local):
    if local:
        q_ref, k_ref, v_ref, kc_ref, vc_ref, bias_ref, o_ref = refs
    else:
        q_ref, kc_ref, vc_ref, o_ref = refs
    lane = lax.broadcasted_iota(jnp.int32, (1, V7X_LANES), 1)
    scale = C_DH ** -0.5
    m_first = jnp.where(lane < C_DH, scale, 0.0).astype(BF16)
    m_second = jnp.where(lane >= C_DH, scale, 0.0).astype(BF16)
    lane_out = lax.broadcasted_iota(jnp.int32, (GRID_W, V7X_LANES), 1)
    n_loc = NA_ROWS * GRID_W
    hq = 2 * GRID_W
    with_ones = lambda v: jnp.concatenate([v, jnp.ones_like(v)], axis=1)
    first_key_row = [min(max(r - NA_ROWS // 2, 0), n_rows - NA_ROWS) for r in range(n_rows)]

    for p in range(pairs):
        cols = slice(p * V7X_LANES, (p + 1) * V7X_LANES)
        blocks = []
        for r in range(n_rows):
            qr = q_ref[r * GRID_W:(r + 1) * GRID_W, cols]
            blocks += [qr * m_first, qr * m_second]
        qbd = jnp.concatenate(blocks, axis=0)
        s_ctx = _dot_nt(qbd, kc_ref[:, cols])
        m = jnp.max(s_ctx, axis=-1, keepdims=True)
        if local:
            s_loc = jnp.concatenate(
                [_dot_nt(qbd[r * hq:(r + 1) * hq], k_ref[rs * GRID_W:rs * GRID_W + n_loc, cols])
                 + bias_ref[rs - r + NA_ROWS - 1] for r, rs in enumerate(first_key_row)], axis=0)
            m = jnp.maximum(m, jnp.max(s_loc, axis=-1, keepdims=True))
            e_loc = jnp.exp(s_loc - m).astype(BF16)
        e_ctx = jnp.exp(s_ctx - m)
        full = _dot(e_ctx.astype(BF16), with_ones(vc_ref[:, cols]))
        if local:
            full = full + jnp.concatenate(
                [_dot(e_loc[r * hq:(r + 1) * hq], with_ones(v_ref[rs * GRID_W:rs * GRID_W + n_loc, cols]))
                 for r, rs in enumerate(first_key_row)], axis=0)
        full = full[:, :V7X_LANES] * (1.0 / full[:, V7X_LANES:V7X_LANES + 1])
        out = jnp.concatenate(
            [jnp.where(lane_out < C_DH, full[r * hq:r * hq + GRID_W], full[r * hq + GRID_W:(r + 1) * hq])
             for r in range(n_rows)], axis=0)
        o_ref[:, cols] = out.astype(o_ref.dtype)


def _nbr_attn(q, k, v, kc, vc, bias_t, pairs_per_step):
    b, lq, wdt = q.shape
    lc = kc.shape[1]
    n_rows = lq // GRID_W
    local = k is not None
    blk = pairs_per_step * V7X_LANES
    pair = lambda n: pl.BlockSpec((None, n, blk), lambda h, bi: (bi, 0, h))
    if local:
        assert pairs_per_step == 1
        in_specs = [pair(lq), pair(lq), pair(lq), pair(lc), pair(lc),
                    pl.BlockSpec((None,) + bias_t.shape[1:], lambda h, bi: (h, 0, 0, 0))]
        args = [q, k, v, kc, vc, bias_t]
    else:
        in_specs = [pair(lq), pair(lc), pair(lc)]
        args = [q, kc, vc]
    return pl.pallas_call(
        functools.partial(_nbr_attn_kernel, n_rows=n_rows, pairs=pairs_per_step, local=local),
        grid=(wdt // blk, b),
        in_specs=in_specs,
        out_specs=pair(lq),
        out_shape=jax.ShapeDtypeStruct((b, lq, wdt), BF16),
        compiler_params=_params(2, 32 * 1024 * 1024),
        name="nbr_attn" if local else "ctx_attn",
    )(*args)


def _nbr_bias_table(rpb):
    qc = np.arange(GRID_W)[:, None]
    kc = np.arange(GRID_W)[None, :]
    win0 = np.clip(qc - NA_COLS // 2, 0, GRID_W - NA_COLS)
    valid = (kc >= win0) & (kc < win0 + NA_COLS)
    dc = np.clip(kc - qc + NA_COLS - 1, 0, 2 * NA_COLS - 2)
    t = jnp.where(valid[None, None], rpb[:, :, dc], -jnp.inf)
    h = t.shape[0]
    dr = np.arange(NA_ROWS)[:, None] + np.arange(NA_ROWS)[None, :]
    t = t[:, dr]
    t = t.reshape(h // 2, 2, NA_ROWS, NA_ROWS, GRID_W, GRID_W).transpose(0, 2, 1, 4, 3, 5)
    return t.reshape(h // 2, NA_ROWS, 2 * GRID_W, NA_ROWS * GRID_W)


def _merge_kernel(x_ref, sh_ref, sc_ref, g1_ref, ng_ref, ya_ref, yb_ref, yc_ref, yd_ref,
                  wg_ref, wb_ref, wo_ref, o_ref):
    x = x_ref[...]
    d = x.shape[-1]
    hb = _norm_modulate(x, ng_ref[...], sh_ref[...], sc_ref[...])
    acc = None
    for i, y_ref in enumerate((ya_ref, yb_ref, yc_ref, yd_ref)):
        gate = _dot(hb, wg_ref[:, i * d:(i + 1) * d])
        term = jax.nn.sigmoid(gate) * _dot(y_ref[...], wb_ref[i * SEG_W:(i + 1) * SEG_W, :])
        acc = term if acc is None else acc + term
    o_ref[...] = x + g1_ref[...] * _dot(acc.astype(BF16), wo_ref[...])


def _merge(x, shift, scale, g1, norm_g, ys, w_gate, w_branch, w_out):
    b, l, d = x.shape
    tm = min(ROW_TILE, l)
    batched = shift.shape[0] != 1
    mod_map = (lambda bi, i: (bi, 0, 0)) if batched else (lambda bi, i: (0, 0, 0))
    x_tile = pl.BlockSpec((None, tm, d), lambda bi, i: (bi, i, 0))
    y_tile = pl.BlockSpec((None, tm, SEG_W), lambda bi, i: (bi, i, 0))
    mod = pl.BlockSpec((None, 1, d), mod_map)
    weights = (w_gate.size + w_branch.size + w_out.size) * 2
    vmem = weights + 2 * (2 * tm * d * 4 + 4 * tm * SEG_W * 2) + 8 * tm * d * 4 + (8 << 20)
    return pl.pallas_call(
        _merge_kernel,
        grid=(b, l // tm),
        in_specs=[x_tile, mod, mod, mod, _resident((1, d)), y_tile, y_tile, y_tile, y_tile,
                  _resident(w_gate.shape), _resident(w_branch.shape), _resident(w_out.shape)],
        out_specs=x_tile,
        out_shape=jax.ShapeDtypeStruct(x.shape, F32),
        compiler_params=_params(2, vmem),
        name="merge",
    )(x, shift, scale, g1, norm_g, *ys, w_gate, w_branch, w_out)


def _ffn_kernel(x_ref, sh_ref, sc_ref, g2_ref, ng_ref, wgu_ref, wd_ref, o_ref, u_ref):
    x = x_ref[...]
    hb = _norm_modulate(x, ng_ref[...], sh_ref[...], sc_ref[...])
    hidden = wd_ref.shape[0]
    for c in range(hidden // FFN_CHUNK):
        cols = slice(c * FFN_CHUNK, (c + 1) * FFN_CHUNK)
        a = _dot(hb, wgu_ref[:, cols])
        bb = _dot(hb, wgu_ref[:, hidden + c * FFN_CHUNK:hidden + (c + 1) * FFN_CHUNK])
        u_ref[:, cols] = (a * jax.nn.sigmoid(a) * bb).astype(BF16)
    o_ref[...] = x + g2_ref[...] * _dot(u_ref[...], wd_ref[...])


def _ffn(x, shift, scale, g2, norm_g, w_gu, w_down):
    b, l, d = x.shape
    tm = min(ROW_TILE, l)
    hidden = w_down.shape[0]
    batched = shift.shape[0] != 1
    mod_map = (lambda bi, i: (bi, 0, 0)) if batched else (lambda bi, i: (0, 0, 0))
    x_tile = pl.BlockSpec((None, tm, d), lambda bi, i: (bi, i, 0))
    mod = pl.BlockSpec((None, 1, d), mod_map)
    vmem = (w_gu.size + w_down.size) * 2 + 4 * tm * d * 4 + tm * hidden * 2 + 8 * tm * d * 4 + (8 << 20)
    return pl.pallas_call(
        _ffn_kernel,
        grid=(b, l // tm),
        in_specs=[x_tile, mod, mod, mod, _resident((1, d)),
                  _resident(w_gu.shape), _resident(w_down.shape)],
        out_specs=x_tile,
        out_shape=jax.ShapeDtypeStruct(x.shape, F32),
        scratch_shapes=[pltpu.VMEM((tm, hidden), BF16)],
        compiler_params=_params(2, vmem),
        name="ffn",
    )(x, shift, scale, g2, norm_g, w_gu, w_down)


def _rope_tables(n_tok):
    nf = A_DH // 4
    t = np.arange(n_tok)
    row = (t // GRID_W).astype(np.float32)
    col = (t % GRID_W).astype(np.float32)
    inv = np.float32(ROPE_THETA) ** (-np.arange(nf, dtype=np.float32) / np.float32(nf))
    ar = row[:, None] * inv
    ac = col[:, None] * inv
    ang = np.concatenate([ar, ar, ac, ac], axis=-1).astype(np.float64)
    cos, sin = np.cos(ang), np.sin(ang)
    quarter = (np.arange(A_DH) // nf) % 2
    sin_up = np.where(quarter == 0, -sin, 0.0)
    sin_dn = np.where(quarter == 1, sin, 0.0)
    rep = V7X_LANES // A_DH
    return tuple(jnp.asarray(np.tile(a, (1, rep)), dtype=F32) for a in (cos, sin_up, sin_dn))


def kernel(x, c, ctx, c_ctx, w_mod, b_mod, norm1_g, w_in, a_qk_g, a_lambda, a_subln_g, b_pool_w,
           b_pool_s, c_qk_g, c_rpb, d_vn_g, d_ws, d_bs, w_branch, w_out, norm2_g, w_gu, w_down):
    b, s, d = x.shape
    depth = w_mod.shape[0]
    rope_tabs = _rope_tables(s)

    rows = -(-(b + 1) // 8) * 8
    c_all = jnp.zeros((rows, d), F32).at[:b].set(c).at[b].set(c_ctx)
    mod = _modulation(c_all, w_mod, b_mod)

    for l in range(depth):
        last = l == depth - 1
        lam_init = 0.8 - 0.6 * math.exp(-0.3 * l)
        mx = [mod[l, :b, k * d:(k + 1) * d].reshape(b, 1, d) for k in range(6)]
        mc = [mod[l, b:b + 1, k * d:(k + 1) * d].reshape(1, 1, d) for k in range(6)]
        w_mix = w_in[l][:, :MIX_COLS].astype(BF16)
        w_gate = w_in[l][:, MIX_COLS:].astype(BF16)
        w_br = w_branch[l].astype(BF16)
        w_o = w_out[l].astype(BF16)
        w_gu_l = w_gu[l].astype(BF16)
        w_dn = w_down[l].astype(BF16)
        w_pool = b_pool_w[l].astype(BF16)
        w_s = d_ws[l].astype(BF16)
        n1 = norm1_g[l].reshape(1, d)
        n2 = norm2_g[l].reshape(1, d)
        rep = SEG_W // A_DH
        qk_gains = jnp.stack([jnp.tile(a_qk_g[l, 0], rep), jnp.tile(a_qk_g[l, 1], rep),
                              jnp.tile(c_qk_g[l, 0], rep), jnp.tile(c_qk_g[l, 1], rep)])
        vn_g = d_vn_g[l].reshape(1, SEG_W)
        bs_full = jnp.repeat(d_bs[l].T, V7X_LANES, axis=1)
        bias_t = _nbr_bias_table(c_rpb[l])

        px = _in_proj(x, mx[0], mx[1], n1, w_mix, qk_gains, vn_g, w_s, bs_full, rope_tabs, MIXER_SEGS)
        pc = _in_proj(ctx, mc[0], mc[1], n1, w_mix, qk_gains, vn_g, w_s, bs_full, None,
                      CTX_KV_SEGS if last else MIXER_SEGS)

        ya = _diff_attn(px[SEG_AQ], [(px[SEG_AK], px[SEG_AV]), (pc[SEG_AK], pc[SEG_AV])],
                        a_lambda[l], a_subln_g[l], lam_init, heads_per_step=1)
        yb = _pool(px[SEG_B], w_pool, b_pool_s[l])
        yc = _nbr_attn(px[SEG_CQ], px[SEG_CK], px[SEG_CV], pc[SEG_CK], pc[SEG_CV], bias_t, pairs_per_step=1)
        x = _merge(x, mx[0], mx[1], mx[2], n1, (ya, yb, yc, px[SEG_YD]), w_gate, w_br, w_o)
        x = _ffn(x, mx[3], mx[4], mx[5], n2, w_gu_l, w_dn)

        if not last:
            ya_c = _diff_attn(pc[SEG_AQ], [(pc[SEG_AK], pc[SEG_AV])], a_lambda[l], a_subln_g[l], lam_init,
                              heads_per_step=A_HEADS)
            yb_c = _pool(pc[SEG_B], w_pool, b_pool_s[l])
            yc_c = _nbr_attn(pc[SEG_CQ], None, None, pc[SEG_CK], pc[SEG_CV], None,
                             pairs_per_step=SEG_W // V7X_LANES)
            ctx = _merge(ctx, mc[0], mc[1], mc[2], n1, (ya_c, yb_c, yc_c, pc[SEG_YD]), w_gate, w_br, w_o)
            ctx = _ffn(ctx, mc[3], mc[4], mc[5], n2, w_gu_l, w_dn)
    return x
```

```python
import functools
import math

import jax
import jax.numpy as jnp
import numpy as np
from jax import lax
from jax.experimental import pallas as pl
from jax.experimental.pallas import tpu as pltpu

F32 = jnp.float32
BF16 = jnp.bfloat16

D_MODEL = 1024
GRID_W = 64
EPS = 1e-6
ROPE_THETA = 10000.0
A_HEADS = 4
A_DH = 64
POOL_WINDOWS = (2, 4, 8, 16)
C_DH = 64
NA_ROWS = 8
NA_COLS = 16
CHUNK = 128
N_BRANCH = 4
SEG_W = 512
N_SEG = 9
MIX_COLS = N_SEG * SEG_W
FFN_HIDDEN = 2816
SEG_AQ, SEG_AK, SEG_AV, SEG_B, SEG_CQ, SEG_CK, SEG_CV, SEG_DU, SEG_DV = range(N_SEG)
SEG_YD = N_SEG
SEG_DTYPES = (BF16, BF16, BF16, F32, BF16, BF16, BF16, F32, BF16, BF16)
MIXER_SEGS = (SEG_AQ, SEG_AK, SEG_AV, SEG_B, SEG_CQ, SEG_CK, SEG_CV, SEG_YD)
CTX_KV_SEGS = (SEG_AK, SEG_AV, SEG_CK, SEG_CV)

V7X_LANES = 128
V7X_VMEM_BYTES = 64 * 1024 * 1024
V7X_VMEM_LIMIT = V7X_VMEM_BYTES - 8 * 1024 * 1024

ROW_TILE = 1024
ATTN_Q_TILE = 512
ATTN_KEY_CHUNK = 768
POOL_TILE = 256
POOL_HALO = 8
FFN_CHUNK = 256


def _params(n_axes, vmem_bytes):
    return pltpu.CompilerParams(
        dimension_semantics=("parallel",) * n_axes,
        vmem_limit_bytes=int(min(vmem_bytes, V7X_VMEM_LIMIT)))


def _resident(shape):
    zeros = (0,) * len(shape)
    return pl.BlockSpec(shape, lambda *_: zeros, pipeline_mode=pl.Buffered(1))


def _dot(a, b):
    return jnp.dot(a, b, preferred_element_type=F32)


def _dot_nt(a, b):
    return lax.dot_general(a, b, (((1,), (1,)), ((), ())), preferred_element_type=F32)


def _dot_tn(a, b):
    return lax.dot_general(a, b, (((0,), (0,)), ((), ())), preferred_element_type=F32)


def _split_bf16(t):
    hi = t.astype(BF16)
    lo = (t - hi.astype(F32)).astype(BF16)
    return hi, lo


def _norm_modulate(x, gain, shift, scale):
    ms = jnp.mean(x * x, axis=-1, keepdims=True)
    n = x * lax.rsqrt(ms + EPS)
    return ((n * gain) * (1.0 + scale) + shift).astype(BF16)


def _mod_kernel(c_ref, w_ref, b_ref, o_ref):
    a = c_ref[...]
    a = a * jax.nn.sigmoid(a)
    a_hi, a_lo = _split_bf16(a)
    w_hi, w_lo = _split_bf16(w_ref[...])
    o_ref[...] = _dot(a_hi, w_hi) + _dot(a_hi, w_lo) + _dot(a_lo, w_hi) + b_ref[...]


def _modulation(c_all, w_mod, b_mod):
    depth, d, n = w_mod.shape
    rows = c_all.shape[0]
    tn = 768
    return pl.pallas_call(
        _mod_kernel,
        grid=(depth, n // tn),
        in_specs=[
            pl.BlockSpec((rows, d), lambda l, j: (0, 0)),
            pl.BlockSpec((None, d, tn), lambda l, j: (l, 0, j)),
            pl.BlockSpec((None, 1, tn), lambda l, j: (l, 0, j)),
        ],
        out_specs=pl.BlockSpec((None, rows, tn), lambda l, j: (l, 0, j)),
        out_shape=jax.ShapeDtypeStruct((depth, rows, n), F32),
        compiler_params=_params(2, 32 * 1024 * 1024),
        name="modulation",
    )(c_all, w_mod, b_mod.reshape(depth, 1, n))


def _half_block_rms(t, first_half, gain):
    sq = t * t
    s_first = jnp.sum(jnp.where(first_half, sq, 0.0), axis=-1, keepdims=True)
    s_second = jnp.sum(jnp.where(first_half, 0.0, sq), axis=-1, keepdims=True)
    ms = jnp.where(first_half, s_first, s_second) * (1.0 / A_DH)
    return t * lax.rsqrt(ms + EPS) * gain


def _rope_block(t, cos, sin_up, sin_dn):
    return (t * cos + pltpu.roll(t, V7X_LANES - 16, 1) * sin_up + pltpu.roll(t, 16, 1) * sin_dn)


def _in_proj_kernel(*refs, rope, out_segs):
    x_ref, sh_ref, sc_ref, ng_ref, w_ref, qkg_ref, vng_ref, ws_ref, bs_ref = refs[:9]
    pos = 9
    if rope:
        cos_ref, sup_ref, sdn_ref = refs[9:12]
        pos = 12
    outs = dict(zip(out_segs, refs[pos:pos + len(out_segs)]))
    gated = SEG_YD in outs
    needed = set(out_segs) | ({SEG_DU, SEG_DV} if gated else set())
    hb = _norm_modulate(x_ref[...], ng_ref[...], sh_ref[...], sc_ref[...])
    qk_row = {SEG_AQ: 0, SEG_AK: 1, SEG_CQ: 2, SEG_CK: 3}
    first_half = lax.broadcasted_iota(jnp.int32, (1, V7X_LANES), 1) < A_DH
    for seg in range(N_SEG):
        if seg not in needed:
            continue
        r = _dot(hb, w_ref[:, seg * SEG_W:(seg + 1) * SEG_W])
        if seg in qk_row:
            row = qk_row[seg]
            blocks = []
            for j in range(SEG_W // V7X_LANES):
                lanes = slice(j * V7X_LANES, (j + 1) * V7X_LANES)
                t = _half_block_rms(r[:, lanes], first_half, qkg_ref[row:row + 1, lanes])
                if rope and seg in (SEG_AQ, SEG_AK):
                    t = _rope_block(t, cos_ref[...], sup_ref[...], sdn_ref[...])
                blocks.append(t)
            r = jnp.concatenate(blocks, axis=1)
        elif seg == SEG_DU and gated:
            gate_u = r
        elif seg == SEG_DV:
            ms = jnp.mean(r * r, axis=-1, keepdims=True)
            r = r * lax.rsqrt(ms + EPS) * vng_ref[...]
            if gated:
                vb = r.astype(BF16)
                n_chunks = r.shape[0] // CHUNK
                for g in range(ws_ref.shape[0]):
                    lanes = slice(g * V7X_LANES, (g + 1) * V7X_LANES)
                    v_wide = jnp.concatenate([vb[n * CHUNK:(n + 1) * CHUNK, lanes] for n in range(n_chunks)], axis=1)
                    sv_wide = _dot(ws_ref[g], v_wide)
                    for n in range(n_chunks):
                        rows = slice(n * CHUNK, (n + 1) * CHUNK)
                        sv = sv_wide[:, n * V7X_LANES:(n + 1) * V7X_LANES] + bs_ref[:, lanes]
                        outs[SEG_YD][rows, lanes] = (gate_u[rows, lanes] * sv).astype(outs[SEG_YD].dtype)
        if seg in outs:
            outs[seg][...] = r.astype(outs[seg].dtype)


def _in_proj(x, shift, scale, norm_g, w_mix, qk_gains, vn_g, w_s, b_full, rope_tabs, out_segs):
    b, l, d = x.shape
    tm = min(ROW_TILE, l)
    batched = shift.shape[0] != 1
    mod_map = (lambda bi, i: (bi, 0, 0)) if batched else (lambda bi, i: (0, 0, 0))
    rope = rope_tabs is not None
    in_specs = [
        pl.BlockSpec((None, tm, d), lambda bi, i: (bi, i, 0)),
        pl.BlockSpec((None, 1, d), mod_map),
        pl.BlockSpec((None, 1, d), mod_map),
        _resident((1, d)),
        _resident((d, MIX_COLS)),
        _resident((4, SEG_W)),
        _resident((1, SEG_W)),
        _resident(w_s.shape),
        _resident(b_full.shape),
    ]
    args = [x, shift, scale, norm_g, w_mix, qk_gains, vn_g, w_s, b_full]
    if rope:
        in_specs += [pl.BlockSpec((tm, V7X_LANES), lambda bi, i: (i, 0))] * 3
        args += list(rope_tabs)
    dtypes = [SEG_DTYPES[s] for s in out_segs]
    out_bytes = sum(tm * SEG_W * jnp.dtype(t).itemsize for t in dtypes)
    vmem = d * MIX_COLS * 2 + 2 * (tm * d * 4 + out_bytes) + 8 * tm * SEG_W * 4 + (8 << 20)
    outs = pl.pallas_call(
        functools.partial(_in_proj_kernel, rope=rope, out_segs=tuple(out_segs)),
        grid=(b, l // tm),
        in_specs=in_specs,
        out_specs=[pl.BlockSpec((None, tm, SEG_W), lambda bi, i: (bi, i, 0))] * len(out_segs),
        out_shape=[jax.ShapeDtypeStruct((b, l, SEG_W), t) for t in dtypes],
        compiler_params=_params(2, vmem),
        name="in_proj_rope" if rope else "in_proj",
    )(*args)
    return dict(zip(out_segs, outs))


def _diff_attn_kernel(*refs, n_parts, lq, tq, heads, lam_init):
    al_ref, sg_ref, q_ref = refs[:3]
    kv = refs[3:3 + 2 * n_parts]
    o_ref, kcat_ref, vt_ref, s_ref, e_ref = refs[3 + 2 * n_parts:]
    al = al_ref[...]
    lam = (jnp.exp(jnp.sum(al[0:1] * al[1:2], axis=-1, keepdims=True))
           - jnp.exp(jnp.sum(al[2:3] * al[3:4], axis=-1, keepdims=True)) + lam_init)
    lane = lax.broadcasted_iota(jnp.int32, (1, V7X_LANES), 1)
    scale = A_DH ** -0.5
    masks = (jnp.where(lane < A_DH, scale, 0.0).astype(BF16),
             jnp.where(lane >= A_DH, scale, 0.0).astype(BF16))
    sub_gain = sg_ref[...] * (1.0 - lam_init)
    hw = 2 * A_DH
    n_keys = kcat_ref.shape[0]
    kc = math.gcd(n_keys, ATTN_KEY_CHUNK)
    chunks = range(0, n_keys, kc)
    vt_ref[hw:, :] = jnp.ones((vt_ref.shape[0] - hw, n_keys), BF16)

    def score_stage(cols, slot, r0):
        qc = q_ref[r0:r0 + tq, cols]
        maxima = []
        for c, mask in enumerate(masks):
            qm = qc * mask
            m8 = None
            for k0 in chunks:
                s = _dot_nt(kcat_ref[k0:k0 + kc, :], qm)
                s_ref[slot, c, k0:k0 + kc, :] = s
                cm = jnp.max(s.reshape(kc // 8, 8, tq), axis=0)
                m8 = cm if m8 is None else jnp.maximum(m8, cm)
            maxima.append(jnp.max(m8, axis=0, keepdims=True))
        return maxima

    def value_stage(cols, slot, r0, maxima):
        acc = []
        for c in range(2):
            for k0 in chunks:
                rows = slice(k0, k0 + kc)
                e_ref[slot, c, rows, :] = jnp.exp(s_ref[slot, c, rows, :] - maxima[c]).astype(BF16)
            acc.append(_dot(vt_ref[...], e_ref[slot, c]))
        o = (acc[0][:hw] * (1.0 / acc[0][hw:hw + 1]) - acc[1][:hw] * (lam / acc[1][hw:hw + 1])).T
        ms = jnp.mean(o * o, axis=-1, keepdims=True)
        o_ref[r0:r0 + tq, cols] = (o * lax.rsqrt(ms + EPS) * sub_gain).astype(o_ref.dtype)

    starts = range(0, lq, tq)
    for h in range(heads):
        cols = slice(h * hw, (h + 1) * hw)
        off = 0
        for p in range(n_parts):
            n = kv[2 * p].shape[0]
            kcat_ref[off:off + n, :] = kv[2 * p][:, cols]
            vt_ref[0:hw, off:off + n] = kv[2 * p + 1][:, cols].astype(F32).T.astype(BF16)
            off += n
        m_next = score_stage(cols, 0, starts[0])
        for j, r0 in enumerate(starts):
            m_cur = m_next
            if j + 1 < len(starts):
                m_next = score_stage(cols, (j + 1) % 2, starts[j + 1])
            value_stage(cols, j % 2, r0, m_cur)


def _diff_attn(q, kv_parts, a_lambda, subln_g, lam_init, heads_per_step):
    b, lq, _ = q.shape
    tq = min(ATTN_Q_TILE, lq)
    hw = 2 * A_DH
    wdt = heads_per_step * hw
    head_spec = lambda n: pl.BlockSpec((None, n, wdt), lambda bi, h: (bi, 0, h))
    in_specs = [_resident(a_lambda.shape), _resident((1, hw)), head_spec(lq)]
    args = [a_lambda, subln_g.reshape(1, hw), q]
    lk = 0
    for k, v in kv_parts:
        in_specs += [head_spec(k.shape[1]), head_spec(v.shape[1])]
        args += [k, v]
        lk += k.shape[1]
    ones_rows = 16
    scratch = [pltpu.VMEM((lk, hw), BF16), pltpu.VMEM((hw + ones_rows, lk), BF16),
               pltpu.VMEM((2, 2, lk, tq), F32), pltpu.VMEM((2, 2, lk, tq), BF16)]
    vmem = (2 * (2 * lq + 4 * lk) * wdt * 2 + 2 * lk * (hw + ones_rows) * 2 + 4 * lk * tq * 6
            + 6 * ATTN_KEY_CHUNK * tq * 4 + (8 << 20))
    return pl.pallas_call(
        functools.partial(_diff_attn_kernel, n_parts=len(kv_parts), lq=lq, tq=tq,
                          heads=heads_per_step, lam_init=lam_init),
        grid=(b, A_HEADS // heads_per_step),
        in_specs=in_specs,
        out_specs=head_spec(lq),
        out_shape=jax.ShapeDtypeStruct((b, lq, A_HEADS * hw), BF16),
        scratch_shapes=scratch,
        compiler_params=_params(2, vmem),
        name="diff_attn",
    )(*args)


def _tree_sum(terms):
    while len(terms) > 1:
        terms = [terms[i] + terms[i + 1] for i in range(0, len(terms), 2)]
    return terms[0]


def _pool_kernel(p_ref, w_ref, s_ref, o_ref, pad_ref, *, l, tp):
    wdt = p_ref.shape[-1]
    pad_ref[0:POOL_HALO, :] = jnp.zeros((POOL_HALO, wdt), F32)
    pad_ref[POOL_HALO + l:, :] = jnp.zeros((POOL_HALO, wdt), F32)
    pad_ref[POOL_HALO:POOL_HALO + l, :] = p_ref[...]
    n_tiles = l // tp
    for ti in range(n_tiles):
        t0 = ti * tp
        edge = ti == 0 or ti == n_tiles - 1
        if edge:
            t_row = t0 + lax.broadcasted_iota(jnp.int32, (tp, V7X_LANES), 0)
        for g, w in enumerate(POOL_WINDOWS):
            lanes = slice(g * V7X_LANES, (g + 1) * V7X_LANES)
            shifted = [pad_ref[POOL_HALO + t0 + j:POOL_HALO + t0 + j + tp, lanes]
                       for j in range(-(w // 2), w // 2)]
            centre = shifted[w // 2]
            total = _tree_sum(shifted)
            if edge:
                cnt = (jnp.minimum(t_row + w // 2, l) - jnp.maximum(t_row - w // 2, 0)).astype(F32)
                mean = total / cnt
            else:
                mean = total * (1.0 / w)
            y = _dot((mean - centre).astype(BF16), w_ref[g]) * s_ref[:, lanes]
            o_ref[t0:t0 + tp, lanes] = y.astype(o_ref.dtype)


def _pool(p, w_pool, s_pool):
    b, l, wdt = p.shape
    tp = min(POOL_TILE, l)
    seq = pl.BlockSpec((None, l, wdt), lambda bi: (bi, 0, 0))
    return pl.pallas_call(
        functools.partial(_pool_kernel, l=l, tp=tp),
        grid=(b,),
        in_specs=[seq, _resident(w_pool.shape), _resident((1, wdt))],
        out_specs=seq,
        out_shape=jax.ShapeDtypeStruct((b, l, wdt), BF16),
        scratch_shapes=[pltpu.VMEM((l + 2 * POOL_HALO, wdt), F32)],
        compiler_params=_params(1, 5 * l * wdt * 4 + (16 << 20)),
        name="pool",
    )(p, w_pool, s_pool.reshape(1, wdt))


def _nbr_attn_kernel(*refs, n_rows, pairs, local):
    if local:
        q_ref, k_ref, v_ref, kc_ref, vc_ref, bias_ref, o_ref = refs
    else:
        q_ref, kc_ref, vc_ref, o_ref = refs
    lane = lax.broadcasted_iota(jnp.int32, (1, V7X_LANES), 1)
    scale = C_DH ** -0.5
    m_first = jnp.where(lane < C_DH, scale, 0.0).astype(BF16)
    m_second = jnp.where(lane >= C_DH, scale, 0.0).astype(BF16)
    lane_out = lax.broadcasted_iota(jnp.int32, (GRID_W, V7X_LANES), 1)
    n_loc = NA_ROWS * GRID_W
    hq = 2 * GRID_W
    with_ones = lambda v: jnp.concatenate([v, jnp.ones_like(v)], axis=1)
    first_key_row = [min(max(r - NA_ROWS // 2, 0), n_rows - NA_ROWS) for r in range(n_rows)]

    for p in range(pairs):
        cols = slice(p * V7X_LANES, (p + 1) * V7X_LANES)
        blocks = []
        for r in range(n_rows):
            qr = q_ref[r * GRID_W:(r + 1) * GRID_W, cols]
            blocks += [qr * m_first, qr * m_second]
        qbd = jnp.concatenate(blocks, axis=0)
        s_ctx = _dot_nt(qbd, kc_ref[:, cols])
        m = jnp.max(s_ctx, axis=-1, keepdims=True)
        if local:
            s_loc = jnp.concatenate(
                [_dot_nt(qbd[r * hq:(r + 1) * hq], k_ref[rs * GRID_W:rs * GRID_W + n_loc, cols])
                 + bias_ref[rs - r + NA_ROWS - 1] for r, rs in enumerate(first_key_row)], axis=0)
            m = jnp.maximum(m, jnp.max(s_loc, axis=-1, keepdims=True))
            e_loc = jnp.exp(s_loc - m).astype(BF16)
        e_ctx = jnp.exp(s_ctx - m)
        full = _dot(e_ctx.astype(BF16), with_ones(vc_ref[:, cols]))
        if local:
            full = full + jnp.concatenate(
                [_dot(e_loc[r * hq:(r + 1) * hq], with_ones(v_ref[rs * GRID_W:rs * GRID_W + n_loc, cols]))
                 for r, rs in enumerate(first_key_row)], axis=0)
        full = full[:, :V7X_LANES] * (1.0 / full[:, V7X_LANES:V7X_LANES + 1])
        out = jnp.concatenate(
            [jnp.where(lane_out < C_DH, full[r * hq:r * hq + GRID_W], full[r * hq + GRID_W:(r + 1) * hq])
             for r in range(n_rows)], axis=0)
        o_ref[:, cols] = out.astype(o_ref.dtype)


def _nbr_attn(q, k, v, kc, vc, bias_t, pairs_per_step):
    b, lq, wdt = q.shape
    lc = kc.shape[1]
    n_rows = lq // GRID_W
    local = k is not None
    blk = pairs_per_step * V7X_LANES
    pair = lambda n: pl.BlockSpec((None, n, blk), lambda h, bi: (bi, 0, h))
    if local:
        assert pairs_per_step == 1
        in_specs = [pair(lq), pair(lq), pair(lq), pair(lc), pair(lc),
                    pl.BlockSpec((None,) + bias_t.shape[1:], lambda h, bi: (h, 0, 0, 0))]
        args = [q, k, v, kc, vc, bias_t]
    else:
        in_specs = [pair(lq), pair(lc), pair(lc)]
        args = [q, kc, vc]
    return pl.pallas_call(
        functools.partial(_nbr_attn_kernel, n_rows=n_rows, pairs=pairs_per_step, local=local),
        grid=(wdt // blk, b),
        in_specs=in_specs,
        out_specs=pair(lq),
        out_shape=jax.ShapeDtypeStruct((b, lq, wdt), BF16),
        compiler_params=_params(2, 32 * 1024 * 1024),
        name="nbr_attn" if local else "ctx_attn",
    )(*args)


def _nbr_bias_table(rpb):
    qc = np.arange(GRID_W)[:, None]
    kc = np.arange(GRID_W)[None, :]
    win0 = np.clip(qc - NA_COLS // 2, 0, GRID_W - NA_COLS)
    valid = (kc >= win0) & (kc < win0 + NA_COLS)
    dc = np.clip(kc - qc + NA_COLS - 1, 0, 2 * NA_COLS - 2)
    t = jnp.where(valid[None, None], rpb[:, :, dc], -jnp.inf)
    h = t.shape[0]
    t = jnp.stack([t[:, d0:d0 + NA_ROWS] for d0 in range(NA_ROWS)], axis=1)
    t = t.reshape(h // 2, 2, NA_ROWS, NA_ROWS, GRID_W, GRID_W).transpose(0, 2, 1, 4, 3, 5)
    return t.reshape(h // 2, NA_ROWS, 2 * GRID_W, NA_ROWS * GRID_W)


def _merge_kernel(x_ref, sh_ref, sc_ref, g1_ref, ng_ref, ya_ref, yb_ref, yc_ref, yd_ref,
                  wg_ref, wb_ref, wo_ref, o_ref):
    x = x_ref[...]
    d = x.shape[-1]
    hb = _norm_modulate(x, ng_ref[...], sh_ref[...], sc_ref[...])
    acc = None
    for i, y_ref in enumerate((ya_ref, yb_ref, yc_ref, yd_ref)):
        gate = _dot(hb, wg_ref[:, i * d:(i + 1) * d])
        term = jax.nn.sigmoid(gate) * _dot(y_ref[...], wb_ref[i * SEG_W:(i + 1) * SEG_W, :])
        acc = term if acc is None else acc + term
    o_ref[...] = x + g1_ref[...] * _dot(acc.astype(BF16), wo_ref[...])


def _merge(x, shift, scale, g1, norm_g, ys, w_gate, w_branch, w_out):
    b, l, d = x.shape
    tm = min(ROW_TILE, l)
    batched = shift.shape[0] != 1
    mod_map = (lambda bi, i: (bi, 0, 0)) if batched else (lambda bi, i: (0, 0, 0))
    x_tile = pl.BlockSpec((None, tm, d), lambda bi, i: (bi, i, 0))
    y_tile = pl.BlockSpec((None, tm, SEG_W), lambda bi, i: (bi, i, 0))
    mod = pl.BlockSpec((None, 1, d), mod_map)
    weights = (w_gate.size + w_branch.size + w_out.size) * 2
    vmem = weights + 2 * (2 * tm * d * 4 + 4 * tm * SEG_W * 2) + 8 * tm * d * 4 + (8 << 20)
    return pl.pallas_call(
        _merge_kernel,
        grid=(b, l // tm),
        in_specs=[x_tile, mod, mod, mod, _resident((1, d)), y_tile, y_tile, y_tile, y_tile,
                  _resident(w_gate.shape), _resident(w_branch.shape), _resident(w_out.shape)],
        out_specs=x_tile,
        out_shape=jax.ShapeDtypeStruct(x.shape, F32),
        compiler_params=_params(2, vmem),
        name="merge",
    )(x, shift, scale, g1, norm_g, *ys, w_gate, w_branch, w_out)


def _ffn_kernel(x_ref, sh_ref, sc_ref, g2_ref, ng_ref, wgu_ref, wd_ref, o_ref, u_ref):
    x = x_ref[...]
    hb = _norm_modulate(x, ng_ref[...], sh_ref[...], sc_ref[...])
    hidden = wd_ref.shape[0]
    for c in range(hidden // FFN_CHUNK):
        cols = slice(c * FFN_CHUNK, (c + 1) * FFN_CHUNK)
        a = _dot(hb, wgu_ref[:, cols])
        bb = _dot(hb, wgu_ref[:, hidden + c * FFN_CHUNK:hidden + (c + 1) * FFN_CHUNK])
        u_ref[:, cols] = (a * jax.nn.sigmoid(a) * bb).astype(BF16)
    o_ref[...] = x + g2_ref[...] * _dot(u_ref[...], wd_ref[...])


def _ffn(x, shift, scale, g2, norm_g, w_gu, w_down):
    b, l, d = x.shape
    tm = min(ROW_TILE, l)
    hidden = w_down.shape[0]
    batched = shift.shape[0] != 1
    mod_map = (lambda bi, i: (bi, 0, 0)) if batched else (lambda bi, i: (0, 0, 0))
    x_tile = pl.BlockSpec((None, tm, d), lambda bi, i: (bi, i, 0))
    mod = pl.BlockSpec((None, 1, d), mod_map)
    vmem = (w_gu.size + w_down.size) * 2 + 4 * tm * d * 4 + tm * hidden * 2 + 8 * tm * d * 4 + (8 << 20)
    return pl.pallas_call(
        _ffn_kernel,
        grid=(b, l // tm),
        in_specs=[x_tile, mod, mod, mod, _resident((1, d)),
                  _resident(w_gu.shape), _resident(w_down.shape)],
        out_specs=x_tile,
        out_shape=jax.ShapeDtypeStruct(x.shape, F32),
        scratch_shapes=[pltpu.VMEM((tm, hidden), BF16)],
        compiler_params=_params(2, vmem),
        name="ffn",
    )(x, shift, scale, g2, norm_g, w_gu, w_down)


def _rope_tables(n_tok):
    nf = A_DH // 4
    t = np.arange(n_tok)
    row = (t // GRID_W).astype(np.float32)
    col = (t % GRID_W).astype(np.float32)
    inv = np.float32(ROPE_THETA) ** (-np.arange(nf, dtype=np.float32) / np.float32(nf))
    ar = row[:, None] * inv
    ac = col[:, None] * inv
    ang = np.concatenate([ar, ar, ac, ac], axis=-1).astype(np.float64)
    cos, sin = np.cos(ang), np.sin(ang)
    quarter = (np.arange(A_DH) // nf) % 2
    sin_up = np.where(quarter == 0, -sin, 0.0)
    sin_dn = np.where(quarter == 1, sin, 0.0)
    rep = V7X_LANES // A_DH
    return tuple(jnp.asarray(np.tile(a, (1, rep)), dtype=F32) for a in (cos, sin_up, sin_dn))


def kernel(x, c, ctx, c_ctx, w_mod, b_mod, norm1_g, w_in, a_qk_g, a_lambda, a_subln_g, b_pool_w,
           b_pool_s, c_qk_g, c_rpb, d_vn_g, d_ws, d_bs, w_branch, w_out, norm2_g, w_gu, w_down):
    b, s, d = x.shape
    depth = w_mod.shape[0]
    rope_tabs = _rope_tables(s)

    rows = -(-(b + 1) // 8) * 8
    c_all = jnp.zeros((rows, d), F32).at[:b].set(c).at[b].set(c_ctx)
    mod = _modulation(c_all, w_mod, b_mod)

    for l in range(depth):
        last = l == depth - 1
        lam_init = 0.8 - 0.6 * math.exp(-0.3 * l)
        mx = [mod[l, :b, k * d:(k + 1) * d].reshape(b, 1, d) for k in range(6)]
        mc = [mod[l, b:b + 1, k * d:(k + 1) * d].reshape(1, 1, d) for k in range(6)]
        w_mix = w_in[l][:, :MIX_COLS].astype(BF16)
        w_gate = w_in[l][:, MIX_COLS:].astype(BF16)
        w_br = w_branch[l].astype(BF16)
        w_o = w_out[l].astype(BF16)
        w_gu_l = w_gu[l].astype(BF16)
        w_dn = w_down[l].astype(BF16)
        w_pool = b_pool_w[l].astype(BF16)
        w_s = d_ws[l].astype(BF16)
        n1 = norm1_g[l].reshape(1, d)
        n2 = norm2_g[l].reshape(1, d)
        rep = SEG_W // A_DH
        qk_gains = jnp.stack([jnp.tile(a_qk_g[l, 0], rep), jnp.tile(a_qk_g[l, 1], rep),
                              jnp.tile(c_qk_g[l, 0], rep), jnp.tile(c_qk_g[l, 1], rep)])
        vn_g = d_vn_g[l].reshape(1, SEG_W)
        bs_full = jnp.repeat(d_bs[l].T, V7X_LANES, axis=1)
        bias_t = _nbr_bias_table(c_rpb[l])

        px = _in_proj(x, mx[0], mx[1], n1, w_mix, qk_gains, vn_g, w_s, bs_full, rope_tabs, MIXER_SEGS)
        pc = _in_proj(ctx, mc[0], mc[1], n1, w_mix, qk_gains, vn_g, w_s, bs_full, None,
                      CTX_KV_SEGS if last else MIXER_SEGS)

        ya = _diff_attn(px[SEG_AQ], [(px[SEG_AK], px[SEG_AV]), (pc[SEG_AK], pc[SEG_AV])],
                        a_lambda[l], a_subln_g[l], lam_init, heads_per_step=1)
        yb = _pool(px[SEG_B], w_pool, b_pool_s[l])
        yc = _nbr_attn(px[SEG_CQ], px[SEG_CK], px[SEG_CV], pc[SEG_CK], pc[SEG_CV], bias_t, pairs_per_step=1)
        x = _merge(x, mx[0], mx[1], mx[2], n1, (ya, yb, yc, px[SEG_YD]), w_gate, w_br, w_o)
        x = _ffn(x, mx[3], mx[4], mx[5], n2, w_gu_l, w_dn)

        if not last:
            ya_c = _diff_attn(pc[SEG_AQ], [(pc[SEG_AK], pc[SEG_AV])], a_lambda[l], a_subln_g[l], lam_init,
                              heads_per_step=A_HEADS)
            yb_c = _pool(pc[SEG_B], w_pool, b_pool_s[l])
            yc_c = _nbr_attn(pc[SEG_CQ], None, None, pc[SEG_CK], pc[SEG_CV], None,
                             pairs_per_step=SEG_W // V7X_LANES)
            ctx = _merge(ctx, mc[0], mc[1], mc[2], n1, (ya_c, yb_c, yc_c, pc[SEG_YD]), w_gate, w_br, w_o)
            ctx = _ffn(ctx, mc[3], mc[4], mc[5], n2, w_gu_l, w_dn)
    return x
```

```python
import functools
import math

import jax
import jax.numpy as jnp
import numpy as np
from jax import lax
from jax.experimental import pallas as pl
from jax.experimental.pallas import tpu as pltpu

F32 = jnp.float32
BF16 = jnp.bfloat16

D_MODEL = 1024
GRID_W = 64
EPS = 1e-6
ROPE_THETA = 10000.0
A_HEADS = 4
A_DH = 64
POOL_WINDOWS = (2, 4, 8, 16)
C_DH = 64
NA_ROWS = 8
NA_COLS = 16
CHUNK = 128
N_BRANCH = 4
SEG_W = 512
N_SEG = 9
MIX_COLS = N_SEG * SEG_W
FFN_HIDDEN = 2816
SEG_AQ, SEG_AK, SEG_AV, SEG_B, SEG_CQ, SEG_CK, SEG_CV, SEG_DU, SEG_DV = range(N_SEG)
SEG_YD = N_SEG
SEG_DTYPES = (BF16, BF16, BF16, F32, BF16, BF16, BF16, F32, BF16, BF16)
MIXER_SEGS = (SEG_AQ, SEG_AK, SEG_AV, SEG_B, SEG_CQ, SEG_CK, SEG_CV, SEG_YD)
CTX_KV_SEGS = (SEG_AK, SEG_AV, SEG_CK, SEG_CV)

V7X_LANES = 128
V7X_VMEM_BYTES = 64 * 1024 * 1024
V7X_VMEM_LIMIT = V7X_VMEM_BYTES - 8 * 1024 * 1024

ROW_TILE = 1024
ATTN_Q_TILE = 256
ATTN_KEY_CHUNK = 256
SCORE_BOUND_SLACK = 1.02
DENOM_FLOOR = 1e-18
POOL_TILE = 256
POOL_HALO = 8
FFN_CHUNK = 256


def _params(n_axes, vmem_bytes):
    return pltpu.CompilerParams(
        dimension_semantics=("parallel",) * n_axes,
        vmem_limit_bytes=int(min(vmem_bytes, V7X_VMEM_LIMIT)))


def _resident(shape):
    zeros = (0,) * len(shape)
    return pl.BlockSpec(shape, lambda *_: zeros, pipeline_mode=pl.Buffered(1))


def _dot(a, b):
    return jnp.dot(a, b, preferred_element_type=F32)


def _dot_nt(a, b):
    return lax.dot_general(a, b, (((1,), (1,)), ((), ())), preferred_element_type=F32)


def _dot_tn(a, b):
    return lax.dot_general(a, b, (((0,), (0,)), ((), ())), preferred_element_type=F32)


def _split_bf16(t):
    hi = t.astype(BF16)
    lo = (t - hi.astype(F32)).astype(BF16)
    return hi, lo


def _norm_modulate(x, gain, shift, scale):
    ms = jnp.mean(x * x, axis=-1, keepdims=True)
    n = x * lax.rsqrt(ms + EPS)
    return ((n * gain) * (1.0 + scale) + shift).astype(BF16)


def _mod_kernel(c_ref, w_ref, b_ref, o_ref):
    a = c_ref[...]
    a = a * jax.nn.sigmoid(a)
    a_hi, a_lo = _split_bf16(a)
    w_hi, w_lo = _split_bf16(w_ref[...])
    o_ref[...] = _dot(a_hi, w_hi) + _dot(a_hi, w_lo) + _dot(a_lo, w_hi) + b_ref[...]


def _modulation(c_all, w_mod, b_mod):
    depth, d, n = w_mod.shape
    rows = c_all.shape[0]
    tn = 768
    return pl.pallas_call(
        _mod_kernel,
        grid=(depth, n // tn),
        in_specs=[
            pl.BlockSpec((rows, d), lambda l, j: (0, 0)),
            pl.BlockSpec((None, d, tn), lambda l, j: (l, 0, j)),
            pl.BlockSpec((None, 1, tn), lambda l, j: (l, 0, j)),
        ],
        out_specs=pl.BlockSpec((None, rows, tn), lambda l, j: (l, 0, j)),
        out_shape=jax.ShapeDtypeStruct((depth, rows, n), F32),
        compiler_params=_params(2, 32 * 1024 * 1024),
        name="modulation",
    )(c_all, w_mod, b_mod.reshape(depth, 1, n))


def _half_block_rms(t, first_half, gain):
    sq = t * t
    s_first = jnp.sum(jnp.where(first_half, sq, 0.0), axis=-1, keepdims=True)
    s_second = jnp.sum(jnp.where(first_half, 0.0, sq), axis=-1, keepdims=True)
    ms = jnp.where(first_half, s_first, s_second) * (1.0 / A_DH)
    return t * lax.rsqrt(ms + EPS) * gain


def _rope_block(t, cos, sin_up, sin_dn):
    return (t * cos + pltpu.roll(t, V7X_LANES - 16, 1) * sin_up + pltpu.roll(t, 16, 1) * sin_dn)


def _in_proj_kernel(*refs, rope, out_segs):
    x_ref, sh_ref, sc_ref, ng_ref, w_ref, qkg_ref, vng_ref, ws_ref, bs_ref = refs[:9]
    pos = 9
    if rope:
        cos_ref, sup_ref, sdn_ref = refs[9:12]
        pos = 12
    outs = dict(zip(out_segs, refs[pos:pos + len(out_segs)]))
    gated = SEG_YD in outs
    needed = set(out_segs) | ({SEG_DU, SEG_DV} if gated else set())
    hb = _norm_modulate(x_ref[...], ng_ref[...], sh_ref[...], sc_ref[...])
    qk_row = {SEG_AQ: 0, SEG_AK: 1, SEG_CQ: 2, SEG_CK: 3}
    first_half = lax.broadcasted_iota(jnp.int32, (1, V7X_LANES), 1) < A_DH
    for seg in range(N_SEG):
        if seg not in needed:
            continue
        r = _dot(hb, w_ref[:, seg * SEG_W:(seg + 1) * SEG_W])
        if seg in qk_row:
            row = qk_row[seg]
            blocks = []
            for j in range(SEG_W // V7X_LANES):
                lanes = slice(j * V7X_LANES, (j + 1) * V7X_LANES)
                t = _half_block_rms(r[:, lanes], first_half, qkg_ref[row:row + 1, lanes])
                if rope and seg in (SEG_AQ, SEG_AK):
                    t = _rope_block(t, cos_ref[...], sup_ref[...], sdn_ref[...])
                blocks.append(t)
            r = jnp.concatenate(blocks, axis=1)
        elif seg == SEG_DU and gated:
            gate_u = r
        elif seg == SEG_DV:
            ms = jnp.mean(r * r, axis=-1, keepdims=True)
            r = r * lax.rsqrt(ms + EPS) * vng_ref[...]
            if gated:
                vb = r.astype(BF16)
                n_chunks = r.shape[0] // CHUNK
                for g in range(ws_ref.shape[0]):
                    lanes = slice(g * V7X_LANES, (g + 1) * V7X_LANES)
                    v_wide = jnp.concatenate([vb[n * CHUNK:(n + 1) * CHUNK, lanes] for n in range(n_chunks)], axis=1)
                    sv_wide = _dot(ws_ref[g], v_wide)
                    for n in range(n_chunks):
                        rows = slice(n * CHUNK, (n + 1) * CHUNK)
                        sv = sv_wide[:, n * V7X_LANES:(n + 1) * V7X_LANES] + bs_ref[:, lanes]
                        outs[SEG_YD][rows, lanes] = (gate_u[rows, lanes] * sv).astype(outs[SEG_YD].dtype)
        if seg in outs:
            outs[seg][...] = r.astype(outs[seg].dtype)


def _in_proj(x, shift, scale, norm_g, w_mix, qk_gains, vn_g, w_s, b_full, rope_tabs, out_segs):
    b, l, d = x.shape
    tm = min(ROW_TILE, l)
    batched = shift.shape[0] != 1
    mod_map = (lambda bi, i: (bi, 0, 0)) if batched else (lambda bi, i: (0, 0, 0))
    rope = rope_tabs is not None
    in_specs = [
        pl.BlockSpec((None, tm, d), lambda bi, i: (bi, i, 0)),
        pl.BlockSpec((None, 1, d), mod_map),
        pl.BlockSpec((None, 1, d), mod_map),
        _resident((1, d)),
        _resident((d, MIX_COLS)),
        _resident((4, SEG_W)),
        _resident((1, SEG_W)),
        _resident(w_s.shape),
        _resident(b_full.shape),
    ]
    args = [x, shift, scale, norm_g, w_mix, qk_gains, vn_g, w_s, b_full]
    if rope:
        in_specs += [pl.BlockSpec((tm, V7X_LANES), lambda bi, i: (i, 0))] * 3
        args += list(rope_tabs)
    dtypes = [SEG_DTYPES[s] for s in out_segs]
    out_bytes = sum(tm * SEG_W * jnp.dtype(t).itemsize for t in dtypes)
    vmem = d * MIX_COLS * 2 + 2 * (tm * d * 4 + out_bytes) + 8 * tm * SEG_W * 4 + (8 << 20)
    outs = pl.pallas_call(
        functools.partial(_in_proj_kernel, rope=rope, out_segs=tuple(out_segs)),
        grid=(b, l // tm),
        in_specs=in_specs,
        out_specs=[pl.BlockSpec((None, tm, SEG_W), lambda bi, i: (bi, i, 0))] * len(out_segs),
        out_shape=[jax.ShapeDtypeStruct((b, l, SEG_W), t) for t in dtypes],
        compiler_params=_params(2, vmem),
        name="in_proj_rope" if rope else "in_proj",
    )(*args)
    return dict(zip(out_segs, outs))


def _diff_attn_kernel(*refs, n_parts, lq, tq, heads, lam_init):
    al_ref, sg_ref, qkg_ref, q_ref = refs[:4]
    kv = refs[4:4 + 2 * n_parts]
    o_ref, kcat_ref, vt_ref, s_ref, e_ref = refs[4 + 2 * n_parts:]
    al = al_ref[...]
    lam = (jnp.exp(jnp.sum(al[0:1] * al[1:2], axis=-1, keepdims=True))
           - jnp.exp(jnp.sum(al[2:3] * al[3:4], axis=-1, keepdims=True)) + lam_init)
    lane = lax.broadcasted_iota(jnp.int32, (1, V7X_LANES), 1)
    scale = A_DH ** -0.5
    masks = (jnp.where(lane < A_DH, scale, 0.0).astype(BF16),
             jnp.where(lane >= A_DH, scale, 0.0).astype(BF16))
    sub_gain = sg_ref[...] * (1.0 - lam_init)
    gains = jnp.max(jnp.abs(qkg_ref[...]), axis=-1, keepdims=True)
    bound = gains[0:1] * gains[1:2] * (A_DH * scale * SCORE_BOUND_SLACK)
    hw = 2 * A_DH
    n_keys = kcat_ref.shape[0]
    kc = math.gcd(n_keys, ATTN_KEY_CHUNK)
    chunks = range(0, n_keys, kc)
    n_tiles = lq // tq
    vt_ref[hw:, :] = jnp.ones((vt_ref.shape[0] - hw, n_keys), BF16)

    def finish(rows, cols, acc):
        o = (acc[0][:hw] * (1.0 / acc[0][hw:hw + 1]) - acc[1][:hw] * (lam / acc[1][hw:hw + 1])).T
        ms = jnp.mean(o * o, axis=-1, keepdims=True)
        o_ref[rows, cols] = (o * lax.rsqrt(ms + EPS) * sub_gain).astype(o_ref.dtype)

    def bounded_tile(cols, slot, r0):
        qc = q_ref[r0:r0 + tq, cols]
        acc = []
        for c, mask in enumerate(masks):
            qm = qc * mask
            for k0 in chunks:
                s = _dot_nt(kcat_ref[k0:k0 + kc, :], qm)
                e_ref[slot, c, k0:k0 + kc, :] = jnp.exp(s - bound).astype(BF16)
            acc.append(_dot(vt_ref[...], e_ref[slot, c]))
        finish(slice(r0, r0 + tq), cols, acc)
        return jnp.minimum(acc[0][hw:hw + 1], acc[1][hw:hw + 1])

    def exact_tile(cols, r0):
        qc = q_ref[pl.ds(r0, tq), cols]
        acc = []
        for c, mask in enumerate(masks):
            qm = qc * mask
            m8 = None
            for k0 in chunks:
                s = _dot_nt(kcat_ref[k0:k0 + kc, :], qm)
                s_ref[c, k0:k0 + kc, :] = s
                cm = jnp.max(s.reshape(kc // 8, 8, tq), axis=0)
                m8 = cm if m8 is None else jnp.maximum(m8, cm)
            m = jnp.max(m8, axis=0, keepdims=True)
            for k0 in chunks:
                rows = slice(k0, k0 + kc)
                e_ref[0, c, rows, :] = jnp.exp(s_ref[c, rows, :] - m).astype(BF16)
            acc.append(_dot(vt_ref[...], e_ref[0, c]))
        finish(pl.ds(r0, tq), cols, acc)

    for h in range(heads):
        cols = slice(h * hw, (h + 1) * hw)
        off = 0
        for p in range(n_parts):
            n = kv[2 * p].shape[0]
            kcat_ref[off:off + n, :] = kv[2 * p][:, cols]
            vt_ref[0:hw, off:off + n] = kv[2 * p + 1][:, cols].astype(F32).T.astype(BF16)
            off += n
        smallest = None
        for j in range(n_tiles):
            denom = bounded_tile(cols, j % 2, j * tq)
            smallest = denom if smallest is None else jnp.minimum(smallest, denom)

        @pl.when(jnp.min(smallest) < DENOM_FLOOR)
        def _():
            def redo(j, carry):
                exact_tile(cols, pl.multiple_of(j * tq, tq))
                return carry
            lax.fori_loop(0, n_tiles, redo, 0)


def _diff_attn(q, kv_parts, a_lambda, subln_g, qk_g, lam_init, heads_per_step):
    b, lq, _ = q.shape
    tq = min(ATTN_Q_TILE, lq)
    hw = 2 * A_DH
    wdt = heads_per_step * hw
    head_spec = lambda n: pl.BlockSpec((None, n, wdt), lambda bi, h: (bi, 0, h))
    in_specs = [_resident(a_lambda.shape), _resident((1, hw)), _resident(qk_g.shape), head_spec(lq)]
    args = [a_lambda, subln_g.reshape(1, hw), qk_g, q]
    lk = 0
    for k, v in kv_parts:
        in_specs += [head_spec(k.shape[1]), head_spec(v.shape[1])]
        args += [k, v]
        lk += k.shape[1]
    ones_rows = 16
    scratch = [pltpu.VMEM((lk, hw), BF16), pltpu.VMEM((hw + ones_rows, lk), BF16),
               pltpu.VMEM((2, lk, tq), F32), pltpu.VMEM((2, 2, lk, tq), BF16)]
    vmem = (2 * (2 * lq + 4 * lk) * wdt * 2 + 2 * lk * (hw + ones_rows) * 2 + 2 * lk * tq * 8
            + 6 * ATTN_KEY_CHUNK * tq * 4 + (8 << 20))
    return pl.pallas_call(
        functools.partial(_diff_attn_kernel, n_parts=len(kv_parts), lq=lq, tq=tq,
                          heads=heads_per_step, lam_init=lam_init),
        grid=(b, A_HEADS // heads_per_step),
        in_specs=in_specs,
        out_specs=head_spec(lq),
        out_shape=jax.ShapeDtypeStruct((b, lq, A_HEADS * hw), BF16),
        scratch_shapes=scratch,
        compiler_params=_params(2, vmem),
        name="diff_attn",
    )(*args)


def _tree_sum(terms):
    while len(terms) > 1:
        terms = [terms[i] + terms[i + 1] for i in range(0, len(terms), 2)]
    return terms[0]


def _pool_kernel(p_ref, w_ref, s_ref, o_ref, pad_ref, *, l, tp):
    wdt = p_ref.shape[-1]
    pad_ref[0:POOL_HALO, :] = jnp.zeros((POOL_HALO, wdt), F32)
    pad_ref[POOL_HALO + l:, :] = jnp.zeros((POOL_HALO, wdt), F32)
    pad_ref[POOL_HALO:POOL_HALO + l, :] = p_ref[...]
    n_tiles = l // tp
    for ti in range(n_tiles):
        t0 = ti * tp
        edge = ti == 0 or ti == n_tiles - 1
        if edge:
            t_row = t0 + lax.broadcasted_iota(jnp.int32, (tp, V7X_LANES), 0)
        for g, w in enumerate(POOL_WINDOWS):
            lanes = slice(g * V7X_LANES, (g + 1) * V7X_LANES)
            shifted = [pad_ref[POOL_HALO + t0 + j:POOL_HALO + t0 + j + tp, lanes]
                       for j in range(-(w // 2), w // 2)]
            centre = shifted[w // 2]
            total = _tree_sum(shifted)
            if edge:
                cnt = (jnp.minimum(t_row + w // 2, l) - jnp.maximum(t_row - w // 2, 0)).astype(F32)
                mean = total / cnt
            else:
                mean = total * (1.0 / w)
            y = _dot((mean - centre).astype(BF16), w_ref[g]) * s_ref[:, lanes]
            o_ref[t0:t0 + tp, lanes] = y.astype(o_ref.dtype)


def _pool(p, w_pool, s_pool):
    b, l, wdt = p.shape
    tp = min(POOL_TILE, l)
    seq = pl.BlockSpec((None, l, wdt), lambda bi: (bi, 0, 0))
    return pl.pallas_call(
        functools.partial(_pool_kernel, l=l, tp=tp),
        grid=(b,),
        in_specs=[seq, _resident(w_pool.shape), _resident((1, wdt))],
        out_specs=seq,
        out_shape=jax.ShapeDtypeStruct((b, l, wdt), BF16),
        scratch_shapes=[pltpu.VMEM((l + 2 * POOL_HALO, wdt), F32)],
        compiler_params=_params(1, 5 * l * wdt * 4 + (16 << 20)),
        name="pool",
    )(p, w_pool, s_pool.reshape(1, wdt))


def _nbr_attn_kernel(*refs, n_rows, pairs, local):
    if local:
        q_ref, k_ref, v_ref, kc_ref, vc_ref, bias_ref, o_ref = refs
    else:
        q_ref, kc_ref, vc_ref, o_ref = refs
    lane = lax.broadcasted_iota(jnp.int32, (1, V7X_LANES), 1)
    scale = C_DH ** -0.5
    m_first = jnp.where(lane < C_DH, scale, 0.0).astype(BF16)
    m_second = jnp.where(lane >= C_DH, scale, 0.0).astype(BF16)
    lane_out = lax.broadcasted_iota(jnp.int32, (GRID_W, V7X_LANES), 1)
    n_loc = NA_ROWS * GRID_W
    hq = 2 * GRID_W
    with_ones = lambda v: jnp.concatenate([v, jnp.ones_like(v)], axis=1)
    first_key_row = [min(max(r - NA_ROWS // 2, 0), n_rows - NA_ROWS) for r in range(n_rows)]

    for p in range(pairs):
        cols = slice(p * V7X_LANES, (p + 1) * V7X_LANES)
        blocks = []
        for r in range(n_rows):
            qr = q_ref[r * GRID_W:(r + 1) * GRID_W, cols]
            blocks += [qr * m_first, qr * m_second]
        qbd = jnp.concatenate(blocks, axis=0)
        s_ctx = _dot_nt(qbd, kc_ref[:, cols])
        m = jnp.max(s_ctx, axis=-1, keepdims=True)
        if local:
            s_loc = jnp.concatenate(
                [_dot_nt(qbd[r * hq:(r + 1) * hq], k_ref[rs * GRID_W:rs * GRID_W + n_loc, cols])
                 + bias_ref[rs - r + NA_ROWS - 1] for r, rs in enumerate(first_key_row)], axis=0)
            m = jnp.maximum(m, jnp.max(s_loc, axis=-1, keepdims=True))
            e_loc = jnp.exp(s_loc - m).astype(BF16)
        e_ctx = jnp.exp(s_ctx - m)
        full = _dot(e_ctx.astype(BF16), with_ones(vc_ref[:, cols]))
        if local:
            full = full + jnp.concatenate(
                [_dot(e_loc[r * hq:(r + 1) * hq], with_ones(v_ref[rs * GRID_W:rs * GRID_W + n_loc, cols]))
                 for r, rs in enumerate(first_key_row)], axis=0)
        full = full[:, :V7X_LANES] * (1.0 / full[:, V7X_LANES:V7X_LANES + 1])
        out = jnp.concatenate(
            [jnp.where(lane_out < C_DH, full[r * hq:r * hq + GRID_W], full[r * hq + GRID_W:(r + 1) * hq])
             for r in range(n_rows)], axis=0)
        o_ref[:, cols] = out.astype(o_ref.dtype)


def _nbr_attn(q, k, v, kc, vc, bias_t, pairs_per_step):
    b, lq, wdt = q.shape
    lc = kc.shape[1]
    n_rows = lq // GRID_W
    local = k is not None
    blk = pairs_per_step * V7X_LANES
    pair = lambda n: pl.BlockSpec((None, n, blk), lambda h, bi: (bi, 0, h))
    if local:
        assert pairs_per_step == 1
        in_specs = [pair(lq), pair(lq), pair(lq), pair(lc), pair(lc),
                    pl.BlockSpec((None,) + bias_t.shape[1:], lambda h, bi: (h, 0, 0, 0))]
        args = [q, k, v, kc, vc, bias_t]
    else:
        in_specs = [pair(lq), pair(lc), pair(lc)]
        args = [q, kc, vc]
    return pl.pallas_call(
        functools.partial(_nbr_attn_kernel, n_rows=n_rows, pairs=pairs_per_step, local=local),
        grid=(wdt // blk, b),
        in_specs=in_specs,
        out_specs=pair(lq),
        out_shape=jax.ShapeDtypeStruct((b, lq, wdt), BF16),
        compiler_params=_params(2, 32 * 1024 * 1024),
        name="nbr_attn" if local else "ctx_attn",
    )(*args)


def _nbr_bias_table(rpb):
    qc = np.arange(GRID_W)[:, None]
    kc = np.arange(GRID_W)[None, :]
    win0 = np.clip(qc - NA_COLS // 2, 0, GRID_W - NA_COLS)
    valid = (kc >= win0) & (kc < win0 + NA_COLS)
    dc = np.clip(kc - qc + NA_COLS - 1, 0, 2 * NA_COLS - 2)
    t = jnp.where(valid[None, None], rpb[:, :, dc], -jnp.inf)
    h = t.shape[0]
    dr = np.arange(NA_ROWS)[:, None] + np.arange(NA_ROWS)[None, :]
    t = t[:, dr]
    t = t.reshape(h // 2, 2, NA_ROWS, NA_ROWS, GRID_W, GRID_W).transpose(0, 2, 1, 4, 3, 5)
    return t.reshape(h // 2, NA_ROWS, 2 * GRID_W, NA_ROWS * GRID_W)


def _merge_kernel(x_ref, sh_ref, sc_ref, g1_ref, ng_ref, ya_ref, yb_ref, yc_ref, yd_ref,
                  wg_ref, wb_ref, wo_ref, o_ref):
    x = x_ref[...]
    d = x.shape[-1]
    hb = _norm_modulate(x, ng_ref[...], sh_ref[...], sc_ref[...])
    acc = None
    for i, y_ref in enumerate((ya_ref, yb_ref, yc_ref, yd_ref)):
        gate = _dot(hb, wg_ref[:, i * d:(i + 1) * d])
        term = jax.nn.sigmoid(gate) * _dot(y_ref[...], wb_ref[i * SEG_W:(i + 1) * SEG_W, :])
        acc = term if acc is None else acc + term
    o_ref[...] = x + g1_ref[...] * _dot(acc.astype(BF16), wo_ref[...])


def _merge(x, shift, scale, g1, norm_g, ys, w_gate, w_branch, w_out):
    b, l, d = x.shape
    tm = min(ROW_TILE, l)
    batched = shift.shape[0] != 1
    mod_map = (lambda bi, i: (bi, 0, 0)) if batched else (lambda bi, i: (0, 0, 0))
    x_tile = pl.BlockSpec((None, tm, d), lambda bi, i: (bi, i, 0))
    y_tile = pl.BlockSpec((None, tm, SEG_W), lambda bi, i: (bi, i, 0))
    mod = pl.BlockSpec((None, 1, d), mod_map)
    weights = (w_gate.size + w_branch.size + w_out.size) * 2
    vmem = weights + 2 * (2 * tm * d * 4 + 4 * tm * SEG_W * 2) + 8 * tm * d * 4 + (8 << 20)
    return pl.pallas_call(
        _merge_kernel,
        grid=(b, l // tm),
        in_specs=[x_tile, mod, mod, mod, _resident((1, d)), y_tile, y_tile, y_tile, y_tile,
                  _resident(w_gate.shape), _resident(w_branch.shape), _resident(w_out.shape)],
        out_specs=x_tile,
        out_shape=jax.ShapeDtypeStruct(x.shape, F32),
        compiler_params=_params(2, vmem),
        name="merge",
    )(x, shift, scale, g1, norm_g, *ys, w_gate, w_branch, w_out)


def _ffn_kernel(x_ref, sh_ref, sc_ref, g2_ref, ng_ref, wgu_ref, wd_ref, o_ref, u_ref):
    x = x_ref[...]
    hb = _norm_modulate(x, ng_ref[...], sh_ref[...], sc_ref[...])
    hidden = wd_ref.shape[0]
    for c in range(hidden // FFN_CHUNK):
        cols = slice(c * FFN_CHUNK, (c + 1) * FFN_CHUNK)
        a = _dot(hb, wgu_ref[:, cols])
        bb = _dot(hb, wgu_ref[:, hidden + c * FFN_CHUNK:hidden + (c + 1) * FFN_CHUNK])
        u_ref[:, cols] = (a * jax.nn.sigmoid(a) * bb).astype(BF16)
    o_ref[...] = x + g2_ref[...] * _dot(u_ref[...], wd_ref[...])


def _ffn(x, shift, scale, g2, norm_g, w_gu, w_down):
    b, l, d = x.shape
    tm = min(ROW_TILE, l)
    hidden = w_down.shape[0]
    batched = shift.shape[0] != 1
    mod_map = (lambda bi, i: (bi, 0, 0)) if batched else (lambda bi, i: (0, 0, 0))
    x_tile = pl.BlockSpec((None, tm, d), lambda bi, i: (bi, i, 0))
    mod = pl.BlockSpec((None, 1, d), mod_map)
    vmem = (w_gu.size + w_down.size) * 2 + 4 * tm * d * 4 + tm * hidden * 2 + 8 * tm * d * 4 + (8 << 20)
    return pl.pallas_call(
        _ffn_kernel,
        grid=(b, l // tm),
        in_specs=[x_tile, mod, mod, mod, _resident((1, d)),
                  _resident(w_gu.shape), _resident(w_down.shape)],
        out_specs=x_tile,
        out_shape=jax.ShapeDtypeStruct(x.shape, F32),
        scratch_shapes=[pltpu.VMEM((tm, hidden), BF16)],
        compiler_params=_params(2, vmem),
        name="ffn",
    )(x, shift, scale, g2, norm_g, w_gu, w_down)


def _rope_tables(n_tok):
    nf = A_DH // 4
    t = np.arange(n_tok)
    row = (t // GRID_W).astype(np.float32)
    col = (t % GRID_W).astype(np.float32)
    inv = np.float32(ROPE_THETA) ** (-np.arange(nf, dtype=np.float32) / np.float32(nf))
    ar = row[:, None] * inv
    ac = col[:, None] * inv
    ang = np.concatenate([ar, ar, ac, ac], axis=-1).astype(np.float64)
    cos, sin = np.cos(ang), np.sin(ang)
    quarter = (np.arange(A_DH) // nf) % 2
    sin_up = np.where(quarter == 0, -sin, 0.0)
    sin_dn = np.where(quarter == 1, sin, 0.0)
    rep = V7X_LANES // A_DH
    return tuple(jnp.asarray(np.tile(a, (1, rep)), dtype=F32) for a in (cos, sin_up, sin_dn))


def kernel(x, c, ctx, c_ctx, w_mod, b_mod, norm1_g, w_in, a_qk_g, a_lambda, a_subln_g, b_pool_w,
           b_pool_s, c_qk_g, c_rpb, d_vn_g, d_ws, d_bs, w_branch, w_out, norm2_g, w_gu, w_down):
    b, s, d = x.shape
    depth = w_mod.shape[0]
    rope_tabs = _rope_tables(s)

    rows = -(-(b + 1) // 8) * 8
    c_all = jnp.zeros((rows, d), F32).at[:b].set(c).at[b].set(c_ctx)
    mod = _modulation(c_all, w_mod, b_mod)

    for l in range(depth):
        last = l == depth - 1
        lam_init = 0.8 - 0.6 * math.exp(-0.3 * l)
        mx = [mod[l, :b, k * d:(k + 1) * d].reshape(b, 1, d) for k in range(6)]
        mc = [mod[l, b:b + 1, k * d:(k + 1) * d].reshape(1, 1, d) for k in range(6)]
        w_mix = w_in[l][:, :MIX_COLS].astype(BF16)
        w_gate = w_in[l][:, MIX_COLS:].astype(BF16)
        w_br = w_branch[l].astype(BF16)
        w_o = w_out[l].astype(BF16)
        w_gu_l = w_gu[l].astype(BF16)
        w_dn = w_down[l].astype(BF16)
        w_pool = b_pool_w[l].astype(BF16)
        w_s = d_ws[l].astype(BF16)
        n1 = norm1_g[l].reshape(1, d)
        n2 = norm2_g[l].reshape(1, d)
        rep = SEG_W // A_DH
        qk_gains = jnp.stack([jnp.tile(a_qk_g[l, 0], rep), jnp.tile(a_qk_g[l, 1], rep),
                              jnp.tile(c_qk_g[l, 0], rep), jnp.tile(c_qk_g[l, 1], rep)])
        vn_g = d_vn_g[l].reshape(1, SEG_W)
        bs_full = jnp.repeat(d_bs[l].T, V7X_LANES, axis=1)
        bias_t = _nbr_bias_table(c_rpb[l])

        px = _in_proj(x, mx[0], mx[1], n1, w_mix, qk_gains, vn_g, w_s, bs_full, rope_tabs, MIXER_SEGS)
        pc = _in_proj(ctx, mc[0], mc[1], n1, w_mix, qk_gains, vn_g, w_s, bs_full, None,
                      CTX_KV_SEGS if last else MIXER_SEGS)

        ya = _diff_attn(px[SEG_AQ], [(px[SEG_AK], px[SEG_AV]), (pc[SEG_AK], pc[SEG_AV])],
                        a_lambda[l], a_subln_g[l], a_qk_g[l], lam_init, heads_per_step=1)
        yb = _pool(px[SEG_B], w_pool, b_pool_s[l])
        yc = _nbr_attn(px[SEG_CQ], px[SEG_CK], px[SEG_CV], pc[SEG_CK], pc[SEG_CV], bias_t, pairs_per_step=1)
        x = _merge(x, mx[0], mx[1], mx[2], n1, (ya, yb, yc, px[SEG_YD]), w_gate, w_br, w_o)
        x = _ffn(x, mx[3], mx[4], mx[5], n2, w_gu_l, w_dn)

        if not last:
            ya_c = _diff_attn(pc[SEG_AQ], [(pc[SEG_AK], pc[SEG_AV])], a_lambda[l], a_subln_g[l], a_qk_g[l], lam_init,
                              heads_per_step=A_HEADS)
            yb_c = _pool(pc[SEG_B], w_pool, b_pool_s[l])
            yc_c = _nbr_attn(pc[SEG_CQ], None, None, pc[SEG_CK], pc[SEG_CV], None,
                             pairs_per_step=SEG_W // V7X_LANES)
            ctx = _merge(ctx, mc[0], mc[1], mc[2], n1, (ya_c, yb_c, yc_c, pc[SEG_YD]), w_gate, w_br, w_o)
            ctx = _ffn(ctx, mc[3], mc[4], mc[5], n2, w_gu_l, w_dn)
    return x
```

```python
import functools
import math

import jax
import jax.numpy as jnp
import numpy as np
from jax import lax
from jax.experimental import pallas as pl
from jax.experimental.pallas import tpu as pltpu

F32 = jnp.float32
BF16 = jnp.bfloat16

D_MODEL = 1024
GRID_W = 64
EPS = 1e-6
ROPE_THETA = 10000.0
A_HEADS = 4
A_DH = 64
POOL_WINDOWS = (2, 4, 8, 16)
C_DH = 64
NA_ROWS = 8
NA_COLS = 16
CHUNK = 128
N_BRANCH = 4
SEG_W = 512
N_SEG = 9
MIX_COLS = N_SEG * SEG_W
FFN_HIDDEN = 2816
SEG_AQ, SEG_AK, SEG_AV, SEG_B, SEG_CQ, SEG_CK, SEG_CV, SEG_DU, SEG_DV = range(N_SEG)
SEG_YD = N_SEG
SEG_DTYPES = (BF16, BF16, BF16, F32, BF16, BF16, BF16, F32, BF16, BF16)
MIXER_SEGS = (SEG_AQ, SEG_AK, SEG_AV, SEG_B, SEG_CQ, SEG_CK, SEG_CV, SEG_YD)
CTX_KV_SEGS = (SEG_AK, SEG_AV, SEG_CK, SEG_CV)

V7X_LANES = 128
V7X_VMEM_BYTES = 64 * 1024 * 1024
V7X_VMEM_LIMIT = V7X_VMEM_BYTES - 8 * 1024 * 1024

ROW_TILE = 1024
ATTN_Q_TILE = 256
ATTN_KEY_CHUNK = 256
SCORE_BOUND_SLACK = 1.02
DENOM_FLOOR = 1e-18
POOL_TILE = 256
POOL_HALO = 8
FFN_CHUNK = 256


def _params(n_axes, vmem_bytes):
    return pltpu.CompilerParams(
        dimension_semantics=("parallel",) * n_axes,
        vmem_limit_bytes=int(min(vmem_bytes, V7X_VMEM_LIMIT)))


def _resident(shape):
    zeros = (0,) * len(shape)
    return pl.BlockSpec(shape, lambda *_: zeros, pipeline_mode=pl.Buffered(1))


def _dot(a, b):
    return jnp.dot(a, b, preferred_element_type=F32)


def _dot_nt(a, b):
    return lax.dot_general(a, b, (((1,), (1,)), ((), ())), preferred_element_type=F32)


def _dot_tn(a, b):
    return lax.dot_general(a, b, (((0,), (0,)), ((), ())), preferred_element_type=F32)


def _split_bf16(t):
    hi = t.astype(BF16)
    lo = (t - hi.astype(F32)).astype(BF16)
    return hi, lo


def _norm_modulate(x, gain, shift, scale):
    ms = jnp.mean(x * x, axis=-1, keepdims=True)
    n = x * lax.rsqrt(ms + EPS)
    return ((n * gain) * (1.0 + scale) + shift).astype(BF16)


def _mod_kernel(c_ref, w_ref, b_ref, o_ref):
    a = c_ref[...]
    a = a * jax.nn.sigmoid(a)
    a_hi, a_lo = _split_bf16(a)
    w_hi, w_lo = _split_bf16(w_ref[...])
    o_ref[...] = _dot(a_hi, w_hi) + _dot(a_hi, w_lo) + _dot(a_lo, w_hi) + b_ref[...]


def _modulation(c_all, w_mod, b_mod):
    depth, d, n = w_mod.shape
    rows = c_all.shape[0]
    tn = 768
    return pl.pallas_call(
        _mod_kernel,
        grid=(depth, n // tn),
        in_specs=[
            pl.BlockSpec((rows, d), lambda l, j: (0, 0)),
            pl.BlockSpec((None, d, tn), lambda l, j: (l, 0, j)),
            pl.BlockSpec((None, 1, tn), lambda l, j: (l, 0, j)),
        ],
        out_specs=pl.BlockSpec((None, rows, tn), lambda l, j: (l, 0, j)),
        out_shape=jax.ShapeDtypeStruct((depth, rows, n), F32),
        compiler_params=_params(2, 32 * 1024 * 1024),
        name="modulation",
    )(c_all, w_mod, b_mod.reshape(depth, 1, n))


def _half_block_rms(t, first_half, gain):
    sq = t * t
    s_first = jnp.sum(jnp.where(first_half, sq, 0.0), axis=-1, keepdims=True)
    s_second = jnp.sum(jnp.where(first_half, 0.0, sq), axis=-1, keepdims=True)
    ms = jnp.where(first_half, s_first, s_second) * (1.0 / A_DH)
    return t * lax.rsqrt(ms + EPS) * gain


def _rope_block(t, cos, sin_up, sin_dn):
    return (t * cos + pltpu.roll(t, V7X_LANES - 16, 1) * sin_up + pltpu.roll(t, 16, 1) * sin_dn)


def _in_proj_kernel(*refs, rope, out_segs):
    x_ref, sh_ref, sc_ref, ng_ref, w_ref, qkg_ref, vng_ref, ws_ref, bs_ref = refs[:9]
    pos = 9
    if rope:
        cos_ref, sup_ref, sdn_ref = refs[9:12]
        pos = 12
    outs = dict(zip(out_segs, refs[pos:pos + len(out_segs)]))
    gated = SEG_YD in outs
    needed = set(out_segs) | ({SEG_DU, SEG_DV} if gated else set())
    hb = _norm_modulate(x_ref[...], ng_ref[...], sh_ref[...], sc_ref[...])
    qk_row = {SEG_AQ: 0, SEG_AK: 1, SEG_CQ: 2, SEG_CK: 3}
    first_half = lax.broadcasted_iota(jnp.int32, (1, V7X_LANES), 1) < A_DH
    for seg in range(N_SEG):
        if seg not in needed:
            continue
        r = _dot(hb, w_ref[:, seg * SEG_W:(seg + 1) * SEG_W])
        if seg in qk_row:
            row = qk_row[seg]
            blocks = []
            for j in range(SEG_W // V7X_LANES):
                lanes = slice(j * V7X_LANES, (j + 1) * V7X_LANES)
                t = _half_block_rms(r[:, lanes], first_half, qkg_ref[row:row + 1, lanes])
                if rope and seg in (SEG_AQ, SEG_AK):
                    t = _rope_block(t, cos_ref[...], sup_ref[...], sdn_ref[...])
                blocks.append(t)
            r = jnp.concatenate(blocks, axis=1)
        elif seg == SEG_DU and gated:
            gate_u = r
        elif seg == SEG_DV:
            ms = jnp.mean(r * r, axis=-1, keepdims=True)
            r = r * lax.rsqrt(ms + EPS) * vng_ref[...]
            if gated:
                vb = r.astype(BF16)
                n_chunks = r.shape[0] // CHUNK
                for g in range(ws_ref.shape[0]):
                    lanes = slice(g * V7X_LANES, (g + 1) * V7X_LANES)
                    v_wide = jnp.concatenate([vb[n * CHUNK:(n + 1) * CHUNK, lanes] for n in range(n_chunks)], axis=1)
                    sv_wide = _dot(ws_ref[g], v_wide)
                    for n in range(n_chunks):
                        rows = slice(n * CHUNK, (n + 1) * CHUNK)
                        sv = sv_wide[:, n * V7X_LANES:(n + 1) * V7X_LANES] + bs_ref[:, lanes]
                        outs[SEG_YD][rows, lanes] = (gate_u[rows, lanes] * sv).astype(outs[SEG_YD].dtype)
        if seg in outs:
            outs[seg][...] = r.astype(outs[seg].dtype)


def _in_proj(x, shift, scale, norm_g, w_mix, qk_gains, vn_g, w_s, b_full, rope_tabs, out_segs):
    b, l, d = x.shape
    tm = min(ROW_TILE, l)
    batched = shift.shape[0] != 1
    mod_map = (lambda bi, i: (bi, 0, 0)) if batched else (lambda bi, i: (0, 0, 0))
    rope = rope_tabs is not None
    in_specs = [
        pl.BlockSpec((None, tm, d), lambda bi, i: (bi, i, 0)),
        pl.BlockSpec((None, 1, d), mod_map),
        pl.BlockSpec((None, 1, d), mod_map),
        _resident((1, d)),
        _resident((d, MIX_COLS)),
        _resident((4, SEG_W)),
        _resident((1, SEG_W)),
        _resident(w_s.shape),
        _resident(b_full.shape),
    ]
    args = [x, shift, scale, norm_g, w_mix, qk_gains, vn_g, w_s, b_full]
    if rope:
        in_specs += [pl.BlockSpec((tm, V7X_LANES), lambda bi, i: (i, 0))] * 3
        args += list(rope_tabs)
    dtypes = [SEG_DTYPES[s] for s in out_segs]
    out_bytes = sum(tm * SEG_W * jnp.dtype(t).itemsize for t in dtypes)
    vmem = d * MIX_COLS * 2 + 2 * (tm * d * 4 + out_bytes) + 8 * tm * SEG_W * 4 + (8 << 20)
    outs = pl.pallas_call(
        functools.partial(_in_proj_kernel, rope=rope, out_segs=tuple(out_segs)),
        grid=(b, l // tm),
        in_specs=in_specs,
        out_specs=[pl.BlockSpec((None, tm, SEG_W), lambda bi, i: (bi, i, 0))] * len(out_segs),
        out_shape=[jax.ShapeDtypeStruct((b, l, SEG_W), t) for t in dtypes],
        compiler_params=_params(2, vmem),
        name="in_proj_rope" if rope else "in_proj",
    )(*args)
    return dict(zip(out_segs, outs))


def _diff_attn_kernel(*refs, n_parts, lq, tq, heads, lam_init):
    al_ref, sg_ref, qkg_ref, q_ref = refs[:4]
    kv = refs[4:4 + 2 * n_parts]
    o_ref, kcat_ref, vt_ref, s_ref, e_ref = refs[4 + 2 * n_parts:]
    al = al_ref[...]
    lam = (jnp.exp(jnp.sum(al[0:1] * al[1:2], axis=-1, keepdims=True))
           - jnp.exp(jnp.sum(al[2:3] * al[3:4], axis=-1, keepdims=True)) + lam_init)
    lane = lax.broadcasted_iota(jnp.int32, (1, V7X_LANES), 1)
    scale = A_DH ** -0.5
    masks = (jnp.where(lane < A_DH, scale, 0.0).astype(BF16),
             jnp.where(lane >= A_DH, scale, 0.0).astype(BF16))
    sub_gain = sg_ref[...] * (1.0 - lam_init)
    gains = jnp.max(jnp.abs(qkg_ref[...]), axis=-1, keepdims=True)
    bound = gains[0:1] * gains[1:2] * (A_DH * scale * SCORE_BOUND_SLACK)
    hw = 2 * A_DH
    n_keys = kcat_ref.shape[0]
    kc = math.gcd(n_keys, ATTN_KEY_CHUNK)
    chunks = range(0, n_keys, kc)
    n_tiles = lq // tq
    vt_ref[hw:, :] = jnp.ones((vt_ref.shape[0] - hw, n_keys), BF16)

    def finish(rows, cols, acc):
        o = (acc[0][:hw] * (1.0 / acc[0][hw:hw + 1]) - acc[1][:hw] * (lam / acc[1][hw:hw + 1])).T
        ms = jnp.mean(o * o, axis=-1, keepdims=True)
        o_ref[rows, cols] = (o * lax.rsqrt(ms + EPS) * sub_gain).astype(o_ref.dtype)

    def bounded_tile(cols, slot, r0):
        qc = q_ref[r0:r0 + tq, cols]
        acc = []
        for c, mask in enumerate(masks):
            qm = qc * mask
            for k0 in chunks:
                s = _dot_nt(kcat_ref[k0:k0 + kc, :], qm)
                e_ref[slot, c, k0:k0 + kc, :] = jnp.exp(s - bound).astype(BF16)
            acc.append(_dot(vt_ref[...], e_ref[slot, c]))
        finish(slice(r0, r0 + tq), cols, acc)
        return jnp.minimum(acc[0][hw:hw + 1], acc[1][hw:hw + 1])

    def exact_tile(cols, r0):
        qc = q_ref[pl.ds(r0, tq), cols]
        acc = []
        for c, mask in enumerate(masks):
            qm = qc * mask
            m8 = None
            for k0 in chunks:
                s = _dot_nt(kcat_ref[k0:k0 + kc, :], qm)
                s_ref[c, k0:k0 + kc, :] = s
                cm = jnp.max(s.reshape(kc // 8, 8, tq), axis=0)
                m8 = cm if m8 is None else jnp.maximum(m8, cm)
            m = jnp.max(m8, axis=0, keepdims=True)
            for k0 in chunks:
                rows = slice(k0, k0 + kc)
                e_ref[0, c, rows, :] = jnp.exp(s_ref[c, rows, :] - m).astype(BF16)
            acc.append(_dot(vt_ref[...], e_ref[0, c]))
        finish(pl.ds(r0, tq), cols, acc)

    def load_head(cols):
        off = 0
        for p in range(n_parts):
            n = kv[2 * p].shape[0]
            kcat_ref[off:off + n, :] = kv[2 * p][:, cols]
            vt_ref[0:hw, off:off + n] = kv[2 * p + 1][:, cols].astype(F32).T.astype(BF16)
            off += n

    for h in range(heads):
        cols = slice(h * hw, (h + 1) * hw)
        load_head(cols)
        smallest = None
        for j in range(n_tiles):
            denom = bounded_tile(cols, j % 2, j * tq)
            smallest = denom if smallest is None else jnp.minimum(smallest, denom)

        @pl.when(jnp.min(smallest) < DENOM_FLOOR)
        def _():
            def redo(j, carry):
                exact_tile(cols, pl.multiple_of(j * tq, tq))
                return carry
            lax.fori_loop(0, n_tiles, redo, 0)


def _diff_attn(q, kv_parts, a_lambda, subln_g, qk_g, lam_init, heads_per_step):
    b, lq, _ = q.shape
    tq = min(ATTN_Q_TILE, lq)
    hw = 2 * A_DH
    wdt = heads_per_step * hw
    head_spec = lambda n: pl.BlockSpec((None, n, wdt), lambda bi, h: (bi, 0, h))
    in_specs = [_resident(a_lambda.shape), _resident((1, hw)), _resident(qk_g.shape), head_spec(lq)]
    args = [a_lambda, subln_g.reshape(1, hw), qk_g, q]
    lk = 0
    for k, v in kv_parts:
        in_specs += [head_spec(k.shape[1]), head_spec(v.shape[1])]
        args += [k, v]
        lk += k.shape[1]
    ones_rows = 16
    scratch = [pltpu.VMEM((lk, hw), BF16), pltpu.VMEM((hw + ones_rows, lk), BF16),
               pltpu.VMEM((2, lk, tq), F32), pltpu.VMEM((2, 2, lk, tq), BF16)]
    vmem = (2 * (2 * lq + 4 * lk) * wdt * 2 + 2 * lk * (hw + ones_rows) * 2 + 2 * lk * tq * 8
            + 6 * ATTN_KEY_CHUNK * tq * 4 + (8 << 20))
    return pl.pallas_call(
        functools.partial(_diff_attn_kernel, n_parts=len(kv_parts), lq=lq, tq=tq,
                          heads=heads_per_step, lam_init=lam_init),
        grid=(b, A_HEADS // heads_per_step),
        in_specs=in_specs,
        out_specs=head_spec(lq),
        out_shape=jax.ShapeDtypeStruct((b, lq, A_HEADS * hw), BF16),
        scratch_shapes=scratch,
        compiler_params=_params(2, vmem),
        name="diff_attn",
    )(*args)


def _tree_sum(terms):
    while len(terms) > 1:
        terms = [terms[i] + terms[i + 1] for i in range(0, len(terms), 2)]
    return terms[0]


def _pool_kernel(p_ref, w_ref, s_ref, o_ref, pad_ref, *, l, tp):
    wdt = p_ref.shape[-1]
    pad_ref[0:POOL_HALO, :] = jnp.zeros((POOL_HALO, wdt), F32)
    pad_ref[POOL_HALO + l:, :] = jnp.zeros((POOL_HALO, wdt), F32)
    pad_ref[POOL_HALO:POOL_HALO + l, :] = p_ref[...]
    n_tiles = l // tp
    for ti in range(n_tiles):
        t0 = ti * tp
        edge = ti == 0 or ti == n_tiles - 1
        if edge:
            t_row = t0 + lax.broadcasted_iota(jnp.int32, (tp, V7X_LANES), 0)
        for g, w in enumerate(POOL_WINDOWS):
            lanes = slice(g * V7X_LANES, (g + 1) * V7X_LANES)
            shifted = [pad_ref[POOL_HALO + t0 + j:POOL_HALO + t0 + j + tp, lanes]
                       for j in range(-(w // 2), w // 2)]
            centre = shifted[w // 2]
            total = _tree_sum(shifted)
            if edge:
                cnt = (jnp.minimum(t_row + w // 2, l) - jnp.maximum(t_row - w // 2, 0)).astype(F32)
                mean = total / cnt
            else:
                mean = total * (1.0 / w)
            y = _dot((mean - centre).astype(BF16), w_ref[g]) * s_ref[:, lanes]
            o_ref[t0:t0 + tp, lanes] = y.astype(o_ref.dtype)


def _pool(p, w_pool, s_pool):
    b, l, wdt = p.shape
    tp = min(POOL_TILE, l)
    seq = pl.BlockSpec((None, l, wdt), lambda bi: (bi, 0, 0))
    return pl.pallas_call(
        functools.partial(_pool_kernel, l=l, tp=tp),
        grid=(b,),
        in_specs=[seq, _resident(w_pool.shape), _resident((1, wdt))],
        out_specs=seq,
        out_shape=jax.ShapeDtypeStruct((b, l, wdt), BF16),
        scratch_shapes=[pltpu.VMEM((l + 2 * POOL_HALO, wdt), F32)],
        compiler_params=_params(1, 5 * l * wdt * 4 + (16 << 20)),
        name="pool",
    )(p, w_pool, s_pool.reshape(1, wdt))


def _nbr_attn_kernel(*refs, n_rows, pairs, local):
    if local:
        qkg_ref, q_ref, k_ref, v_ref, kc_ref, vc_ref, bias_ref, o_ref = refs
    else:
        qkg_ref, q_ref, kc_ref, vc_ref, o_ref = refs
    gains = jnp.max(jnp.abs(qkg_ref[...]), axis=-1, keepdims=True)
    bound = gains[0:1] * gains[1:2] * (C_DH ** 0.5 * SCORE_BOUND_SLACK)
    if local:
        bmax = jnp.max(jnp.max(jnp.max(bias_ref[...], axis=0), axis=0, keepdims=True), axis=1, keepdims=True)
        bound = bound + jnp.maximum(bmax, 0.0)
    lane = lax.broadcasted_iota(jnp.int32, (1, V7X_LANES), 1)
    scale = C_DH ** -0.5
    m_first = jnp.where(lane < C_DH, scale, 0.0).astype(BF16)
    m_second = jnp.where(lane >= C_DH, scale, 0.0).astype(BF16)
    lane_out = lax.broadcasted_iota(jnp.int32, (GRID_W, V7X_LANES), 1)
    n_loc = NA_ROWS * GRID_W
    hq = 2 * GRID_W
    with_ones = lambda v: jnp.concatenate([v, jnp.ones_like(v)], axis=1)
    first_key_row = [min(max(r - NA_ROWS // 2, 0), n_rows - NA_ROWS) for r in range(n_rows)]

    def attend(cols, shift):
        blocks = []
        for r in range(n_rows):
            qr = q_ref[r * GRID_W:(r + 1) * GRID_W, cols]
            blocks += [qr * m_first, qr * m_second]
        qbd = jnp.concatenate(blocks, axis=0)
        s_ctx = _dot_nt(qbd, kc_ref[:, cols])
        if local:
            s_loc = jnp.concatenate(
                [_dot_nt(qbd[r * hq:(r + 1) * hq], k_ref[rs * GRID_W:rs * GRID_W + n_loc, cols])
                 + bias_ref[rs - r + NA_ROWS - 1] for r, rs in enumerate(first_key_row)], axis=0)
        if shift is None:
            shift = jnp.max(s_ctx, axis=-1, keepdims=True)
            if local:
                shift = jnp.maximum(shift, jnp.max(s_loc, axis=-1, keepdims=True))
        e_ctx = jnp.exp(s_ctx - shift)
        full = _dot(e_ctx.astype(BF16), with_ones(vc_ref[:, cols]))
        if local:
            e_loc = jnp.exp(s_loc - shift).astype(BF16)
            full = full + jnp.concatenate(
                [_dot(e_loc[r * hq:(r + 1) * hq], with_ones(v_ref[rs * GRID_W:rs * GRID_W + n_loc, cols]))
                 for r, rs in enumerate(first_key_row)], axis=0)
        denom = full[:, V7X_LANES:V7X_LANES + 1]
        full = full[:, :V7X_LANES] * (1.0 / denom)
        out = jnp.concatenate(
            [jnp.where(lane_out < C_DH, full[r * hq:r * hq + GRID_W], full[r * hq + GRID_W:(r + 1) * hq])
             for r in range(n_rows)], axis=0)
        o_ref[:, cols] = out.astype(o_ref.dtype)
        return jnp.min(denom, axis=0, keepdims=True)

    pair_cols = [slice(p * V7X_LANES, (p + 1) * V7X_LANES) for p in range(pairs)]
    smallest = functools.reduce(jnp.minimum, [attend(cols, bound) for cols in pair_cols])

    @pl.when(jnp.min(smallest) < DENOM_FLOOR)
    def _():
        for cols in pair_cols:
            attend(cols, None)


def _nbr_attn(q, k, v, kc, vc, bias_t, qk_g, pairs_per_step):
    b, lq, wdt = q.shape
    lc = kc.shape[1]
    n_rows = lq // GRID_W
    local = k is not None
    blk = pairs_per_step * V7X_LANES
    pair = lambda n: pl.BlockSpec((None, n, blk), lambda h, bi: (bi, 0, h))
    if local:
        assert pairs_per_step == 1
        in_specs = [_resident(qk_g.shape), pair(lq), pair(lq), pair(lq), pair(lc), pair(lc),
                    pl.BlockSpec((None,) + bias_t.shape[1:], lambda h, bi: (h, 0, 0, 0))]
        args = [qk_g, q, k, v, kc, vc, bias_t]
    else:
        in_specs = [_resident(qk_g.shape), pair(lq), pair(lc), pair(lc)]
        args = [qk_g, q, kc, vc]
    return pl.pallas_call(
        functools.partial(_nbr_attn_kernel, n_rows=n_rows, pairs=pairs_per_step, local=local),
        grid=(wdt // blk, b),
        in_specs=in_specs,
        out_specs=pair(lq),
        out_shape=jax.ShapeDtypeStruct((b, lq, wdt), BF16),
        compiler_params=_params(2, V7X_VMEM_LIMIT),
        name="nbr_attn" if local else "ctx_attn",
    )(*args)


def _nbr_bias_table(rpb):
    qc = np.arange(GRID_W)[:, None]
    kc = np.arange(GRID_W)[None, :]
    win0 = np.clip(qc - NA_COLS // 2, 0, GRID_W - NA_COLS)
    valid = (kc >= win0) & (kc < win0 + NA_COLS)
    dc = np.clip(kc - qc + NA_COLS - 1, 0, 2 * NA_COLS - 2)
    t = jnp.where(valid[None, None], rpb[:, :, dc], -jnp.inf)
    h = t.shape[0]
    dr = np.arange(NA_ROWS)[:, None] + np.arange(NA_ROWS)[None, :]
    t = t[:, dr]
    t = t.reshape(h // 2, 2, NA_ROWS, NA_ROWS, GRID_W, GRID_W).transpose(0, 2, 1, 4, 3, 5)
    return t.reshape(h // 2, NA_ROWS, 2 * GRID_W, NA_ROWS * GRID_W)


def _merge_kernel(x_ref, sh_ref, sc_ref, g1_ref, ng_ref, ya_ref, yb_ref, yc_ref, yd_ref,
                  wg_ref, wb_ref, wo_ref, o_ref):
    x = x_ref[...]
    d = x.shape[-1]
    hb = _norm_modulate(x, ng_ref[...], sh_ref[...], sc_ref[...])
    acc = None
    for i, y_ref in enumerate((ya_ref, yb_ref, yc_ref, yd_ref)):
        gate = _dot(hb, wg_ref[:, i * d:(i + 1) * d])
        term = jax.nn.sigmoid(gate) * _dot(y_ref[...], wb_ref[i * SEG_W:(i + 1) * SEG_W, :])
        acc = term if acc is None else acc + term
    o_ref[...] = x + g1_ref[...] * _dot(acc.astype(BF16), wo_ref[...])


def _merge(x, shift, scale, g1, norm_g, ys, w_gate, w_branch, w_out):
    b, l, d = x.shape
    tm = min(ROW_TILE, l)
    batched = shift.shape[0] != 1
    mod_map = (lambda bi, i: (bi, 0, 0)) if batched else (lambda bi, i: (0, 0, 0))
    x_tile = pl.BlockSpec((None, tm, d), lambda bi, i: (bi, i, 0))
    y_tile = pl.BlockSpec((None, tm, SEG_W), lambda bi, i: (bi, i, 0))
    mod = pl.BlockSpec((None, 1, d), mod_map)
    weights = (w_gate.size + w_branch.size + w_out.size) * 2
    vmem = weights + 2 * (2 * tm * d * 4 + 4 * tm * SEG_W * 2) + 8 * tm * d * 4 + (8 << 20)
    return pl.pallas_call(
        _merge_kernel,
        grid=(b, l // tm),
        in_specs=[x_tile, mod, mod, mod, _resident((1, d)), y_tile, y_tile, y_tile, y_tile,
                  _resident(w_gate.shape), _resident(w_branch.shape), _resident(w_out.shape)],
        out_specs=x_tile,
        out_shape=jax.ShapeDtypeStruct(x.shape, F32),
        compiler_params=_params(2, vmem),
        name="merge",
    )(x, shift, scale, g1, norm_g, *ys, w_gate, w_branch, w_out)


def _ffn_kernel(x_ref, sh_ref, sc_ref, g2_ref, ng_ref, wgu_ref, wd_ref, o_ref, u_ref):
    x = x_ref[...]
    hb = _norm_modulate(x, ng_ref[...], sh_ref[...], sc_ref[...])
    hidden = wd_ref.shape[0]
    for c in range(hidden // FFN_CHUNK):
        cols = slice(c * FFN_CHUNK, (c + 1) * FFN_CHUNK)
        a = _dot(hb, wgu_ref[:, cols])
        bb = _dot(hb, wgu_ref[:, hidden + c * FFN_CHUNK:hidden + (c + 1) * FFN_CHUNK])
        u_ref[:, cols] = (a * jax.nn.sigmoid(a) * bb).astype(BF16)
    o_ref[...] = x + g2_ref[...] * _dot(u_ref[...], wd_ref[...])


def _ffn(x, shift, scale, g2, norm_g, w_gu, w_down):
    b, l, d = x.shape
    tm = min(ROW_TILE, l)
    hidden = w_down.shape[0]
    batched = shift.shape[0] != 1
    mod_map = (lambda bi, i: (bi, 0, 0)) if batched else (lambda bi, i: (0, 0, 0))
    x_tile = pl.BlockSpec((None, tm, d), lambda bi, i: (bi, i, 0))
    mod = pl.BlockSpec((None, 1, d), mod_map)
    vmem = (w_gu.size + w_down.size) * 2 + 4 * tm * d * 4 + tm * hidden * 2 + 8 * tm * d * 4 + (8 << 20)
    return pl.pallas_call(
        _ffn_kernel,
        grid=(b, l // tm),
        in_specs=[x_tile, mod, mod, mod, _resident((1, d)),
                  _resident(w_gu.shape), _resident(w_down.shape)],
        out_specs=x_tile,
        out_shape=jax.ShapeDtypeStruct(x.shape, F32),
        scratch_shapes=[pltpu.VMEM((tm, hidden), BF16)],
        compiler_params=_params(2, vmem),
        name="ffn",
    )(x, shift, scale, g2, norm_g, w_gu, w_down)


def _rope_tables(n_tok):
    nf = A_DH // 4
    t = np.arange(n_tok)
    row = (t // GRID_W).astype(np.float32)
    col = (t % GRID_W).astype(np.float32)
    inv = np.float32(ROPE_THETA) ** (-np.arange(nf, dtype=np.float32) / np.float32(nf))
    ar = row[:, None] * inv
    ac = col[:, None] * inv
    ang = np.concatenate([ar, ar, ac, ac], axis=-1).astype(np.float64)
    cos, sin = np.cos(ang), np.sin(ang)
    quarter = (np.arange(A_DH) // nf) % 2
    sin_up = np.where(quarter == 0, -sin, 0.0)
    sin_dn = np.where(quarter == 1, sin, 0.0)
    rep = V7X_LANES // A_DH
    return tuple(jnp.asarray(np.tile(a, (1, rep)), dtype=F32) for a in (cos, sin_up, sin_dn))


def kernel(x, c, ctx, c_ctx, w_mod, b_mod, norm1_g, w_in, a_qk_g, a_lambda, a_subln_g, b_pool_w,
           b_pool_s, c_qk_g, c_rpb, d_vn_g, d_ws, d_bs, w_branch, w_out, norm2_g, w_gu, w_down):
    b, s, d = x.shape
    depth = w_mod.shape[0]
    rope_tabs = _rope_tables(s)

    rows = -(-(b + 1) // 8) * 8
    c_all = jnp.zeros((rows, d), F32).at[:b].set(c).at[b].set(c_ctx)
    mod = _modulation(c_all, w_mod, b_mod)

    for l in range(depth):
        last = l == depth - 1
        lam_init = 0.8 - 0.6 * math.exp(-0.3 * l)
        mx = [mod[l, :b, k * d:(k + 1) * d].reshape(b, 1, d) for k in range(6)]
        mc = [mod[l, b:b + 1, k * d:(k + 1) * d].reshape(1, 1, d) for k in range(6)]
        w_mix = w_in[l][:, :MIX_COLS].astype(BF16)
        w_gate = w_in[l][:, MIX_COLS:].astype(BF16)
        w_br = w_branch[l].astype(BF16)
        w_o = w_out[l].astype(BF16)
        w_gu_l = w_gu[l].astype(BF16)
        w_dn = w_down[l].astype(BF16)
        w_pool = b_pool_w[l].astype(BF16)
        w_s = d_ws[l].astype(BF16)
        n1 = norm1_g[l].reshape(1, d)
        n2 = norm2_g[l].reshape(1, d)
        rep = SEG_W // A_DH
        qk_gains = jnp.stack([jnp.tile(a_qk_g[l, 0], rep), jnp.tile(a_qk_g[l, 1], rep),
                              jnp.tile(c_qk_g[l, 0], rep), jnp.tile(c_qk_g[l, 1], rep)])
        vn_g = d_vn_g[l].reshape(1, SEG_W)
        bs_full = jnp.repeat(d_bs[l].T, V7X_LANES, axis=1)
        bias_t = _nbr_bias_table(c_rpb[l])

        px = _in_proj(x, mx[0], mx[1], n1, w_mix, qk_gains, vn_g, w_s, bs_full, rope_tabs, MIXER_SEGS)
        pc = _in_proj(ctx, mc[0], mc[1], n1, w_mix, qk_gains, vn_g, w_s, bs_full, None,
                      CTX_KV_SEGS if last else MIXER_SEGS)

        ya = _diff_attn(px[SEG_AQ], [(px[SEG_AK], px[SEG_AV]), (pc[SEG_AK], pc[SEG_AV])],
                        a_lambda[l], a_subln_g[l], a_qk_g[l], lam_init, heads_per_step=1)
        yb = _pool(px[SEG_B], w_pool, b_pool_s[l])
        yc = _nbr_attn(px[SEG_CQ], px[SEG_CK], px[SEG_CV], pc[SEG_CK], pc[SEG_CV], bias_t, c_qk_g[l],
                       pairs_per_step=1)
        x = _merge(x, mx[0], mx[1], mx[2], n1, (ya, yb, yc, px[SEG_YD]), w_gate, w_br, w_o)
        x = _ffn(x, mx[3], mx[4], mx[5], n2, w_gu_l, w_dn)

        if not last:
            ya_c = _diff_attn(pc[SEG_AQ], [(pc[SEG_AK], pc[SEG_AV])], a_lambda[l], a_subln_g[l], a_qk_g[l], lam_init,
                              heads_per_step=A_HEADS)
            yb_c = _pool(pc[SEG_B], w_pool, b_pool_s[l])
            yc_c = _nbr_attn(pc[SEG_CQ], None, None, pc[SEG_CK], pc[SEG_CV], None, c_qk_g[l],
                             pairs_per_step=SEG_W // V7X_LANES)
            ctx = _merge(ctx, mc[0], mc[1], mc[2], n1, (ya_c, yb_c, yc_c, pc[SEG_YD]), w_gate, w_br, w_o)
            ctx = _ffn(ctx, mc[3], mc[4], mc[5], n2, w_gu_l, w_dn)
    return x
```

```python
import functools
import math

import jax
import jax.numpy as jnp
import numpy as np
from jax import lax
from jax.experimental import pallas as pl
from jax.experimental.pallas import tpu as pltpu

F32 = jnp.float32
BF16 = jnp.bfloat16

D_MODEL = 1024
GRID_W = 64
EPS = 1e-6
ROPE_THETA = 10000.0
A_HEADS = 4
A_DH = 64
POOL_WINDOWS = (2, 4, 8, 16)
C_DH = 64
NA_ROWS = 8
NA_COLS = 16
CHUNK = 128
N_BRANCH = 4
SEG_W = 512
N_SEG = 9
MIX_COLS = N_SEG * SEG_W
FFN_HIDDEN = 2816
SEG_AQ, SEG_AK, SEG_AV, SEG_B, SEG_CQ, SEG_CK, SEG_CV, SEG_DU, SEG_DV = range(N_SEG)
SEG_YD = N_SEG
SEG_DTYPES = (BF16, BF16, BF16, F32, BF16, BF16, BF16, F32, BF16, BF16)
MIXER_SEGS = (SEG_AQ, SEG_AK, SEG_AV, SEG_B, SEG_CQ, SEG_CK, SEG_CV, SEG_YD)
CTX_KV_SEGS = (SEG_AK, SEG_AV, SEG_CK, SEG_CV)

V7X_LANES = 128
V7X_VMEM_BYTES = 64 * 1024 * 1024
V7X_VMEM_LIMIT = V7X_VMEM_BYTES - 8 * 1024 * 1024

ROW_TILE = 1024
ATTN_Q_TILE = 256
ATTN_KEY_CHUNK = 256
SCORE_BOUND_SLACK = 1.02
DENOM_FLOOR = 1e-18
POOL_TILE = 256
POOL_HALO = 8
FFN_CHUNK = 256


def _params(n_axes, vmem_bytes):
    return pltpu.CompilerParams(
        dimension_semantics=("parallel",) * n_axes,
        vmem_limit_bytes=int(min(vmem_bytes, V7X_VMEM_LIMIT)))


def _resident(shape, layer=None):
    zeros = (0,) * len(shape)
    if layer is None:
        return pl.BlockSpec(shape, lambda *_: zeros, pipeline_mode=pl.Buffered(1))
    return pl.BlockSpec((None,) + tuple(shape), lambda *_: (layer,) + zeros, pipeline_mode=pl.Buffered(1))


def _dot(a, b):
    return jnp.dot(a, b, preferred_element_type=F32)


def _dot_nt(a, b):
    return lax.dot_general(a, b, (((1,), (1,)), ((), ())), preferred_element_type=F32)


def _dot_tn(a, b):
    return lax.dot_general(a, b, (((0,), (0,)), ((), ())), preferred_element_type=F32)


def _split_bf16(t):
    hi = t.astype(BF16)
    lo = (t - hi.astype(F32)).astype(BF16)
    return hi, lo


def _norm_modulate(x, gain, shift, scale):
    ms = jnp.mean(x * x, axis=-1, keepdims=True)
    n = x * lax.rsqrt(ms + EPS)
    return ((n * gain) * (1.0 + scale) + shift).astype(BF16)


def _mod_kernel(c_ref, w_ref, b_ref, o_ref):
    a = c_ref[...]
    a = a * jax.nn.sigmoid(a)
    a_hi, a_lo = _split_bf16(a)
    w_hi, w_lo = _split_bf16(w_ref[...])
    o_ref[...] = _dot(a_hi, w_hi) + _dot(a_hi, w_lo) + _dot(a_lo, w_hi) + b_ref[...]


def _modulation(c_all, w_mod, b_mod):
    depth, d, n = w_mod.shape
    rows = c_all.shape[0]
    tn = 768
    return pl.pallas_call(
        _mod_kernel,
        grid=(depth, n // tn),
        in_specs=[
            pl.BlockSpec((rows, d), lambda l, j: (0, 0)),
            pl.BlockSpec((None, d, tn), lambda l, j: (l, 0, j)),
            pl.BlockSpec((None, 1, tn), lambda l, j: (l, 0, j)),
        ],
        out_specs=pl.BlockSpec((None, rows, tn), lambda l, j: (l, 0, j)),
        out_shape=jax.ShapeDtypeStruct((depth, rows, n), F32),
        compiler_params=_params(2, 32 * 1024 * 1024),
        name="modulation",
    )(c_all, w_mod, b_mod.reshape(depth, 1, n))


def _half_block_rms(t, first_half, gain):
    sq = t * t
    s_first = jnp.sum(jnp.where(first_half, sq, 0.0), axis=-1, keepdims=True)
    s_second = jnp.sum(jnp.where(first_half, 0.0, sq), axis=-1, keepdims=True)
    ms = jnp.where(first_half, s_first, s_second) * (1.0 / A_DH)
    return t * lax.rsqrt(ms + EPS) * gain


def _rope_block(t, cos, sin_up, sin_dn):
    return (t * cos + pltpu.roll(t, V7X_LANES - 16, 1) * sin_up + pltpu.roll(t, 16, 1) * sin_dn)


def _in_proj_kernel(*refs, rope, out_segs):
    x_ref, sh_ref, sc_ref, ng_ref, w_ref, qkg_ref, vng_ref, ws_ref, bs_ref = refs[:9]
    pos = 9
    if rope:
        cos_ref, sup_ref, sdn_ref = refs[9:12]
        pos = 12
    outs = dict(zip(out_segs, refs[pos:pos + len(out_segs)]))
    gated = SEG_YD in outs
    needed = set(out_segs) | ({SEG_DU, SEG_DV} if gated else set())
    hb = _norm_modulate(x_ref[...], ng_ref[...], sh_ref[...], sc_ref[...])
    qk_row = {SEG_AQ: 0, SEG_AK: 1, SEG_CQ: 2, SEG_CK: 3}
    first_half = lax.broadcasted_iota(jnp.int32, (1, V7X_LANES), 1) < A_DH
    for seg in range(N_SEG):
        if seg not in needed:
            continue
        r = _dot(hb, w_ref[:, seg * SEG_W:(seg + 1) * SEG_W])
        if seg in qk_row:
            row = qk_row[seg]
            blocks = []
            for j in range(SEG_W // V7X_LANES):
                lanes = slice(j * V7X_LANES, (j + 1) * V7X_LANES)
                t = _half_block_rms(r[:, lanes], first_half, qkg_ref[row:row + 1, lanes])
                if rope and seg in (SEG_AQ, SEG_AK):
                    t = _rope_block(t, cos_ref[...], sup_ref[...], sdn_ref[...])
                blocks.append(t)
            r = jnp.concatenate(blocks, axis=1)
        elif seg == SEG_DU and gated:
            gate_u = r
        elif seg == SEG_DV:
            ms = jnp.mean(r * r, axis=-1, keepdims=True)
            r = r * lax.rsqrt(ms + EPS) * vng_ref[...]
            if gated:
                vb = r.astype(BF16)
                n_chunks = r.shape[0] // CHUNK
                for g in range(ws_ref.shape[0]):
                    lanes = slice(g * V7X_LANES, (g + 1) * V7X_LANES)
                    v_wide = jnp.concatenate([vb[n * CHUNK:(n + 1) * CHUNK, lanes] for n in range(n_chunks)], axis=1)
                    sv_wide = _dot(ws_ref[g], v_wide)
                    for n in range(n_chunks):
                        rows = slice(n * CHUNK, (n + 1) * CHUNK)
                        sv = sv_wide[:, n * V7X_LANES:(n + 1) * V7X_LANES] + bs_ref[:, lanes]
                        outs[SEG_YD][rows, lanes] = (gate_u[rows, lanes] * sv).astype(outs[SEG_YD].dtype)
        if seg in outs:
            outs[seg][...] = r.astype(outs[seg].dtype)


def _in_proj(x, shift, scale, norm_g, w_mix, qk_gains, vn_g, w_s, b_full, rope_tabs, out_segs, layer):
    b, l, d = x.shape
    tm = min(ROW_TILE, l)
    batched = shift.shape[0] != 1
    mod_map = (lambda bi, i: (bi, 0, 0)) if batched else (lambda bi, i: (0, 0, 0))
    rope = rope_tabs is not None
    in_specs = [
        pl.BlockSpec((None, tm, d), lambda bi, i: (bi, i, 0)),
        pl.BlockSpec((None, 1, d), mod_map),
        pl.BlockSpec((None, 1, d), mod_map),
        _resident((1, d)),
        _resident((d, MIX_COLS), layer),
        _resident((4, SEG_W)),
        _resident((1, SEG_W)),
        _resident(w_s.shape[1:], layer),
        _resident(b_full.shape),
    ]
    args = [x, shift, scale, norm_g, w_mix, qk_gains, vn_g, w_s, b_full]
    if rope:
        in_specs += [pl.BlockSpec((tm, V7X_LANES), lambda bi, i: (i, 0))] * 3
        args += list(rope_tabs)
    dtypes = [SEG_DTYPES[s] for s in out_segs]
    out_bytes = sum(tm * SEG_W * jnp.dtype(t).itemsize for t in dtypes)
    vmem = d * MIX_COLS * 2 + 2 * (tm * d * 4 + out_bytes) + 8 * tm * SEG_W * 4 + (8 << 20)
    outs = pl.pallas_call(
        functools.partial(_in_proj_kernel, rope=rope, out_segs=tuple(out_segs)),
        grid=(b, l // tm),
        in_specs=in_specs,
        out_specs=[pl.BlockSpec((None, tm, SEG_W), lambda bi, i: (bi, i, 0))] * len(out_segs),
        out_shape=[jax.ShapeDtypeStruct((b, l, SEG_W), t) for t in dtypes],
        compiler_params=_params(2, vmem),
        name="in_proj_rope" if rope else "in_proj",
    )(*args)
    return dict(zip(out_segs, outs))


def _diff_attn_kernel(*refs, n_parts, lq, tq, heads, lam_init):
    al_ref, sg_ref, qkg_ref, q_ref = refs[:4]
    kv = refs[4:4 + 2 * n_parts]
    o_ref, kcat_ref, vt_ref, s_ref, e_ref = refs[4 + 2 * n_parts:]
    al = al_ref[...]
    lam = (jnp.exp(jnp.sum(al[0:1] * al[1:2], axis=-1, keepdims=True))
           - jnp.exp(jnp.sum(al[2:3] * al[3:4], axis=-1, keepdims=True)) + lam_init)
    lane = lax.broadcasted_iota(jnp.int32, (1, V7X_LANES), 1)
    scale = A_DH ** -0.5
    masks = (jnp.where(lane < A_DH, scale, 0.0).astype(BF16),
             jnp.where(lane >= A_DH, scale, 0.0).astype(BF16))
    sub_gain = sg_ref[...] * (1.0 - lam_init)
    gains = jnp.max(jnp.abs(qkg_ref[...]), axis=-1, keepdims=True)
    bound = gains[0:1] * gains[1:2] * (A_DH * scale * SCORE_BOUND_SLACK)
    hw = 2 * A_DH
    n_keys = kcat_ref.shape[0]
    kc = math.gcd(n_keys, ATTN_KEY_CHUNK)
    chunks = range(0, n_keys, kc)
    n_tiles = lq // tq
    vt_ref[hw:, :] = jnp.ones((vt_ref.shape[0] - hw, n_keys), BF16)

    def finish(rows, cols, acc):
        o = (acc[0][:hw] * (1.0 / acc[0][hw:hw + 1]) - acc[1][:hw] * (lam / acc[1][hw:hw + 1])).T
        ms = jnp.mean(o * o, axis=-1, keepdims=True)
        o_ref[rows, cols] = (o * lax.rsqrt(ms + EPS) * sub_gain).astype(o_ref.dtype)

    def bounded_tile(cols, slot, r0):
        qc = q_ref[r0:r0 + tq, cols]
        acc = []
        for c, mask in enumerate(masks):
            qm = qc * mask
            for k0 in chunks:
                s = _dot_nt(kcat_ref[k0:k0 + kc, :], qm)
                e_ref[slot, c, k0:k0 + kc, :] = jnp.exp(s - bound).astype(BF16)
            acc.append(_dot(vt_ref[...], e_ref[slot, c]))
        finish(slice(r0, r0 + tq), cols, acc)
        return jnp.minimum(acc[0][hw:hw + 1], acc[1][hw:hw + 1])

    def exact_tile(cols, r0):
        qc = q_ref[pl.ds(r0, tq), cols]
        acc = []
        for c, mask in enumerate(masks):
            qm = qc * mask
            m8 = None
            for k0 in chunks:
                s = _dot_nt(kcat_ref[k0:k0 + kc, :], qm)
                s_ref[c, k0:k0 + kc, :] = s
                cm = jnp.max(s.reshape(kc // 8, 8, tq), axis=0)
                m8 = cm if m8 is None else jnp.maximum(m8, cm)
            m = jnp.max(m8, axis=0, keepdims=True)
            for k0 in chunks:
                rows = slice(k0, k0 + kc)
                e_ref[0, c, rows, :] = jnp.exp(s_ref[c, rows, :] - m).astype(BF16)
            acc.append(_dot(vt_ref[...], e_ref[0, c]))
        finish(pl.ds(r0, tq), cols, acc)

    def load_head(cols):
        off = 0
        for p in range(n_parts):
            n = kv[2 * p].shape[0]
            kcat_ref[off:off + n, :] = kv[2 * p][:, cols]
            vt_ref[0:hw, off:off + n] = kv[2 * p + 1][:, cols].astype(F32).T.astype(BF16)
            off += n

    for h in range(heads):
        cols = slice(h * hw, (h + 1) * hw)
        load_head(cols)
        smallest = None
        for j in range(n_tiles):
            denom = bounded_tile(cols, j % 2, j * tq)
            smallest = denom if smallest is None else jnp.minimum(smallest, denom)

        @pl.when(jnp.min(smallest) < DENOM_FLOOR)
        def _():
            def redo(j, carry):
                exact_tile(cols, pl.multiple_of(j * tq, tq))
                return carry
            lax.fori_loop(0, n_tiles, redo, 0)


def _diff_attn(q, kv_parts, a_lambda, subln_g, qk_g, lam_init, heads_per_step):
    b, lq, _ = q.shape
    tq = min(ATTN_Q_TILE, lq)
    hw = 2 * A_DH
    wdt = heads_per_step * hw
    head_spec = lambda n: pl.BlockSpec((None, n, wdt), lambda bi, h: (bi, 0, h))
    in_specs = [_resident(a_lambda.shape), _resident((1, hw)), _resident(qk_g.shape), head_spec(lq)]
    args = [a_lambda, subln_g.reshape(1, hw), qk_g, q]
    lk = 0
    for k, v in kv_parts:
        in_specs += [head_spec(k.shape[1]), head_spec(v.shape[1])]
        args += [k, v]
        lk += k.shape[1]
    ones_rows = 16
    scratch = [pltpu.VMEM((lk, hw), BF16), pltpu.VMEM((hw + ones_rows, lk), BF16),
               pltpu.VMEM((2, lk, tq), F32), pltpu.VMEM((2, 2, lk, tq), BF16)]
    vmem = (2 * (2 * lq + 4 * lk) * wdt * 2 + 2 * lk * (hw + ones_rows) * 2 + 2 * lk * tq * 8
            + 6 * ATTN_KEY_CHUNK * tq * 4 + (8 << 20))
    return pl.pallas_call(
        functools.partial(_diff_attn_kernel, n_parts=len(kv_parts), lq=lq, tq=tq,
                          heads=heads_per_step, lam_init=lam_init),
        grid=(b, A_HEADS // heads_per_step),
        in_specs=in_specs,
        out_specs=head_spec(lq),
        out_shape=jax.ShapeDtypeStruct((b, lq, A_HEADS * hw), BF16),
        scratch_shapes=scratch,
        compiler_params=_params(2, vmem),
        name="diff_attn",
    )(*args)


def _tree_sum(terms):
    while len(terms) > 1:
        terms = [terms[i] + terms[i + 1] for i in range(0, len(terms), 2)]
    return terms[0]


def _pool_kernel(p_ref, w_ref, s_ref, o_ref, pad_ref, *, l, tp):
    wdt = p_ref.shape[-1]
    pad_ref[0:POOL_HALO, :] = jnp.zeros((POOL_HALO, wdt), F32)
    pad_ref[POOL_HALO + l:, :] = jnp.zeros((POOL_HALO, wdt), F32)
    pad_ref[POOL_HALO:POOL_HALO + l, :] = p_ref[...]
    n_tiles = l // tp
    for ti in range(n_tiles):
        t0 = ti * tp
        edge = ti == 0 or ti == n_tiles - 1
        if edge:
            t_row = t0 + lax.broadcasted_iota(jnp.int32, (tp, V7X_LANES), 0)
        for g, w in enumerate(POOL_WINDOWS):
            lanes = slice(g * V7X_LANES, (g + 1) * V7X_LANES)
            shifted = [pad_ref[POOL_HALO + t0 + j:POOL_HALO + t0 + j + tp, lanes]
                       for j in range(-(w // 2), w // 2)]
            centre = shifted[w // 2]
            total = _tree_sum(shifted)
            if edge:
                cnt = (jnp.minimum(t_row + w // 2, l) - jnp.maximum(t_row - w // 2, 0)).astype(F32)
                mean = total / cnt
            else:
                mean = total * (1.0 / w)
            y = _dot((mean - centre).astype(BF16), w_ref[g]) * s_ref[:, lanes]
            o_ref[t0:t0 + tp, lanes] = y.astype(o_ref.dtype)


def _pool(p, w_pool, s_pool, layer):
    b, l, wdt = p.shape
    tp = min(POOL_TILE, l)
    seq = pl.BlockSpec((None, l, wdt), lambda bi: (bi, 0, 0))
    return pl.pallas_call(
        functools.partial(_pool_kernel, l=l, tp=tp),
        grid=(b,),
        in_specs=[seq, _resident(w_pool.shape[1:], layer), _resident((1, wdt))],
        out_specs=seq,
        out_shape=jax.ShapeDtypeStruct((b, l, wdt), BF16),
        scratch_shapes=[pltpu.VMEM((l + 2 * POOL_HALO, wdt), F32)],
        compiler_params=_params(1, 5 * l * wdt * 4 + (16 << 20)),
        name="pool",
    )(p, w_pool, s_pool.reshape(1, wdt))


def _nbr_attn_kernel(*refs, n_rows, pairs, local):
    if local:
        qkg_ref, q_ref, k_ref, v_ref, kc_ref, vc_ref, bias_ref, o_ref = refs
    else:
        qkg_ref, q_ref, kc_ref, vc_ref, o_ref = refs
    gains = jnp.max(jnp.abs(qkg_ref[...]), axis=-1, keepdims=True)
    bound = gains[0:1] * gains[1:2] * (C_DH ** 0.5 * SCORE_BOUND_SLACK)
    if local:
        bmax = jnp.max(jnp.max(jnp.max(bias_ref[...], axis=0), axis=0, keepdims=True), axis=1, keepdims=True)
        bound = bound + jnp.maximum(bmax, 0.0)
    lane = lax.broadcasted_iota(jnp.int32, (1, V7X_LANES), 1)
    scale = C_DH ** -0.5
    m_first = jnp.where(lane < C_DH, scale, 0.0).astype(BF16)
    m_second = jnp.where(lane >= C_DH, scale, 0.0).astype(BF16)
    lane_out = lax.broadcasted_iota(jnp.int32, (GRID_W, V7X_LANES), 1)
    n_loc = NA_ROWS * GRID_W
    hq = 2 * GRID_W
    with_ones = lambda v: jnp.concatenate([v, jnp.ones_like(v)], axis=1)
    first_key_row = [min(max(r - NA_ROWS // 2, 0), n_rows - NA_ROWS) for r in range(n_rows)]

    def attend(cols, shift):
        blocks = []
        for r in range(n_rows):
            qr = q_ref[r * GRID_W:(r + 1) * GRID_W, cols]
            blocks += [qr * m_first, qr * m_second]
        qbd = jnp.concatenate(blocks, axis=0)
        s_ctx = _dot_nt(qbd, kc_ref[:, cols])
        if local:
            s_loc = jnp.concatenate(
                [_dot_nt(qbd[r * hq:(r + 1) * hq], k_ref[rs * GRID_W:rs * GRID_W + n_loc, cols])
                 + bias_ref[rs - r + NA_ROWS - 1] for r, rs in enumerate(first_key_row)], axis=0)
        if shift is None:
            shift = jnp.max(s_ctx, axis=-1, keepdims=True)
            if local:
                shift = jnp.maximum(shift, jnp.max(s_loc, axis=-1, keepdims=True))
        e_ctx = jnp.exp(s_ctx - shift)
        full = _dot(e_ctx.astype(BF16), with_ones(vc_ref[:, cols]))
        if local:
            e_loc = jnp.exp(s_loc - shift).astype(BF16)
            full = full + jnp.concatenate(
                [_dot(e_loc[r * hq:(r + 1) * hq], with_ones(v_ref[rs * GRID_W:rs * GRID_W + n_loc, cols]))
                 for r, rs in enumerate(first_key_row)], axis=0)
        denom = full[:, V7X_LANES:V7X_LANES + 1]
        full = full[:, :V7X_LANES] * (1.0 / denom)
        out = jnp.concatenate(
            [jnp.where(lane_out < C_DH, full[r * hq:r * hq + GRID_W], full[r * hq + GRID_W:(r + 1) * hq])
             for r in range(n_rows)], axis=0)
        o_ref[:, cols] = out.astype(o_ref.dtype)
        return jnp.min(denom, axis=0, keepdims=True)

    pair_cols = [slice(p * V7X_LANES, (p + 1) * V7X_LANES) for p in range(pairs)]
    smallest = functools.reduce(jnp.minimum, [attend(cols, bound) for cols in pair_cols])

    @pl.when(jnp.min(smallest) < DENOM_FLOOR)
    def _():
        for cols in pair_cols:
            attend(cols, None)


def _nbr_attn(q, k, v, kc, vc, bias_t, qk_g, pairs_per_step):
    b, lq, wdt = q.shape
    lc = kc.shape[1]
    n_rows = lq // GRID_W
    local = k is not None
    blk = pairs_per_step * V7X_LANES
    pair = lambda n: pl.BlockSpec((None, n, blk), lambda h, bi: (bi, 0, h))
    if local:
        assert pairs_per_step == 1
        in_specs = [_resident(qk_g.shape), pair(lq), pair(lq), pair(lq), pair(lc), pair(lc),
                    pl.BlockSpec((None,) + bias_t.shape[1:], lambda h, bi: (h, 0, 0, 0))]
        args = [qk_g, q, k, v, kc, vc, bias_t]
    else:
        in_specs = [_resident(qk_g.shape), pair(lq), pair(lc), pair(lc)]
        args = [qk_g, q, kc, vc]
    return pl.pallas_call(
        functools.partial(_nbr_attn_kernel, n_rows=n_rows, pairs=pairs_per_step, local=local),
        grid=(wdt // blk, b),
        in_specs=in_specs,
        out_specs=pair(lq),
        out_shape=jax.ShapeDtypeStruct((b, lq, wdt), BF16),
        compiler_params=_params(2, V7X_VMEM_LIMIT),
        name="nbr_attn" if local else "ctx_attn",
    )(*args)


def _nbr_bias_table(rpb):
    qc = np.arange(GRID_W)[:, None]
    kc = np.arange(GRID_W)[None, :]
    win0 = np.clip(qc - NA_COLS // 2, 0, GRID_W - NA_COLS)
    valid = (kc >= win0) & (kc < win0 + NA_COLS)
    dc = np.clip(kc - qc + NA_COLS - 1, 0, 2 * NA_COLS - 2)
    t = jnp.where(valid[None, None], rpb[:, :, dc], -jnp.inf)
    h = t.shape[0]
    dr = np.arange(NA_ROWS)[:, None] + np.arange(NA_ROWS)[None, :]
    t = t[:, dr]
    t = t.reshape(h // 2, 2, NA_ROWS, NA_ROWS, GRID_W, GRID_W).transpose(0, 2, 1, 4, 3, 5)
    return t.reshape(h // 2, NA_ROWS, 2 * GRID_W, NA_ROWS * GRID_W)


def _merge_kernel(x_ref, sh_ref, sc_ref, g1_ref, ng_ref, ya_ref, yb_ref, yc_ref, yd_ref,
                  wg_ref, wb_ref, wo_ref, o_ref):
    x = x_ref[...]
    d = x.shape[-1]
    hb = _norm_modulate(x, ng_ref[...], sh_ref[...], sc_ref[...])
    acc = None
    for i, y_ref in enumerate((ya_ref, yb_ref, yc_ref, yd_ref)):
        gate = _dot(hb, wg_ref[:, i * d:(i + 1) * d])
        term = jax.nn.sigmoid(gate) * _dot(y_ref[...], wb_ref[i * SEG_W:(i + 1) * SEG_W, :])
        acc = term if acc is None else acc + term
    o_ref[...] = x + g1_ref[...] * _dot(acc.astype(BF16), wo_ref[...])


def _merge(x, shift, scale, g1, norm_g, ys, w_gate, w_branch, w_out, layer):
    b, l, d = x.shape
    tm = min(ROW_TILE, l)
    batched = shift.shape[0] != 1
    mod_map = (lambda bi, i: (bi, 0, 0)) if batched else (lambda bi, i: (0, 0, 0))
    x_tile = pl.BlockSpec((None, tm, d), lambda bi, i: (bi, i, 0))
    y_tile = pl.BlockSpec((None, tm, SEG_W), lambda bi, i: (bi, i, 0))
    mod = pl.BlockSpec((None, 1, d), mod_map)
    weights = sum(math.prod(w.shape[1:]) for w in (w_gate, w_branch, w_out)) * 2
    vmem = weights + 2 * (2 * tm * d * 4 + 4 * tm * SEG_W * 2) + 8 * tm * d * 4 + (8 << 20)
    return pl.pallas_call(
        _merge_kernel,
        grid=(b, l // tm),
        in_specs=[x_tile, mod, mod, mod, _resident((1, d)), y_tile, y_tile, y_tile, y_tile,
                  _resident(w_gate.shape[1:], layer), _resident(w_branch.shape[1:], layer),
                  _resident(w_out.shape[1:], layer)],
        out_specs=x_tile,
        out_shape=jax.ShapeDtypeStruct(x.shape, F32),
        compiler_params=_params(2, vmem),
        name="merge",
    )(x, shift, scale, g1, norm_g, *ys, w_gate, w_branch, w_out)


def _ffn_kernel(x_ref, sh_ref, sc_ref, g2_ref, ng_ref, wgu_ref, wd_ref, o_ref, u_ref):
    x = x_ref[...]
    hb = _norm_modulate(x, ng_ref[...], sh_ref[...], sc_ref[...])
    hidden = wd_ref.shape[0]
    for c in range(hidden // FFN_CHUNK):
        cols = slice(c * FFN_CHUNK, (c + 1) * FFN_CHUNK)
        a = _dot(hb, wgu_ref[:, cols])
        bb = _dot(hb, wgu_ref[:, hidden + c * FFN_CHUNK:hidden + (c + 1) * FFN_CHUNK])
        u_ref[:, cols] = (a * jax.nn.sigmoid(a) * bb).astype(BF16)
    o_ref[...] = x + g2_ref[...] * _dot(u_ref[...], wd_ref[...])


def _ffn(x, shift, scale, g2, norm_g, w_gu, w_down, layer):
    b, l, d = x.shape
    tm = min(ROW_TILE, l)
    hidden = w_down.shape[1]
    batched = shift.shape[0] != 1
    mod_map = (lambda bi, i: (bi, 0, 0)) if batched else (lambda bi, i: (0, 0, 0))
    x_tile = pl.BlockSpec((None, tm, d), lambda bi, i: (bi, i, 0))
    mod = pl.BlockSpec((None, 1, d), mod_map)
    vmem = (math.prod(w_gu.shape[1:]) + math.prod(w_down.shape[1:])) * 2 + 4 * tm * d * 4 + tm * hidden * 2 + 8 * tm * d * 4 + (8 << 20)
    return pl.pallas_call(
        _ffn_kernel,
        grid=(b, l // tm),
        in_specs=[x_tile, mod, mod, mod, _resident((1, d)),
                  _resident(w_gu.shape[1:], layer), _resident(w_down.shape[1:], layer)],
        out_specs=x_tile,
        out_shape=jax.ShapeDtypeStruct(x.shape, F32),
        scratch_shapes=[pltpu.VMEM((tm, hidden), BF16)],
        compiler_params=_params(2, vmem),
        name="ffn",
    )(x, shift, scale, g2, norm_g, w_gu, w_down)


def _rope_tables(n_tok):
    nf = A_DH // 4
    t = np.arange(n_tok)
    row = (t // GRID_W).astype(np.float32)
    col = (t % GRID_W).astype(np.float32)
    inv = np.float32(ROPE_THETA) ** (-np.arange(nf, dtype=np.float32) / np.float32(nf))
    ar = row[:, None] * inv
    ac = col[:, None] * inv
    ang = np.concatenate([ar, ar, ac, ac], axis=-1).astype(np.float64)
    cos, sin = np.cos(ang), np.sin(ang)
    quarter = (np.arange(A_DH) // nf) % 2
    sin_up = np.where(quarter == 0, -sin, 0.0)
    sin_dn = np.where(quarter == 1, sin, 0.0)
    rep = V7X_LANES // A_DH
    return tuple(jnp.asarray(np.tile(a, (1, rep)), dtype=F32) for a in (cos, sin_up, sin_dn))


def kernel(x, c, ctx, c_ctx, w_mod, b_mod, norm1_g, w_in, a_qk_g, a_lambda, a_subln_g, b_pool_w,
           b_pool_s, c_qk_g, c_rpb, d_vn_g, d_ws, d_bs, w_branch, w_out, norm2_g, w_gu, w_down):
    b, s, d = x.shape
    depth = w_mod.shape[0]
    rope_tabs = _rope_tables(s)

    rows = -(-(b + 1) // 8) * 8
    c_all = jnp.zeros((rows, d), F32).at[:b].set(c).at[b].set(c_ctx)
    mod = _modulation(c_all, w_mod, b_mod)

    w_mix = w_in[:, :, :MIX_COLS].astype(BF16)
    w_gate = w_in[:, :, MIX_COLS:].astype(BF16)
    w_br = w_branch.astype(BF16)
    w_o = w_out.astype(BF16)
    w_gu_b = w_gu.astype(BF16)
    w_dn = w_down.astype(BF16)
    w_pool = b_pool_w.astype(BF16)
    w_s = d_ws.astype(BF16)

    for l in range(depth):
        last = l == depth - 1
        lam_init = 0.8 - 0.6 * math.exp(-0.3 * l)
        mx = [mod[l, :b, k * d:(k + 1) * d].reshape(b, 1, d) for k in range(6)]
        mc = [mod[l, b:b + 1, k * d:(k + 1) * d].reshape(1, 1, d) for k in range(6)]
        n1 = norm1_g[l].reshape(1, d)
        n2 = norm2_g[l].reshape(1, d)
        rep = SEG_W // A_DH
        qk_gains = jnp.stack([jnp.tile(a_qk_g[l, 0], rep), jnp.tile(a_qk_g[l, 1], rep),
                              jnp.tile(c_qk_g[l, 0], rep), jnp.tile(c_qk_g[l, 1], rep)])
        vn_g = d_vn_g[l].reshape(1, SEG_W)
        bs_full = jnp.repeat(d_bs[l].T, V7X_LANES, axis=1)
        bias_t = _nbr_bias_table(c_rpb[l])

        px = _in_proj(x, mx[0], mx[1], n1, w_mix, qk_gains, vn_g, w_s, bs_full, rope_tabs, MIXER_SEGS, l)
        pc = _in_proj(ctx, mc[0], mc[1], n1, w_mix, qk_gains, vn_g, w_s, bs_full, None,
                      CTX_KV_SEGS if last else MIXER_SEGS, l)

        ya = _diff_attn(px[SEG_AQ], [(px[SEG_AK], px[SEG_AV]), (pc[SEG_AK], pc[SEG_AV])],
                        a_lambda[l], a_subln_g[l], a_qk_g[l], lam_init, heads_per_step=1)
        yb = _pool(px[SEG_B], w_pool, b_pool_s[l], l)
        yc = _nbr_attn(px[SEG_CQ], px[SEG_CK], px[SEG_CV], pc[SEG_CK], pc[SEG_CV], bias_t, c_qk_g[l],
                       pairs_per_step=1)
        x = _merge(x, mx[0], mx[1], mx[2], n1, (ya, yb, yc, px[SEG_YD]), w_gate, w_br, w_o, l)
        x = _ffn(x, mx[3], mx[4], mx[5], n2, w_gu_b, w_dn, l)

        if not last:
            ya_c = _diff_attn(pc[SEG_AQ], [(pc[SEG_AK], pc[SEG_AV])], a_lambda[l], a_subln_g[l], a_qk_g[l], lam_init,
                              heads_per_step=A_HEADS)
            yb_c = _pool(pc[SEG_B], w_pool, b_pool_s[l], l)
            yc_c = _nbr_attn(pc[SEG_CQ], None, None, pc[SEG_CK], pc[SEG_CV], None, c_qk_g[l],
                             pairs_per_step=SEG_W // V7X_LANES)
            ctx = _merge(ctx, mc[0], mc[1], mc[2], n1, (ya_c, yb_c, yc_c, pc[SEG_YD]), w_gate, w_br, w_o, l)
            ctx = _ffn(ctx, mc[3], mc[4], mc[5], n2, w_gu_b, w_dn, l)
    return x
```

```python
import functools
import math

import jax
import jax.numpy as jnp
import numpy as np
from jax import lax
from jax.experimental import pallas as pl
from jax.experimental.pallas import tpu as pltpu

F32 = jnp.float32
BF16 = jnp.bfloat16

D_MODEL = 1024
GRID_W = 64
EPS = 1e-6
ROPE_THETA = 10000.0
A_HEADS = 4
A_DH = 64
POOL_WINDOWS = (2, 4, 8, 16)
C_DH = 64
NA_ROWS = 8
NA_COLS = 16
CHUNK = 128
N_BRANCH = 4
SEG_W = 512
N_SEG = 9
MIX_COLS = N_SEG * SEG_W
FFN_HIDDEN = 2816
SEG_AQ, SEG_AK, SEG_AV, SEG_B, SEG_CQ, SEG_CK, SEG_CV, SEG_DU, SEG_DV = range(N_SEG)
SEG_YD = N_SEG
SEG_DTYPES = (BF16, BF16, BF16, F32, BF16, BF16, BF16, F32, BF16, BF16)
MIXER_SEGS = (SEG_AQ, SEG_AK, SEG_AV, SEG_B, SEG_CQ, SEG_CK, SEG_CV, SEG_YD)
CTX_KV_SEGS = (SEG_AK, SEG_AV, SEG_CK, SEG_CV)

V7X_LANES = 128
V7X_VMEM_BYTES = 64 * 1024 * 1024
V7X_VMEM_LIMIT = V7X_VMEM_BYTES - 8 * 1024 * 1024

ROW_TILE = 1024
ATTN_Q_TILE = 256
ATTN_KEY_CHUNK = 256
SCORE_BOUND_SLACK = 1.02
DENOM_FLOOR = 1e-18
POOL_TILE = 256
POOL_HALO = 8
FFN_CHUNK = 256


def _params(n_axes, vmem_bytes):
    return pltpu.CompilerParams(
        dimension_semantics=("parallel",) * n_axes,
        vmem_limit_bytes=int(min(vmem_bytes, V7X_VMEM_LIMIT)))


def _resident(shape, layer=None):
    zeros = (0,) * len(shape)
    if layer is None:
        return pl.BlockSpec(shape, lambda *_: zeros, pipeline_mode=pl.Buffered(1))
    return pl.BlockSpec((None,) + tuple(shape), lambda *_: (layer,) + zeros, pipeline_mode=pl.Buffered(1))


def _dot(a, b):
    return jnp.dot(a, b, preferred_element_type=F32)


def _dot_nt(a, b):
    return lax.dot_general(a, b, (((1,), (1,)), ((), ())), preferred_element_type=F32)


def _dot_tn(a, b):
    return lax.dot_general(a, b, (((0,), (0,)), ((), ())), preferred_element_type=F32)


def _split_bf16(t):
    hi = t.astype(BF16)
    lo = (t - hi.astype(F32)).astype(BF16)
    return hi, lo


def _norm_modulate(x, gain, shift, scale):
    ms = jnp.mean(x * x, axis=-1, keepdims=True)
    n = x * lax.rsqrt(ms + EPS)
    return ((n * gain) * (1.0 + scale) + shift).astype(BF16)


def _mod_kernel(c_ref, w_ref, b_ref, o_ref):
    a = c_ref[...]
    a = a * jax.nn.sigmoid(a)
    a_hi, a_lo = _split_bf16(a)
    w_hi, w_lo = _split_bf16(w_ref[...])
    o_ref[...] = _dot(a_hi, w_hi) + _dot(a_hi, w_lo) + _dot(a_lo, w_hi) + b_ref[...]


def _modulation(c_all, w_mod, b_mod):
    depth, d, n = w_mod.shape
    rows = c_all.shape[0]
    tn = 768
    return pl.pallas_call(
        _mod_kernel,
        grid=(depth, n // tn),
        in_specs=[
            pl.BlockSpec((rows, d), lambda l, j: (0, 0)),
            pl.BlockSpec((None, d, tn), lambda l, j: (l, 0, j)),
            pl.BlockSpec((None, 1, tn), lambda l, j: (l, 0, j)),
        ],
        out_specs=pl.BlockSpec((None, rows, tn), lambda l, j: (l, 0, j)),
        out_shape=jax.ShapeDtypeStruct((depth, rows, n), F32),
        compiler_params=_params(2, 32 * 1024 * 1024),
        name="modulation",
    )(c_all, w_mod, b_mod.reshape(depth, 1, n))


def _half_block_rms(t, first_half, gain):
    sq = t * t
    s_first = jnp.sum(jnp.where(first_half, sq, 0.0), axis=-1, keepdims=True)
    s_second = jnp.sum(jnp.where(first_half, 0.0, sq), axis=-1, keepdims=True)
    ms = jnp.where(first_half, s_first, s_second) * (1.0 / A_DH)
    return t * lax.rsqrt(ms + EPS) * gain


def _rope_block(t, cos, sin_up, sin_dn):
    return (t * cos + pltpu.roll(t, V7X_LANES - 16, 1) * sin_up + pltpu.roll(t, 16, 1) * sin_dn)


def _in_proj_kernel(*refs, rope, out_segs):
    x_ref, sh_ref, sc_ref, ng_ref, w_ref, qkg_ref, vng_ref, ws_ref, bs_ref = refs[:9]
    pos = 9
    if rope:
        cos_ref, sup_ref, sdn_ref = refs[9:12]
        pos = 12
    outs = dict(zip(out_segs, refs[pos:pos + len(out_segs)]))
    gated = SEG_YD in outs
    needed = set(out_segs) | ({SEG_DU, SEG_DV} if gated else set())
    hb = _norm_modulate(x_ref[...], ng_ref[...], sh_ref[...], sc_ref[...])
    qk_row = {SEG_AQ: 0, SEG_AK: 1, SEG_CQ: 2, SEG_CK: 3}
    first_half = lax.broadcasted_iota(jnp.int32, (1, V7X_LANES), 1) < A_DH
    for seg in range(N_SEG):
        if seg not in needed:
            continue
        r = _dot(hb, w_ref[:, seg * SEG_W:(seg + 1) * SEG_W])
        if seg in qk_row:
            row = qk_row[seg]
            blocks = []
            for j in range(SEG_W // V7X_LANES):
                lanes = slice(j * V7X_LANES, (j + 1) * V7X_LANES)
                t = _half_block_rms(r[:, lanes], first_half, qkg_ref[row:row + 1, lanes])
                if rope and seg in (SEG_AQ, SEG_AK):
                    t = _rope_block(t, cos_ref[...], sup_ref[...], sdn_ref[...])
                blocks.append(t)
            r = jnp.concatenate(blocks, axis=1)
        elif seg == SEG_DU and gated:
            gate_u = r
        elif seg == SEG_DV:
            ms = jnp.mean(r * r, axis=-1, keepdims=True)
            r = r * lax.rsqrt(ms + EPS) * vng_ref[...]
            if gated:
                vb = r.astype(BF16)
                n_chunks = r.shape[0] // CHUNK
                for g in range(ws_ref.shape[0]):
                    lanes = slice(g * V7X_LANES, (g + 1) * V7X_LANES)
                    v_wide = jnp.concatenate([vb[n * CHUNK:(n + 1) * CHUNK, lanes] for n in range(n_chunks)], axis=1)
                    sv_wide = _dot(ws_ref[g], v_wide)
                    for n in range(n_chunks):
                        rows = slice(n * CHUNK, (n + 1) * CHUNK)
                        sv = sv_wide[:, n * V7X_LANES:(n + 1) * V7X_LANES] + bs_ref[:, lanes]
                        outs[SEG_YD][rows, lanes] = (gate_u[rows, lanes] * sv).astype(outs[SEG_YD].dtype)
        if seg in outs:
            outs[seg][...] = r.astype(outs[seg].dtype)


def _in_proj(x, shift, scale, norm_g, w_mix, qk_gains, vn_g, w_s, b_full, rope_tabs, out_segs, layer):
    b, l, d = x.shape
    tm = min(ROW_TILE, l)
    batched = shift.shape[0] != 1
    mod_map = (lambda bi, i: (bi, 0, 0)) if batched else (lambda bi, i: (0, 0, 0))
    rope = rope_tabs is not None
    in_specs = [
        pl.BlockSpec((None, tm, d), lambda bi, i: (bi, i, 0)),
        pl.BlockSpec((None, 1, d), mod_map),
        pl.BlockSpec((None, 1, d), mod_map),
        _resident((1, d)),
        _resident((d, MIX_COLS), layer),
        _resident((4, SEG_W)),
        _resident((1, SEG_W)),
        _resident(w_s.shape[1:], layer),
        _resident(b_full.shape),
    ]
    args = [x, shift, scale, norm_g, w_mix, qk_gains, vn_g, w_s, b_full]
    if rope:
        in_specs += [pl.BlockSpec((tm, V7X_LANES), lambda bi, i: (i, 0))] * 3
        args += list(rope_tabs)
    dtypes = [SEG_DTYPES[s] for s in out_segs]
    out_bytes = sum(tm * SEG_W * jnp.dtype(t).itemsize for t in dtypes)
    vmem = d * MIX_COLS * 2 + 2 * (tm * d * 4 + out_bytes) + 8 * tm * SEG_W * 4 + (8 << 20)
    outs = pl.pallas_call(
        functools.partial(_in_proj_kernel, rope=rope, out_segs=tuple(out_segs)),
        grid=(b, l // tm),
        in_specs=in_specs,
        out_specs=[pl.BlockSpec((None, tm, SEG_W), lambda bi, i: (bi, i, 0))] * len(out_segs),
        out_shape=[jax.ShapeDtypeStruct((b, l, SEG_W), t) for t in dtypes],
        compiler_params=_params(2, vmem),
        name="in_proj_rope" if rope else "in_proj",
    )(*args)
    return dict(zip(out_segs, outs))


def _diff_attn_kernel(*refs, n_parts, lq, tq, heads, lam_init):
    al_ref, sg_ref, qkg_ref, q_ref = refs[:4]
    kv = refs[4:4 + 2 * n_parts]
    o_ref, kcat_ref, vt_ref, s_ref, e_ref = refs[4 + 2 * n_parts:]
    al = al_ref[...]
    lam = (jnp.exp(jnp.sum(al[0:1] * al[1:2], axis=-1, keepdims=True))
           - jnp.exp(jnp.sum(al[2:3] * al[3:4], axis=-1, keepdims=True)) + lam_init)
    lane = lax.broadcasted_iota(jnp.int32, (1, V7X_LANES), 1)
    scale = A_DH ** -0.5
    masks = (jnp.where(lane < A_DH, scale, 0.0).astype(BF16),
             jnp.where(lane >= A_DH, scale, 0.0).astype(BF16))
    sub_gain = sg_ref[...] * (1.0 - lam_init)
    gains = jnp.max(jnp.abs(qkg_ref[...]), axis=-1, keepdims=True)
    bound = gains[0:1] * gains[1:2] * (A_DH * scale * SCORE_BOUND_SLACK)
    hw = 2 * A_DH
    n_keys = kcat_ref.shape[0]
    kc = math.gcd(n_keys, ATTN_KEY_CHUNK)
    chunks = range(0, n_keys, kc)
    n_tiles = lq // tq
    vt_ref[hw:, :] = jnp.ones((vt_ref.shape[0] - hw, n_keys), BF16)

    def finish(rows, cols, acc):
        o = (acc[0][:hw] * (1.0 / acc[0][hw:hw + 1]) - acc[1][:hw] * (lam / acc[1][hw:hw + 1])).T
        ms = jnp.mean(o * o, axis=-1, keepdims=True)
        o_ref[rows, cols] = (o * lax.rsqrt(ms + EPS) * sub_gain).astype(o_ref.dtype)

    def bounded_tile(cols, slot, r0):
        qc = q_ref[r0:r0 + tq, cols]
        acc = []
        for c, mask in enumerate(masks):
            qm = qc * mask
            for k0 in chunks:
                s = _dot_nt(kcat_ref[k0:k0 + kc, :], qm)
                e_ref[slot, c, k0:k0 + kc, :] = jnp.exp(s - bound).astype(BF16)
            acc.append(_dot(vt_ref[...], e_ref[slot, c]))
        finish(slice(r0, r0 + tq), cols, acc)
        return jnp.minimum(acc[0][hw:hw + 1], acc[1][hw:hw + 1])

    def exact_tile(cols, r0):
        qc = q_ref[pl.ds(r0, tq), cols]
        acc = []
        for c, mask in enumerate(masks):
            qm = qc * mask
            m8 = None
            for k0 in chunks:
                s = _dot_nt(kcat_ref[k0:k0 + kc, :], qm)
                s_ref[c, k0:k0 + kc, :] = s
                cm = jnp.max(s.reshape(kc // 8, 8, tq), axis=0)
                m8 = cm if m8 is None else jnp.maximum(m8, cm)
            m = jnp.max(m8, axis=0, keepdims=True)
            for k0 in chunks:
                rows = slice(k0, k0 + kc)
                e_ref[0, c, rows, :] = jnp.exp(s_ref[c, rows, :] - m).astype(BF16)
            acc.append(_dot(vt_ref[...], e_ref[0, c]))
        finish(pl.ds(r0, tq), cols, acc)

    def load_head(cols):
        off = 0
        for p in range(n_parts):
            n = kv[2 * p].shape[0]
            kcat_ref[off:off + n, :] = kv[2 * p][:, cols]
            vt_ref[0:hw, off:off + n] = kv[2 * p + 1][:, cols].astype(F32).T.astype(BF16)
            off += n

    for h in range(heads):
        cols = slice(h * hw, (h + 1) * hw)
        load_head(cols)
        smallest = None
        for j in range(n_tiles):
            denom = bounded_tile(cols, j % 2, j * tq)
            smallest = denom if smallest is None else jnp.minimum(smallest, denom)

        @pl.when(jnp.min(smallest) < DENOM_FLOOR)
        def _():
            def redo(j, carry):
                exact_tile(cols, pl.multiple_of(j * tq, tq))
                return carry
            lax.fori_loop(0, n_tiles, redo, 0)


def _diff_attn(q, kv_parts, a_lambda, subln_g, qk_g, lam_init, heads_per_step):
    b, lq, _ = q.shape
    tq = min(ATTN_Q_TILE, lq)
    hw = 2 * A_DH
    wdt = heads_per_step * hw
    head_spec = lambda n: pl.BlockSpec((None, n, wdt), lambda bi, h: (bi, 0, h))
    in_specs = [_resident(a_lambda.shape), _resident((1, hw)), _resident(qk_g.shape), head_spec(lq)]
    args = [a_lambda, subln_g.reshape(1, hw), qk_g, q]
    lk = 0
    for k, v in kv_parts:
        in_specs += [head_spec(k.shape[1]), head_spec(v.shape[1])]
        args += [k, v]
        lk += k.shape[1]
    ones_rows = 16
    scratch = [pltpu.VMEM((lk, hw), BF16), pltpu.VMEM((hw + ones_rows, lk), BF16),
               pltpu.VMEM((2, lk, tq), F32), pltpu.VMEM((2, 2, lk, tq), BF16)]
    vmem = (2 * (2 * lq + 4 * lk) * wdt * 2 + 2 * lk * (hw + ones_rows) * 2 + 2 * lk * tq * 8
            + 6 * ATTN_KEY_CHUNK * tq * 4 + (8 << 20))
    return pl.pallas_call(
        functools.partial(_diff_attn_kernel, n_parts=len(kv_parts), lq=lq, tq=tq,
                          heads=heads_per_step, lam_init=lam_init),
        grid=(b, A_HEADS // heads_per_step),
        in_specs=in_specs,
        out_specs=head_spec(lq),
        out_shape=jax.ShapeDtypeStruct((b, lq, A_HEADS * hw), BF16),
        scratch_shapes=scratch,
        compiler_params=_params(2, vmem),
        name="diff_attn",
    )(*args)


def _tree_sum(terms):
    while len(terms) > 1:
        terms = [terms[i] + terms[i + 1] for i in range(0, len(terms), 2)]
    return terms[0]


def _pool_kernel(p_ref, w_ref, s_ref, o_ref, pad_ref, *, l, tp):
    wdt = p_ref.shape[-1]
    pad_ref[0:POOL_HALO, :] = jnp.zeros((POOL_HALO, wdt), F32)
    pad_ref[POOL_HALO + l:, :] = jnp.zeros((POOL_HALO, wdt), F32)
    pad_ref[POOL_HALO:POOL_HALO + l, :] = p_ref[...]
    n_tiles = l // tp
    for ti in range(n_tiles):
        t0 = ti * tp
        edge = ti == 0 or ti == n_tiles - 1
        if edge:
            t_row = t0 + lax.broadcasted_iota(jnp.int32, (tp, V7X_LANES), 0)
        for g, w in enumerate(POOL_WINDOWS):
            lanes = slice(g * V7X_LANES, (g + 1) * V7X_LANES)
            shifted = [pad_ref[POOL_HALO + t0 + j:POOL_HALO + t0 + j + tp, lanes]
                       for j in range(-(w // 2), w // 2)]
            centre = shifted[w // 2]
            total = _tree_sum(shifted)
            if edge:
                cnt = (jnp.minimum(t_row + w // 2, l) - jnp.maximum(t_row - w // 2, 0)).astype(F32)
                mean = total / cnt
            else:
                mean = total * (1.0 / w)
            y = _dot((mean - centre).astype(BF16), w_ref[g]) * s_ref[:, lanes]
            o_ref[t0:t0 + tp, lanes] = y.astype(o_ref.dtype)


def _pool(p, w_pool, s_pool, layer):
    b, l, wdt = p.shape
    tp = min(POOL_TILE, l)
    seq = pl.BlockSpec((None, l, wdt), lambda bi: (bi, 0, 0))
    return pl.pallas_call(
        functools.partial(_pool_kernel, l=l, tp=tp),
        grid=(b,),
        in_specs=[seq, _resident(w_pool.shape[1:], layer), _resident((1, wdt))],
        out_specs=seq,
        out_shape=jax.ShapeDtypeStruct((b, l, wdt), BF16),
        scratch_shapes=[pltpu.VMEM((l + 2 * POOL_HALO, wdt), F32)],
        compiler_params=_params(1, 5 * l * wdt * 4 + (16 << 20)),
        name="pool",
    )(p, w_pool, s_pool.reshape(1, wdt))


def _nbr_attn_kernel(*refs, n_rows, pairs, local):
    if local:
        qkg_ref, q_ref, k_ref, v_ref, kc_ref, vc_ref, bias_ref, o_ref = refs
    else:
        qkg_ref, q_ref, kc_ref, vc_ref, o_ref = refs
    gains = jnp.max(jnp.abs(qkg_ref[...]), axis=-1, keepdims=True)
    bound = gains[0:1] * gains[1:2] * (C_DH ** 0.5 * SCORE_BOUND_SLACK)
    if local:
        bmax = jnp.max(jnp.max(jnp.max(bias_ref[...], axis=0), axis=0, keepdims=True), axis=1, keepdims=True)
        bound = bound + jnp.maximum(bmax, 0.0)
    lane = lax.broadcasted_iota(jnp.int32, (1, V7X_LANES), 1)
    scale = C_DH ** -0.5
    m_first = jnp.where(lane < C_DH, scale, 0.0).astype(BF16)
    m_second = jnp.where(lane >= C_DH, scale, 0.0).astype(BF16)
    lane_out = lax.broadcasted_iota(jnp.int32, (GRID_W, V7X_LANES), 1)
    n_loc = NA_ROWS * GRID_W
    hq = 2 * GRID_W
    with_ones = lambda v: jnp.concatenate([v, jnp.ones_like(v)], axis=1)
    first_key_row = [min(max(r - NA_ROWS // 2, 0), n_rows - NA_ROWS) for r in range(n_rows)]

    def attend(cols, shift):
        blocks = []
        for r in range(n_rows):
            qr = q_ref[r * GRID_W:(r + 1) * GRID_W, cols]
            blocks += [qr * m_first, qr * m_second]
        qbd = jnp.concatenate(blocks, axis=0)
        s_ctx = _dot_nt(qbd, kc_ref[:, cols])
        if local:
            s_loc = jnp.concatenate(
                [_dot_nt(qbd[r * hq:(r + 1) * hq], k_ref[rs * GRID_W:rs * GRID_W + n_loc, cols])
                 + bias_ref[rs - r + NA_ROWS - 1] for r, rs in enumerate(first_key_row)], axis=0)
        if shift is None:
            shift = jnp.max(s_ctx, axis=-1, keepdims=True)
            if local:
                shift = jnp.maximum(shift, jnp.max(s_loc, axis=-1, keepdims=True))
        e_ctx = jnp.exp(s_ctx - shift)
        full = _dot(e_ctx.astype(BF16), with_ones(vc_ref[:, cols]))
        if local:
            e_loc = jnp.exp(s_loc - shift).astype(BF16)
            full = full + jnp.concatenate(
                [_dot(e_loc[r * hq:(r + 1) * hq], with_ones(v_ref[rs * GRID_W:rs * GRID_W + n_loc, cols]))
                 for r, rs in enumerate(first_key_row)], axis=0)
        denom = full[:, V7X_LANES:V7X_LANES + 1]
        full = full[:, :V7X_LANES] * (1.0 / denom)
        out = jnp.concatenate(
            [jnp.where(lane_out < C_DH, full[r * hq:r * hq + GRID_W], full[r * hq + GRID_W:(r + 1) * hq])
             for r in range(n_rows)], axis=0)
        o_ref[:, cols] = out.astype(o_ref.dtype)
        return jnp.min(denom, axis=0, keepdims=True)

    pair_cols = [slice(p * V7X_LANES, (p + 1) * V7X_LANES) for p in range(pairs)]
    smallest = functools.reduce(jnp.minimum, [attend(cols, bound) for cols in pair_cols])

    @pl.when(jnp.min(smallest) < DENOM_FLOOR)
    def _():
        for cols in pair_cols:
            attend(cols, None)


def _nbr_attn(q, k, v, kc, vc, bias_t, qk_g, pairs_per_step):
    b, lq, wdt = q.shape
    lc = kc.shape[1]
    n_rows = lq // GRID_W
    local = k is not None
    blk = pairs_per_step * V7X_LANES
    pair = lambda n: pl.BlockSpec((None, n, blk), lambda h, bi: (bi, 0, h))
    if local:
        assert pairs_per_step == 1
        in_specs = [_resident(qk_g.shape), pair(lq), pair(lq), pair(lq), pair(lc), pair(lc),
                    pl.BlockSpec((None,) + bias_t.shape[1:], lambda h, bi: (h, 0, 0, 0))]
        args = [qk_g, q, k, v, kc, vc, bias_t]
    else:
        in_specs = [_resident(qk_g.shape), pair(lq), pair(lc), pair(lc)]
        args = [qk_g, q, kc, vc]
    return pl.pallas_call(
        functools.partial(_nbr_attn_kernel, n_rows=n_rows, pairs=pairs_per_step, local=local),
        grid=(wdt // blk, b),
        in_specs=in_specs,
        out_specs=pair(lq),
        out_shape=jax.ShapeDtypeStruct((b, lq, wdt), BF16),
        compiler_params=_params(2, V7X_VMEM_LIMIT),
        name="nbr_attn" if local else "ctx_attn",
    )(*args)


N_ROW_OFFSETS = 2 * NA_ROWS - 1
ROW_OFFSETS_PAD = 16


def _nbr_bias_kernel(base_ref, mask_ref, o_ref):
    lane = lax.broadcasted_iota(jnp.int32, (GRID_W, V7X_LANES), 1)
    first_half = lane < GRID_W
    blocks = []
    for h in range(2):
        per_offset = []
        for dr in range(N_ROW_OFFSETS):
            row = base_ref[h * ROW_OFFSETS_PAD + dr:h * ROW_OFFSETS_PAD + dr + 1, :]
            low = pltpu.roll(jnp.broadcast_to(row, (GRID_W, V7X_LANES)), 0, 1, stride=1, stride_axis=0)
            per_offset.append((low, pltpu.roll(low, GRID_W, 1)))
        blocks.append(per_offset)
    for d0 in range(NA_ROWS):
        for h in range(2):
            rows = slice(h * GRID_W, (h + 1) * GRID_W)
            for c in range(NA_ROWS // 2):
                lanes = slice(c * V7X_LANES, (c + 1) * V7X_LANES)
                blk = jnp.where(first_half, blocks[h][d0 + 2 * c][0], blocks[h][d0 + 2 * c + 1][1])
                o_ref[d0, rows, lanes] = blk + mask_ref[rows, lanes]


def _nbr_bias_table(rpb):
    h, n_dr, n_dc = rpb.shape
    base = jnp.roll(jnp.pad(rpb, ((0, 0), (0, ROW_OFFSETS_PAD - n_dr), (0, V7X_LANES - n_dc))),
                    -(NA_COLS - 1), axis=-1).reshape(h // 2, 2 * ROW_OFFSETS_PAD, V7X_LANES)
    qc = np.arange(GRID_W)[:, None]
    kc = np.arange(GRID_W)[None, :]
    win0 = np.clip(qc - NA_COLS // 2, 0, GRID_W - NA_COLS)
    valid = (kc >= win0) & (kc < win0 + NA_COLS)
    mask = np.tile(np.where(valid, 0.0, -np.inf).astype(np.float32), (2, NA_ROWS))
    shape = (h // 2, NA_ROWS, 2 * GRID_W, NA_ROWS * GRID_W)
    return pl.pallas_call(
        _nbr_bias_kernel,
        grid=(h // 2,),
        in_specs=[pl.BlockSpec((None, 2 * ROW_OFFSETS_PAD, V7X_LANES), lambda p: (p, 0, 0)),
                  _resident(mask.shape)],
        out_specs=pl.BlockSpec((None,) + shape[1:], lambda p: (p, 0, 0, 0)),
        out_shape=jax.ShapeDtypeStruct(shape, F32),
        compiler_params=_params(1, 32 * 1024 * 1024),
        name="nbr_bias",
    )(base, jnp.asarray(mask))


def _merge_kernel(x_ref, sh_ref, sc_ref, g1_ref, ng_ref, ya_ref, yb_ref, yc_ref, yd_ref,
                  wg_ref, wb_ref, wo_ref, o_ref):
    x = x_ref[...]
    d = x.shape[-1]
    hb = _norm_modulate(x, ng_ref[...], sh_ref[...], sc_ref[...])
    acc = None
    for i, y_ref in enumerate((ya_ref, yb_ref, yc_ref, yd_ref)):
        gate = _dot(hb, wg_ref[:, i * d:(i + 1) * d])
        term = jax.nn.sigmoid(gate) * _dot(y_ref[...], wb_ref[i * SEG_W:(i + 1) * SEG_W, :])
        acc = term if acc is None else acc + term
    o_ref[...] = x + g1_ref[...] * _dot(acc.astype(BF16), wo_ref[...])


def _merge(x, shift, scale, g1, norm_g, ys, w_gate, w_branch, w_out, layer):
    b, l, d = x.shape
    tm = min(ROW_TILE, l)
    batched = shift.shape[0] != 1
    mod_map = (lambda bi, i: (bi, 0, 0)) if batched else (lambda bi, i: (0, 0, 0))
    x_tile = pl.BlockSpec((None, tm, d), lambda bi, i: (bi, i, 0))
    y_tile = pl.BlockSpec((None, tm, SEG_W), lambda bi, i: (bi, i, 0))
    mod = pl.BlockSpec((None, 1, d), mod_map)
    weights = sum(math.prod(w.shape[1:]) for w in (w_gate, w_branch, w_out)) * 2
    vmem = weights + 2 * (2 * tm * d * 4 + 4 * tm * SEG_W * 2) + 8 * tm * d * 4 + (8 << 20)
    return pl.pallas_call(
        _merge_kernel,
        grid=(b, l // tm),
        in_specs=[x_tile, mod, mod, mod, _resident((1, d)), y_tile, y_tile, y_tile, y_tile,
                  _resident(w_gate.shape[1:], layer), _resident(w_branch.shape[1:], layer),
                  _resident(w_out.shape[1:], layer)],
        out_specs=x_tile,
        out_shape=jax.ShapeDtypeStruct(x.shape, F32),
        compiler_params=_params(2, vmem),
        name="merge",
    )(x, shift, scale, g1, norm_g, *ys, w_gate, w_branch, w_out)


def _ffn_kernel(x_ref, sh_ref, sc_ref, g2_ref, ng_ref, wgu_ref, wd_ref, o_ref, u_ref):
    x = x_ref[...]
    hb = _norm_modulate(x, ng_ref[...], sh_ref[...], sc_ref[...])
    hidden = wd_ref.shape[0]
    for c in range(hidden // FFN_CHUNK):
        cols = slice(c * FFN_CHUNK, (c + 1) * FFN_CHUNK)
        a = _dot(hb, wgu_ref[:, cols])
        bb = _dot(hb, wgu_ref[:, hidden + c * FFN_CHUNK:hidden + (c + 1) * FFN_CHUNK])
        u_ref[:, cols] = (a * jax.nn.sigmoid(a) * bb).astype(BF16)
    o_ref[...] = x + g2_ref[...] * _dot(u_ref[...], wd_ref[...])


def _ffn(x, shift, scale, g2, norm_g, w_gu, w_down, layer):
    b, l, d = x.shape
    tm = min(ROW_TILE, l)
    hidden = w_down.shape[1]
    batched = shift.shape[0] != 1
    mod_map = (lambda bi, i: (bi, 0, 0)) if batched else (lambda bi, i: (0, 0, 0))
    x_tile = pl.BlockSpec((None, tm, d), lambda bi, i: (bi, i, 0))
    mod = pl.BlockSpec((None, 1, d), mod_map)
    vmem = (math.prod(w_gu.shape[1:]) + math.prod(w_down.shape[1:])) * 2 + 4 * tm * d * 4 + tm * hidden * 2 + 8 * tm * d * 4 + (8 << 20)
    return pl.pallas_call(
        _ffn_kernel,
        grid=(b, l // tm),
        in_specs=[x_tile, mod, mod, mod, _resident((1, d)),
                  _resident(w_gu.shape[1:], layer), _resident(w_down.shape[1:], layer)],
        out_specs=x_tile,
        out_shape=jax.ShapeDtypeStruct(x.shape, F32),
        scratch_shapes=[pltpu.VMEM((tm, hidden), BF16)],
        compiler_params=_params(2, vmem),
        name="ffn",
    )(x, shift, scale, g2, norm_g, w_gu, w_down)


def _rope_tables(n_tok):
    nf = A_DH // 4
    t = np.arange(n_tok)
    row = (t // GRID_W).astype(np.float32)
    col = (t % GRID_W).astype(np.float32)
    inv = np.float32(ROPE_THETA) ** (-np.arange(nf, dtype=np.float32) / np.float32(nf))
    ar = row[:, None] * inv
    ac = col[:, None] * inv
    ang = np.concatenate([ar, ar, ac, ac], axis=-1).astype(np.float64)
    cos, sin = np.cos(ang), np.sin(ang)
    quarter = (np.arange(A_DH) // nf) % 2
    sin_up = np.where(quarter == 0, -sin, 0.0)
    sin_dn = np.where(quarter == 1, sin, 0.0)
    rep = V7X_LANES // A_DH
    return tuple(jnp.asarray(np.tile(a, (1, rep)), dtype=F32) for a in (cos, sin_up, sin_dn))


def kernel(x, c, ctx, c_ctx, w_mod, b_mod, norm1_g, w_in, a_qk_g, a_lambda, a_subln_g, b_pool_w,
           b_pool_s, c_qk_g, c_rpb, d_vn_g, d_ws, d_bs, w_branch, w_out, norm2_g, w_gu, w_down):
    b, s, d = x.shape
    depth = w_mod.shape[0]
    rope_tabs = _rope_tables(s)

    rows = -(-(b + 1) // 8) * 8
    c_all = jnp.zeros((rows, d), F32).at[:b].set(c).at[b].set(c_ctx)
    mod = _modulation(c_all, w_mod, b_mod)

    w_mix = w_in[:, :, :MIX_COLS].astype(BF16)
    w_gate = w_in[:, :, MIX_COLS:].astype(BF16)
    w_br = w_branch.astype(BF16)
    w_o = w_out.astype(BF16)
    w_gu_b = w_gu.astype(BF16)
    w_dn = w_down.astype(BF16)
    w_pool = b_pool_w.astype(BF16)
    w_s = d_ws.astype(BF16)

    for l in range(depth):
        last = l == depth - 1
        lam_init = 0.8 - 0.6 * math.exp(-0.3 * l)
        mx = [mod[l, :b, k * d:(k + 1) * d].reshape(b, 1, d) for k in range(6)]
        mc = [mod[l, b:b + 1, k * d:(k + 1) * d].reshape(1, 1, d) for k in range(6)]
        n1 = norm1_g[l].reshape(1, d)
        n2 = norm2_g[l].reshape(1, d)
        rep = SEG_W // A_DH
        qk_gains = jnp.stack([jnp.tile(a_qk_g[l, 0], rep), jnp.tile(a_qk_g[l, 1], rep),
                              jnp.tile(c_qk_g[l, 0], rep), jnp.tile(c_qk_g[l, 1], rep)])
        vn_g = d_vn_g[l].reshape(1, SEG_W)
        bs_full = jnp.repeat(d_bs[l].T, V7X_LANES, axis=1)
        bias_t = _nbr_bias_table(c_rpb[l])

        px = _in_proj(x, mx[0], mx[1], n1, w_mix, qk_gains, vn_g, w_s, bs_full, rope_tabs, MIXER_SEGS, l)
        pc = _in_proj(ctx, mc[0], mc[1], n1, w_mix, qk_gains, vn_g, w_s, bs_full, None,
                      CTX_KV_SEGS if last else MIXER_SEGS, l)

        ya = _diff_attn(px[SEG_AQ], [(px[SEG_AK], px[SEG_AV]), (pc[SEG_AK], pc[SEG_AV])],
                        a_lambda[l], a_subln_g[l], a_qk_g[l], lam_init, heads_per_step=1)
        yb = _pool(px[SEG_B], w_pool, b_pool_s[l], l)
        yc = _nbr_attn(px[SEG_CQ], px[SEG_CK], px[SEG_CV], pc[SEG_CK], pc[SEG_CV], bias_t, c_qk_g[l],
                       pairs_per_step=1)
        x = _merge(x, mx[0], mx[1], mx[2], n1, (ya, yb, yc, px[SEG_YD]), w_gate, w_br, w_o, l)
        x = _ffn(x, mx[3], mx[4], mx[5], n2, w_gu_b, w_dn, l)

        if not last:
            ya_c = _diff_attn(pc[SEG_AQ], [(pc[SEG_AK], pc[SEG_AV])], a_lambda[l], a_subln_g[l], a_qk_g[l], lam_init,
                              heads_per_step=A_HEADS)
            yb_c = _pool(pc[SEG_B], w_pool, b_pool_s[l], l)
            yc_c = _nbr_attn(pc[SEG_CQ], None, None, pc[SEG_CK], pc[SEG_CV], None, c_qk_g[l],
                             pairs_per_step=SEG_W // V7X_LANES)
            ctx = _merge(ctx, mc[0], mc[1], mc[2], n1, (ya_c, yb_c, yc_c, pc[SEG_YD]), w_gate, w_br, w_o, l)
            ctx = _ffn(ctx, mc[3], mc[4], mc[5], n2, w_gu_b, w_dn, l)
    return x
```

```python
import functools
import math

import jax
import jax.numpy as jnp
import numpy as np
from jax import lax
from jax.experimental import pallas as pl
from jax.experimental.pallas import tpu as pltpu

F32 = jnp.float32
BF16 = jnp.bfloat16

D_MODEL = 1024
GRID_W = 64
EPS = 1e-6
ROPE_THETA = 10000.0
A_HEADS = 4
A_DH = 64
POOL_WINDOWS = (2, 4, 8, 16)
C_DH = 64
NA_ROWS = 8
NA_COLS = 16
CHUNK = 128
N_BRANCH = 4
SEG_W = 512
N_SEG = 9
MIX_COLS = N_SEG * SEG_W
FFN_HIDDEN = 2816
SEG_AQ, SEG_AK, SEG_AV, SEG_B, SEG_CQ, SEG_CK, SEG_CV, SEG_DU, SEG_DV = range(N_SEG)
SEG_YD = N_SEG
SEG_DTYPES = (BF16, BF16, BF16, F32, BF16, BF16, BF16, F32, BF16, BF16)
MIXER_SEGS = (SEG_AQ, SEG_AK, SEG_AV, SEG_B, SEG_CQ, SEG_CK, SEG_CV, SEG_YD)
CTX_KV_SEGS = (SEG_AK, SEG_AV, SEG_CK, SEG_CV)

V7X_LANES = 128
V7X_VMEM_BYTES = 64 * 1024 * 1024
V7X_VMEM_LIMIT = V7X_VMEM_BYTES - 8 * 1024 * 1024

ROW_TILE = 1024
ATTN_Q_TILE = 256
ATTN_KEY_CHUNK = 256
SCORE_BOUND_SLACK = 1.02
DENOM_FLOOR = 1e-18
POOL_TILE = 256
POOL_HALO = 8
FFN_CHUNK = 256


def _params(n_axes, vmem_bytes):
    return pltpu.CompilerParams(
        dimension_semantics=("parallel",) * n_axes,
        vmem_limit_bytes=int(min(vmem_bytes, V7X_VMEM_LIMIT)))


def _resident(shape, layer=None):
    zeros = (0,) * len(shape)
    if layer is None:
        return pl.BlockSpec(shape, lambda *_: zeros, pipeline_mode=pl.Buffered(1))
    return pl.BlockSpec((None,) + tuple(shape), lambda *_: (layer,) + zeros, pipeline_mode=pl.Buffered(1))


def _dot(a, b):
    return jnp.dot(a, b, preferred_element_type=F32)


def _dot_nt(a, b):
    return lax.dot_general(a, b, (((1,), (1,)), ((), ())), preferred_element_type=F32)


def _dot_tn(a, b):
    return lax.dot_general(a, b, (((0,), (0,)), ((), ())), preferred_element_type=F32)


def _split_bf16(t):
    hi = t.astype(BF16)
    lo = (t - hi.astype(F32)).astype(BF16)
    return hi, lo


def _norm_modulate(x, gain, shift, scale):
    ms = jnp.mean(x * x, axis=-1, keepdims=True)
    n = x * lax.rsqrt(ms + EPS)
    return ((n * gain) * (1.0 + scale) + shift).astype(BF16)


def _mod_kernel(c_ref, w_ref, b_ref, o_ref):
    a = c_ref[...]
    a = a * jax.nn.sigmoid(a)
    a_hi, a_lo = _split_bf16(a)
    w_hi, w_lo = _split_bf16(w_ref[...])
    o_ref[...] = _dot(a_hi, w_hi) + _dot(a_hi, w_lo) + _dot(a_lo, w_hi) + b_ref[...]


def _modulation(c_all, w_mod, b_mod):
    depth, d, n = w_mod.shape
    rows = c_all.shape[0]
    tn = 768
    return pl.pallas_call(
        _mod_kernel,
        grid=(depth, n // tn),
        in_specs=[
            pl.BlockSpec((rows, d), lambda l, j: (0, 0)),
            pl.BlockSpec((None, d, tn), lambda l, j: (l, 0, j)),
            pl.BlockSpec((None, 1, tn), lambda l, j: (l, 0, j)),
        ],
        out_specs=pl.BlockSpec((None, rows, tn), lambda l, j: (l, 0, j)),
        out_shape=jax.ShapeDtypeStruct((depth, rows, n), F32),
        compiler_params=_params(2, 32 * 1024 * 1024),
        name="modulation",
    )(c_all, w_mod, b_mod.reshape(depth, 1, n))


def _half_block_rms(t, first_half, gain):
    sq = t * t
    s_first = jnp.sum(jnp.where(first_half, sq, 0.0), axis=-1, keepdims=True)
    s_second = jnp.sum(jnp.where(first_half, 0.0, sq), axis=-1, keepdims=True)
    ms = jnp.where(first_half, s_first, s_second) * (1.0 / A_DH)
    return t * lax.rsqrt(ms + EPS) * gain


def _rope_block(t, cos, sin_up, sin_dn):
    return (t * cos + pltpu.roll(t, V7X_LANES - 16, 1) * sin_up + pltpu.roll(t, 16, 1) * sin_dn)


def _in_proj_kernel(*refs, rope, out_segs):
    x_ref, sh_ref, sc_ref, ng_ref, w_ref, qkg_ref, vng_ref, ws_ref, bs_ref = refs[:9]
    pos = 9
    if rope:
        cos_ref, sup_ref, sdn_ref = refs[9:12]
        pos = 12
    outs = dict(zip(out_segs, refs[pos:pos + len(out_segs)]))
    gated = SEG_YD in outs
    needed = set(out_segs) | ({SEG_DU, SEG_DV} if gated else set())
    hb = _norm_modulate(x_ref[...], ng_ref[...], sh_ref[...], sc_ref[...])
    qk_row = {SEG_AQ: 0, SEG_AK: 1, SEG_CQ: 2, SEG_CK: 3}
    first_half = lax.broadcasted_iota(jnp.int32, (1, V7X_LANES), 1) < A_DH
    for seg in range(N_SEG):
        if seg not in needed:
            continue
        r = _dot(hb, w_ref[:, seg * SEG_W:(seg + 1) * SEG_W])
        if seg in qk_row:
            row = qk_row[seg]
            blocks = []
            for j in range(SEG_W // V7X_LANES):
                lanes = slice(j * V7X_LANES, (j + 1) * V7X_LANES)
                t = _half_block_rms(r[:, lanes], first_half, qkg_ref[row:row + 1, lanes])
                if rope and seg in (SEG_AQ, SEG_AK):
                    t = _rope_block(t, cos_ref[...], sup_ref[...], sdn_ref[...])
                blocks.append(t)
            r = jnp.concatenate(blocks, axis=1)
        elif seg == SEG_DU and gated:
            gate_u = r
        elif seg == SEG_DV:
            ms = jnp.mean(r * r, axis=-1, keepdims=True)
            r = r * lax.rsqrt(ms + EPS) * vng_ref[...]
            if gated:
                vb = r.astype(BF16)
                n_chunks = r.shape[0] // CHUNK
                for g in range(ws_ref.shape[0]):
                    lanes = slice(g * V7X_LANES, (g + 1) * V7X_LANES)
                    v_wide = jnp.concatenate([vb[n * CHUNK:(n + 1) * CHUNK, lanes] for n in range(n_chunks)], axis=1)
                    sv_wide = _dot(ws_ref[g], v_wide)
                    for n in range(n_chunks):
                        rows = slice(n * CHUNK, (n + 1) * CHUNK)
                        sv = sv_wide[:, n * V7X_LANES:(n + 1) * V7X_LANES] + bs_ref[:, lanes]
                        outs[SEG_YD][rows, lanes] = (gate_u[rows, lanes] * sv).astype(outs[SEG_YD].dtype)
        if seg in outs:
            outs[seg][...] = r.astype(outs[seg].dtype)


def _in_proj(x, shift, scale, norm_g, w_mix, qk_gains, vn_g, w_s, b_full, rope_tabs, out_segs, layer):
    b, l, d = x.shape
    tm = min(ROW_TILE, l)
    batched = shift.shape[0] != 1
    mod_map = (lambda bi, i: (bi, 0, 0)) if batched else (lambda bi, i: (0, 0, 0))
    rope = rope_tabs is not None
    in_specs = [
        pl.BlockSpec((None, tm, d), lambda bi, i: (bi, i, 0)),
        pl.BlockSpec((None, 1, d), mod_map),
        pl.BlockSpec((None, 1, d), mod_map),
        _resident((1, d)),
        _resident((d, MIX_COLS), layer),
        _resident((4, SEG_W)),
        _resident((1, SEG_W)),
        _resident(w_s.shape[1:], layer),
        _resident(b_full.shape),
    ]
    args = [x, shift, scale, norm_g, w_mix, qk_gains, vn_g, w_s, b_full]
    if rope:
        in_specs += [pl.BlockSpec((tm, V7X_LANES), lambda bi, i: (i, 0))] * 3
        args += list(rope_tabs)
    dtypes = [SEG_DTYPES[s] for s in out_segs]
    out_bytes = sum(tm * SEG_W * jnp.dtype(t).itemsize for t in dtypes)
    vmem = d * MIX_COLS * 2 + 2 * (tm * d * 4 + out_bytes) + 8 * tm * SEG_W * 4 + (8 << 20)
    outs = pl.pallas_call(
        functools.partial(_in_proj_kernel, rope=rope, out_segs=tuple(out_segs)),
        grid=(b, l // tm),
        in_specs=in_specs,
        out_specs=[pl.BlockSpec((None, tm, SEG_W), lambda bi, i: (bi, i, 0))] * len(out_segs),
        out_shape=[jax.ShapeDtypeStruct((b, l, SEG_W), t) for t in dtypes],
        compiler_params=_params(2, vmem),
        name="in_proj_rope" if rope else "in_proj",
    )(*args)
    return dict(zip(out_segs, outs))


def _diff_attn_kernel(*refs, n_parts, lq, tq, heads, lam_init):
    al_ref, sg_ref, qkg_ref, q_ref = refs[:4]
    kv = refs[4:4 + 2 * n_parts]
    o_ref, kcat_ref, vt_ref, s_ref, e_ref = refs[4 + 2 * n_parts:]
    al = al_ref[...]
    lam = (jnp.exp(jnp.sum(al[0:1] * al[1:2], axis=-1, keepdims=True))
           - jnp.exp(jnp.sum(al[2:3] * al[3:4], axis=-1, keepdims=True)) + lam_init)
    lane = lax.broadcasted_iota(jnp.int32, (1, V7X_LANES), 1)
    scale = A_DH ** -0.5
    masks = (jnp.where(lane < A_DH, scale, 0.0).astype(BF16),
             jnp.where(lane >= A_DH, scale, 0.0).astype(BF16))
    sub_gain = sg_ref[...] * (1.0 - lam_init)
    gains = jnp.max(jnp.abs(qkg_ref[...]), axis=-1, keepdims=True)
    bound = gains[0:1] * gains[1:2] * (A_DH * scale * SCORE_BOUND_SLACK)
    hw = 2 * A_DH
    n_keys = kcat_ref.shape[0]
    kc = math.gcd(n_keys, ATTN_KEY_CHUNK)
    chunks = range(0, n_keys, kc)
    n_tiles = lq // tq
    vt_ref[hw:, :] = jnp.ones((vt_ref.shape[0] - hw, n_keys), BF16)

    def finish(rows, cols, acc):
        o = (acc[0][:hw] * (1.0 / acc[0][hw:hw + 1]) - acc[1][:hw] * (lam / acc[1][hw:hw + 1])).T
        ms = jnp.mean(o * o, axis=-1, keepdims=True)
        o_ref[rows, cols] = (o * lax.rsqrt(ms + EPS) * sub_gain).astype(o_ref.dtype)

    def bounded_tile(cols, slot, r0):
        qc = q_ref[r0:r0 + tq, cols]
        acc = []
        for c, mask in enumerate(masks):
            qm = qc * mask
            for k0 in chunks:
                s = _dot_nt(kcat_ref[k0:k0 + kc, :], qm)
                e_ref[slot, c, k0:k0 + kc, :] = jnp.exp(s - bound).astype(BF16)
            acc.append(_dot(vt_ref[...], e_ref[slot, c]))
        finish(slice(r0, r0 + tq), cols, acc)
        return jnp.minimum(acc[0][hw:hw + 1], acc[1][hw:hw + 1])

    def exact_tile(cols, r0):
        qc = q_ref[pl.ds(r0, tq), cols]
        acc = []
        for c, mask in enumerate(masks):
            qm = qc * mask
            m8 = None
            for k0 in chunks:
                s = _dot_nt(kcat_ref[k0:k0 + kc, :], qm)
                s_ref[c, k0:k0 + kc, :] = s
                cm = jnp.max(s.reshape(kc // 8, 8, tq), axis=0)
                m8 = cm if m8 is None else jnp.maximum(m8, cm)
            m = jnp.max(m8, axis=0, keepdims=True)
            for k0 in chunks:
                rows = slice(k0, k0 + kc)
                e_ref[0, c, rows, :] = jnp.exp(s_ref[c, rows, :] - m).astype(BF16)
            acc.append(_dot(vt_ref[...], e_ref[0, c]))
        finish(pl.ds(r0, tq), cols, acc)

    def load_head(cols):
        off = 0
        for p in range(n_parts):
            n = kv[2 * p].shape[0]
            kcat_ref[off:off + n, :] = kv[2 * p][:, cols]
            vt_ref[0:hw, off:off + n] = kv[2 * p + 1][:, cols].astype(F32).T.astype(BF16)
            off += n

    for h in range(heads):
        cols = slice(h * hw, (h + 1) * hw)
        load_head(cols)
        if n_tiles == 1:
            exact_tile(cols, 0)
            continue
        smallest = None
        for j in range(n_tiles):
            denom = bounded_tile(cols, j % 2, j * tq)
            smallest = denom if smallest is None else jnp.minimum(smallest, denom)

        @pl.when(jnp.min(smallest) < DENOM_FLOOR)
        def _():
            def redo(j, carry):
                exact_tile(cols, pl.multiple_of(j * tq, tq))
                return carry
            lax.fori_loop(0, n_tiles, redo, 0)


def _diff_attn(q, kv_parts, a_lambda, subln_g, qk_g, lam_init, heads_per_step):
    b, lq, _ = q.shape
    tq = min(ATTN_Q_TILE, lq)
    hw = 2 * A_DH
    wdt = heads_per_step * hw
    head_spec = lambda n: pl.BlockSpec((None, n, wdt), lambda bi, h: (bi, 0, h))
    in_specs = [_resident(a_lambda.shape), _resident((1, hw)), _resident(qk_g.shape), head_spec(lq)]
    args = [a_lambda, subln_g.reshape(1, hw), qk_g, q]
    lk = 0
    for k, v in kv_parts:
        in_specs += [head_spec(k.shape[1]), head_spec(v.shape[1])]
        args += [k, v]
        lk += k.shape[1]
    ones_rows = 16
    scratch = [pltpu.VMEM((lk, hw), BF16), pltpu.VMEM((hw + ones_rows, lk), BF16),
               pltpu.VMEM((2, lk, tq), F32), pltpu.VMEM((2, 2, lk, tq), BF16)]
    vmem = (2 * (2 * lq + 4 * lk) * wdt * 2 + 2 * lk * (hw + ones_rows) * 2 + 2 * lk * tq * 8
            + 6 * ATTN_KEY_CHUNK * tq * 4 + (8 << 20))
    return pl.pallas_call(
        functools.partial(_diff_attn_kernel, n_parts=len(kv_parts), lq=lq, tq=tq,
                          heads=heads_per_step, lam_init=lam_init),
        grid=(b, A_HEADS // heads_per_step),
        in_specs=in_specs,
        out_specs=head_spec(lq),
        out_shape=jax.ShapeDtypeStruct((b, lq, A_HEADS * hw), BF16),
        scratch_shapes=scratch,
        compiler_params=_params(2, vmem),
        name="diff_attn",
    )(*args)


def _tree_sum(terms):
    while len(terms) > 1:
        terms = [terms[i] + terms[i + 1] for i in range(0, len(terms), 2)]
    return terms[0]


def _pool_kernel(p_ref, w_ref, s_ref, o_ref, pad_ref, *, l, tp):
    wdt = p_ref.shape[-1]
    pad_ref[0:POOL_HALO, :] = jnp.zeros((POOL_HALO, wdt), F32)
    pad_ref[POOL_HALO + l:, :] = jnp.zeros((POOL_HALO, wdt), F32)
    pad_ref[POOL_HALO:POOL_HALO + l, :] = p_ref[...]
    n_tiles = l // tp
    for ti in range(n_tiles):
        t0 = ti * tp
        edge = ti == 0 or ti == n_tiles - 1
        if edge:
            t_row = t0 + lax.broadcasted_iota(jnp.int32, (tp, V7X_LANES), 0)
        for g, w in enumerate(POOL_WINDOWS):
            lanes = slice(g * V7X_LANES, (g + 1) * V7X_LANES)
            shifted = [pad_ref[POOL_HALO + t0 + j:POOL_HALO + t0 + j + tp, lanes]
                       for j in range(-(w // 2), w // 2)]
            centre = shifted[w // 2]
            total = _tree_sum(shifted)
            if edge:
                cnt = (jnp.minimum(t_row + w // 2, l) - jnp.maximum(t_row - w // 2, 0)).astype(F32)
                mean = total / cnt
            else:
                mean = total * (1.0 / w)
            y = _dot((mean - centre).astype(BF16), w_ref[g]) * s_ref[:, lanes]
            o_ref[t0:t0 + tp, lanes] = y.astype(o_ref.dtype)


def _pool(p, w_pool, s_pool, layer):
    b, l, wdt = p.shape
    tp = min(POOL_TILE, l)
    seq = pl.BlockSpec((None, l, wdt), lambda bi: (bi, 0, 0))
    return pl.pallas_call(
        functools.partial(_pool_kernel, l=l, tp=tp),
        grid=(b,),
        in_specs=[seq, _resident(w_pool.shape[1:], layer), _resident((1, wdt))],
        out_specs=seq,
        out_shape=jax.ShapeDtypeStruct((b, l, wdt), BF16),
        scratch_shapes=[pltpu.VMEM((l + 2 * POOL_HALO, wdt), F32)],
        compiler_params=_params(1, 5 * l * wdt * 4 + (16 << 20)),
        name="pool",
    )(p, w_pool, s_pool.reshape(1, wdt))


def _nbr_attn_kernel(*refs, n_rows, pairs, local):
    if local:
        qkg_ref, q_ref, k_ref, v_ref, kc_ref, vc_ref, bias_ref, o_ref = refs
    else:
        qkg_ref, q_ref, kc_ref, vc_ref, o_ref = refs
    gains = jnp.max(jnp.abs(qkg_ref[...]), axis=-1, keepdims=True)
    bound = gains[0:1] * gains[1:2] * (C_DH ** 0.5 * SCORE_BOUND_SLACK)
    if local:
        bmax = jnp.max(jnp.max(jnp.max(bias_ref[...], axis=0), axis=0, keepdims=True), axis=1, keepdims=True)
        bound = bound + jnp.maximum(bmax, 0.0)
    lane = lax.broadcasted_iota(jnp.int32, (1, V7X_LANES), 1)
    scale = C_DH ** -0.5
    m_first = jnp.where(lane < C_DH, scale, 0.0).astype(BF16)
    m_second = jnp.where(lane >= C_DH, scale, 0.0).astype(BF16)
    lane_out = lax.broadcasted_iota(jnp.int32, (GRID_W, V7X_LANES), 1)
    n_loc = NA_ROWS * GRID_W
    hq = 2 * GRID_W
    with_ones = lambda v: jnp.concatenate([v, jnp.ones_like(v)], axis=1)
    first_key_row = [min(max(r - NA_ROWS // 2, 0), n_rows - NA_ROWS) for r in range(n_rows)]

    def attend(cols, shift):
        blocks = []
        for r in range(n_rows):
            qr = q_ref[r * GRID_W:(r + 1) * GRID_W, cols]
            blocks += [qr * m_first, qr * m_second]
        qbd = jnp.concatenate(blocks, axis=0)
        s_ctx = _dot_nt(qbd, kc_ref[:, cols])
        if local:
            s_loc = jnp.concatenate(
                [_dot_nt(qbd[r * hq:(r + 1) * hq], k_ref[rs * GRID_W:rs * GRID_W + n_loc, cols])
                 + bias_ref[rs - r + NA_ROWS - 1] for r, rs in enumerate(first_key_row)], axis=0)
        if shift is None:
            shift = jnp.max(s_ctx, axis=-1, keepdims=True)
            if local:
                shift = jnp.maximum(shift, jnp.max(s_loc, axis=-1, keepdims=True))
        e_ctx = jnp.exp(s_ctx - shift)
        full = _dot(e_ctx.astype(BF16), with_ones(vc_ref[:, cols]))
        if local:
            e_loc = jnp.exp(s_loc - shift).astype(BF16)
            full = full + jnp.concatenate(
                [_dot(e_loc[r * hq:(r + 1) * hq], with_ones(v_ref[rs * GRID_W:rs * GRID_W + n_loc, cols]))
                 for r, rs in enumerate(first_key_row)], axis=0)
        denom = full[:, V7X_LANES:V7X_LANES + 1]
        full = full[:, :V7X_LANES] * (1.0 / denom)
        out = jnp.concatenate(
            [jnp.where(lane_out < C_DH, full[r * hq:r * hq + GRID_W], full[r * hq + GRID_W:(r + 1) * hq])
             for r in range(n_rows)], axis=0)
        o_ref[:, cols] = out.astype(o_ref.dtype)
        return jnp.min(denom, axis=0, keepdims=True)

    pair_cols = [slice(p * V7X_LANES, (p + 1) * V7X_LANES) for p in range(pairs)]
    smallest = functools.reduce(jnp.minimum, [attend(cols, bound) for cols in pair_cols])

    @pl.when(jnp.min(smallest) < DENOM_FLOOR)
    def _():
        for cols in pair_cols:
            attend(cols, None)


def _nbr_attn(q, k, v, kc, vc, bias_t, qk_g, pairs_per_step):
    b, lq, wdt = q.shape
    lc = kc.shape[1]
    n_rows = lq // GRID_W
    local = k is not None
    blk = pairs_per_step * V7X_LANES
    pair = lambda n: pl.BlockSpec((None, n, blk), lambda h, bi: (bi, 0, h))
    if local:
        assert pairs_per_step == 1
        in_specs = [_resident(qk_g.shape), pair(lq), pair(lq), pair(lq), pair(lc), pair(lc),
                    pl.BlockSpec((None,) + bias_t.shape[1:], lambda h, bi: (h, 0, 0, 0))]
        args = [qk_g, q, k, v, kc, vc, bias_t]
    else:
        in_specs = [_resident(qk_g.shape), pair(lq), pair(lc), pair(lc)]
        args = [qk_g, q, kc, vc]
    return pl.pallas_call(
        functools.partial(_nbr_attn_kernel, n_rows=n_rows, pairs=pairs_per_step, local=local),
        grid=(wdt // blk, b),
        in_specs=in_specs,
        out_specs=pair(lq),
        out_shape=jax.ShapeDtypeStruct((b, lq, wdt), BF16),
        compiler_params=_params(2, V7X_VMEM_LIMIT),
        name="nbr_attn" if local else "ctx_attn",
    )(*args)


N_ROW_OFFSETS = 2 * NA_ROWS - 1
ROW_OFFSETS_PAD = 16


def _nbr_bias_kernel(base_ref, mask_ref, o_ref):
    lane = lax.broadcasted_iota(jnp.int32, (GRID_W, V7X_LANES), 1)
    first_half = lane < GRID_W
    blocks = []
    for h in range(2):
        per_offset = []
        for dr in range(N_ROW_OFFSETS):
            row = base_ref[h * ROW_OFFSETS_PAD + dr:h * ROW_OFFSETS_PAD + dr + 1, :]
            low = pltpu.roll(jnp.broadcast_to(row, (GRID_W, V7X_LANES)), 0, 1, stride=1, stride_axis=0)
            per_offset.append((low, pltpu.roll(low, GRID_W, 1)))
        blocks.append(per_offset)
    for d0 in range(NA_ROWS):
        for h in range(2):
            rows = slice(h * GRID_W, (h + 1) * GRID_W)
            for c in range(NA_ROWS // 2):
                lanes = slice(c * V7X_LANES, (c + 1) * V7X_LANES)
                blk = jnp.where(first_half, blocks[h][d0 + 2 * c][0], blocks[h][d0 + 2 * c + 1][1])
                o_ref[d0, rows, lanes] = blk + mask_ref[rows, lanes]


def _nbr_bias_table(rpb):
    h, n_dr, n_dc = rpb.shape
    base = jnp.roll(jnp.pad(rpb, ((0, 0), (0, ROW_OFFSETS_PAD - n_dr), (0, V7X_LANES - n_dc))),
                    -(NA_COLS - 1), axis=-1).reshape(h // 2, 2 * ROW_OFFSETS_PAD, V7X_LANES)
    qc = np.arange(GRID_W)[:, None]
    kc = np.arange(GRID_W)[None, :]
    win0 = np.clip(qc - NA_COLS // 2, 0, GRID_W - NA_COLS)
    valid = (kc >= win0) & (kc < win0 + NA_COLS)
    mask = np.tile(np.where(valid, 0.0, -np.inf).astype(np.float32), (2, NA_ROWS))
    shape = (h // 2, NA_ROWS, 2 * GRID_W, NA_ROWS * GRID_W)
    return pl.pallas_call(
        _nbr_bias_kernel,
        grid=(h // 2,),
        in_specs=[pl.BlockSpec((None, 2 * ROW_OFFSETS_PAD, V7X_LANES), lambda p: (p, 0, 0)),
                  _resident(mask.shape)],
        out_specs=pl.BlockSpec((None,) + shape[1:], lambda p: (p, 0, 0, 0)),
        out_shape=jax.ShapeDtypeStruct(shape, F32),
        compiler_params=_params(1, 32 * 1024 * 1024),
        name="nbr_bias",
    )(base, jnp.asarray(mask))


def _merge_kernel(x_ref, sh_ref, sc_ref, g1_ref, ng_ref, ya_ref, yb_ref, yc_ref, yd_ref,
                  wg_ref, wb_ref, wo_ref, o_ref):
    x = x_ref[...]
    d = x.shape[-1]
    hb = _norm_modulate(x, ng_ref[...], sh_ref[...], sc_ref[...])
    acc = None
    for i, y_ref in enumerate((ya_ref, yb_ref, yc_ref, yd_ref)):
        gate = _dot(hb, wg_ref[:, i * d:(i + 1) * d])
        term = jax.nn.sigmoid(gate) * _dot(y_ref[...], wb_ref[i * SEG_W:(i + 1) * SEG_W, :])
        acc = term if acc is None else acc + term
    o_ref[...] = x + g1_ref[...] * _dot(acc.astype(BF16), wo_ref[...])


def _merge(x, shift, scale, g1, norm_g, ys, w_gate, w_branch, w_out, layer):
    b, l, d = x.shape
    tm = min(ROW_TILE, l)
    batched = shift.shape[0] != 1
    mod_map = (lambda bi, i: (bi, 0, 0)) if batched else (lambda bi, i: (0, 0, 0))
    x_tile = pl.BlockSpec((None, tm, d), lambda bi, i: (bi, i, 0))
    y_tile = pl.BlockSpec((None, tm, SEG_W), lambda bi, i: (bi, i, 0))
    mod = pl.BlockSpec((None, 1, d), mod_map)
    weights = sum(math.prod(w.shape[1:]) for w in (w_gate, w_branch, w_out)) * 2
    vmem = weights + 2 * (2 * tm * d * 4 + 4 * tm * SEG_W * 2) + 8 * tm * d * 4 + (8 << 20)
    return pl.pallas_call(
        _merge_kernel,
        grid=(b, l // tm),
        in_specs=[x_tile, mod, mod, mod, _resident((1, d)), y_tile, y_tile, y_tile, y_tile,
                  _resident(w_gate.shape[1:], layer), _resident(w_branch.shape[1:], layer),
                  _resident(w_out.shape[1:], layer)],
        out_specs=x_tile,
        out_shape=jax.ShapeDtypeStruct(x.shape, F32),
        compiler_params=_params(2, vmem),
        name="merge",
    )(x, shift, scale, g1, norm_g, *ys, w_gate, w_branch, w_out)


def _ffn_kernel(x_ref, sh_ref, sc_ref, g2_ref, ng_ref, wgu_ref, wd_ref, o_ref, u_ref):
    x = x_ref[...]
    hb = _norm_modulate(x, ng_ref[...], sh_ref[...], sc_ref[...])
    hidden = wd_ref.shape[0]
    for c in range(hidden // FFN_CHUNK):
        cols = slice(c * FFN_CHUNK, (c + 1) * FFN_CHUNK)
        a = _dot(hb, wgu_ref[:, cols])
        bb = _dot(hb, wgu_ref[:, hidden + c * FFN_CHUNK:hidden + (c + 1) * FFN_CHUNK])
        u_ref[:, cols] = (a * jax.nn.sigmoid(a) * bb).astype(BF16)
    o_ref[...] = x + g2_ref[...] * _dot(u_ref[...], wd_ref[...])


def _ffn(x, shift, scale, g2, norm_g, w_gu, w_down, layer):
    b, l, d = x.shape
    tm = min(ROW_TILE, l)
    hidden = w_down.shape[1]
    batched = shift.shape[0] != 1
    mod_map = (lambda bi, i: (bi, 0, 0)) if batched else (lambda bi, i: (0, 0, 0))
    x_tile = pl.BlockSpec((None, tm, d), lambda bi, i: (bi, i, 0))
    mod = pl.BlockSpec((None, 1, d), mod_map)
    vmem = (math.prod(w_gu.shape[1:]) + math.prod(w_down.shape[1:])) * 2 + 4 * tm * d * 4 + tm * hidden * 2 + 8 * tm * d * 4 + (8 << 20)
    return pl.pallas_call(
        _ffn_kernel,
        grid=(b, l // tm),
        in_specs=[x_tile, mod, mod, mod, _resident((1, d)),
                  _resident(w_gu.shape[1:], layer), _resident(w_down.shape[1:], layer)],
        out_specs=x_tile,
        out_shape=jax.ShapeDtypeStruct(x.shape, F32),
        scratch_shapes=[pltpu.VMEM((tm, hidden), BF16)],
        compiler_params=_params(2, vmem),
        name="ffn",
    )(x, shift, scale, g2, norm_g, w_gu, w_down)


def _rope_tables(n_tok):
    nf = A_DH // 4
    t = np.arange(n_tok)
    row = (t // GRID_W).astype(np.float32)
    col = (t % GRID_W).astype(np.float32)
    inv = np.float32(ROPE_THETA) ** (-np.arange(nf, dtype=np.float32) / np.float32(nf))
    ar = row[:, None] * inv
    ac = col[:, None] * inv
    ang = np.concatenate([ar, ar, ac, ac], axis=-1).astype(np.float64)
    cos, sin = np.cos(ang), np.sin(ang)
    quarter = (np.arange(A_DH) // nf) % 2
    sin_up = np.where(quarter == 0, -sin, 0.0)
    sin_dn = np.where(quarter == 1, sin, 0.0)
    rep = V7X_LANES // A_DH
    return tuple(jnp.asarray(np.tile(a, (1, rep)), dtype=F32) for a in (cos, sin_up, sin_dn))


def kernel(x, c, ctx, c_ctx, w_mod, b_mod, norm1_g, w_in, a_qk_g, a_lambda, a_subln_g, b_pool_w,
           b_pool_s, c_qk_g, c_rpb, d_vn_g, d_ws, d_bs, w_branch, w_out, norm2_g, w_gu, w_down):
    b, s, d = x.shape
    depth = w_mod.shape[0]
    rope_tabs = _rope_tables(s)

    rows = -(-(b + 1) // 8) * 8
    c_all = jnp.zeros((rows, d), F32).at[:b].set(c).at[b].set(c_ctx)
    mod = _modulation(c_all, w_mod, b_mod)

    w_mix = w_in[:, :, :MIX_COLS].astype(BF16)
    w_gate = w_in[:, :, MIX_COLS:].astype(BF16)
    w_br = w_branch.astype(BF16)
    w_o = w_out.astype(BF16)
    w_gu_b = w_gu.astype(BF16)
    w_dn = w_down.astype(BF16)
    w_pool = b_pool_w.astype(BF16)
    w_s = d_ws.astype(BF16)

    for l in range(depth):
        last = l == depth - 1
        lam_init = 0.8 - 0.6 * math.exp(-0.3 * l)
        mx = [mod[l, :b, k * d:(k + 1) * d].reshape(b, 1, d) for k in range(6)]
        mc = [mod[l, b:b + 1, k * d:(k + 1) * d].reshape(1, 1, d) for k in range(6)]
        n1 = norm1_g[l].reshape(1, d)
        n2 = norm2_g[l].reshape(1, d)
        rep = SEG_W // A_DH
        qk_gains = jnp.stack([jnp.tile(a_qk_g[l, 0], rep), jnp.tile(a_qk_g[l, 1], rep),
                              jnp.tile(c_qk_g[l, 0], rep), jnp.tile(c_qk_g[l, 1], rep)])
        vn_g = d_vn_g[l].reshape(1, SEG_W)
        bs_full = jnp.repeat(d_bs[l].T, V7X_LANES, axis=1)
        bias_t = _nbr_bias_table(c_rpb[l])

        px = _in_proj(x, mx[0], mx[1], n1, w_mix, qk_gains, vn_g, w_s, bs_full, rope_tabs, MIXER_SEGS, l)
        lc = ctx.shape[1]
        flat = lambda t: t.reshape(1, b * lc, t.shape[-1])
        pc = _in_proj(flat(ctx), mc[0], mc[1], n1, w_mix, qk_gains, vn_g, w_s, bs_full, None,
                      CTX_KV_SEGS if last else MIXER_SEGS, l)
        pc = {seg: t.reshape(b, lc, SEG_W) for seg, t in pc.items()}

        ya = _diff_attn(px[SEG_AQ], [(px[SEG_AK], px[SEG_AV]), (pc[SEG_AK], pc[SEG_AV])],
                        a_lambda[l], a_subln_g[l], a_qk_g[l], lam_init, heads_per_step=1)
        yb = _pool(px[SEG_B], w_pool, b_pool_s[l], l)
        yc = _nbr_attn(px[SEG_CQ], px[SEG_CK], px[SEG_CV], pc[SEG_CK], pc[SEG_CV], bias_t, c_qk_g[l],
                       pairs_per_step=1)
        x = _merge(x, mx[0], mx[1], mx[2], n1, (ya, yb, yc, px[SEG_YD]), w_gate, w_br, w_o, l)
        x = _ffn(x, mx[3], mx[4], mx[5], n2, w_gu_b, w_dn, l)

        if not last:
            ya_c = _diff_attn(pc[SEG_AQ], [(pc[SEG_AK], pc[SEG_AV])], a_lambda[l], a_subln_g[l], a_qk_g[l], lam_init,
                              heads_per_step=A_HEADS)
            yb_c = _pool(pc[SEG_B], w_pool, b_pool_s[l], l)
            yc_c = _nbr_attn(pc[SEG_CQ], None, None, pc[SEG_CK], pc[SEG_CV], None, c_qk_g[l],
                             pairs_per_step=SEG_W // V7X_LANES)
            ys_c = tuple(flat(t) for t in (ya_c, yb_c, yc_c, pc[SEG_YD]))
            ctx_f = _merge(flat(ctx), mc[0], mc[1], mc[2], n1, ys_c, w_gate, w_br, w_o, l)
            ctx = _ffn(ctx_f, mc[3], mc[4], mc[5], n2, w_gu_b, w_dn, l).reshape(b, lc, d)
    return x
```

```python
import functools
import math

import jax
import jax.numpy as jnp
import numpy as np
from jax import lax
from jax.experimental import pallas as pl
from jax.experimental.pallas import tpu as pltpu

F32 = jnp.float32
BF16 = jnp.bfloat16

D_MODEL = 1024
GRID_W = 64
EPS = 1e-6
ROPE_THETA = 10000.0
A_HEADS = 4
A_DH = 64
POOL_WINDOWS = (2, 4, 8, 16)
C_DH = 64
NA_ROWS = 8
NA_COLS = 16
CHUNK = 128
N_BRANCH = 4
SEG_W = 512
N_SEG = 9
MIX_COLS = N_SEG * SEG_W
FFN_HIDDEN = 2816
SEG_AQ, SEG_AK, SEG_AV, SEG_B, SEG_CQ, SEG_CK, SEG_CV, SEG_DU, SEG_DV = range(N_SEG)
SEG_YD = N_SEG
SEG_DTYPES = (BF16, BF16, BF16, F32, BF16, BF16, BF16, F32, BF16, BF16)
MIXER_SEGS = (SEG_AQ, SEG_AK, SEG_AV, SEG_B, SEG_CQ, SEG_CK, SEG_CV, SEG_YD)
CTX_KV_SEGS = (SEG_AK, SEG_AV, SEG_CK, SEG_CV)

V7X_LANES = 128
V7X_VMEM_BYTES = 64 * 1024 * 1024
V7X_VMEM_LIMIT = V7X_VMEM_BYTES - 8 * 1024 * 1024

ROW_TILE = 1024
ATTN_Q_TILE = 256
ATTN_KEY_CHUNK = 256
SCORE_BOUND_SLACK = 1.02
DENOM_FLOOR = 1e-18
POOL_TILE = 256
POOL_HALO = 8
FFN_CHUNK = 256


def _params(n_axes, vmem_bytes):
    return pltpu.CompilerParams(
        dimension_semantics=("parallel",) * n_axes,
        vmem_limit_bytes=int(min(vmem_bytes, V7X_VMEM_LIMIT)))


def _resident(shape, layer=None):
    zeros = (0,) * len(shape)
    if layer is None:
        return pl.BlockSpec(shape, lambda *_: zeros, pipeline_mode=pl.Buffered(1))
    return pl.BlockSpec((None,) + tuple(shape), lambda *_: (layer,) + zeros, pipeline_mode=pl.Buffered(1))


def _dot(a, b):
    return jnp.dot(a, b, preferred_element_type=F32)


def _dot_nt(a, b):
    return lax.dot_general(a, b, (((1,), (1,)), ((), ())), preferred_element_type=F32)


def _dot_tn(a, b):
    return lax.dot_general(a, b, (((0,), (0,)), ((), ())), preferred_element_type=F32)


def _split_bf16(t):
    hi = t.astype(BF16)
    lo = (t - hi.astype(F32)).astype(BF16)
    return hi, lo


def _norm_modulate(x, gain, shift, scale):
    ms = jnp.mean(x * x, axis=-1, keepdims=True)
    n = x * lax.rsqrt(ms + EPS)
    return ((n * gain) * (1.0 + scale) + shift).astype(BF16)


def _mod_kernel(c_ref, w_ref, b_ref, o_ref):
    a = c_ref[...]
    a = a * jax.nn.sigmoid(a)
    a_hi, a_lo = _split_bf16(a)
    w_hi, w_lo = _split_bf16(w_ref[...])
    o_ref[...] = _dot(a_hi, w_hi) + _dot(a_hi, w_lo) + _dot(a_lo, w_hi) + b_ref[...]


def _modulation(c_all, w_mod, b_mod):
    depth, d, n = w_mod.shape
    rows = c_all.shape[0]
    tn = 768
    return pl.pallas_call(
        _mod_kernel,
        grid=(depth, n // tn),
        in_specs=[
            pl.BlockSpec((rows, d), lambda l, j: (0, 0)),
            pl.BlockSpec((None, d, tn), lambda l, j: (l, 0, j)),
            pl.BlockSpec((None, 1, tn), lambda l, j: (l, 0, j)),
        ],
        out_specs=pl.BlockSpec((None, rows, tn), lambda l, j: (l, 0, j)),
        out_shape=jax.ShapeDtypeStruct((depth, rows, n), F32),
        compiler_params=_params(2, 32 * 1024 * 1024),
        name="modulation",
    )(c_all, w_mod, b_mod.reshape(depth, 1, n))


def _half_block_rms(t, first_half, gain):
    sq = t * t
    s_first = jnp.sum(jnp.where(first_half, sq, 0.0), axis=-1, keepdims=True)
    s_second = jnp.sum(jnp.where(first_half, 0.0, sq), axis=-1, keepdims=True)
    ms = jnp.where(first_half, s_first, s_second) * (1.0 / A_DH)
    return t * lax.rsqrt(ms + EPS) * gain


def _rope_block(t, cos, sin_up, sin_dn):
    return (t * cos + pltpu.roll(t, V7X_LANES - 16, 1) * sin_up + pltpu.roll(t, 16, 1) * sin_dn)


def _in_proj_kernel(*refs, rope, out_segs):
    x_ref, sh_ref, sc_ref, ng_ref, w_ref, qkg_ref, vng_ref, ws_ref, bs_ref = refs[:9]
    pos = 9
    if rope:
        cos_ref, sup_ref, sdn_ref = refs[9:12]
        pos = 12
    outs = dict(zip(out_segs, refs[pos:pos + len(out_segs)]))
    gated = SEG_YD in outs
    needed = set(out_segs) | ({SEG_DU, SEG_DV} if gated else set())
    hb = _norm_modulate(x_ref[...], ng_ref[...], sh_ref[...], sc_ref[...])
    qk_row = {SEG_AQ: 0, SEG_AK: 1, SEG_CQ: 2, SEG_CK: 3}
    first_half = lax.broadcasted_iota(jnp.int32, (1, V7X_LANES), 1) < A_DH
    for seg in range(N_SEG):
        if seg not in needed:
            continue
        r = _dot(hb, w_ref[:, seg * SEG_W:(seg + 1) * SEG_W])
        if seg in qk_row:
            row = qk_row[seg]
            blocks = []
            for j in range(SEG_W // V7X_LANES):
                lanes = slice(j * V7X_LANES, (j + 1) * V7X_LANES)
                t = _half_block_rms(r[:, lanes], first_half, qkg_ref[row:row + 1, lanes])
                if rope and seg in (SEG_AQ, SEG_AK):
                    t = _rope_block(t, cos_ref[...], sup_ref[...], sdn_ref[...])
                blocks.append(t)
            r = jnp.concatenate(blocks, axis=1)
        elif seg == SEG_DU and gated:
            gate_u = r
        elif seg == SEG_DV:
            ms = jnp.mean(r * r, axis=-1, keepdims=True)
            r = r * lax.rsqrt(ms + EPS) * vng_ref[...]
            if gated:
                vb = r.astype(BF16)
                n_chunks = r.shape[0] // CHUNK
                for g in range(ws_ref.shape[0]):
                    lanes = slice(g * V7X_LANES, (g + 1) * V7X_LANES)
                    v_wide = jnp.concatenate([vb[n * CHUNK:(n + 1) * CHUNK, lanes] for n in range(n_chunks)], axis=1)
                    sv_wide = _dot(ws_ref[g], v_wide)
                    for n in range(n_chunks):
                        rows = slice(n * CHUNK, (n + 1) * CHUNK)
                        sv = sv_wide[:, n * V7X_LANES:(n + 1) * V7X_LANES] + bs_ref[:, lanes]
                        outs[SEG_YD][rows, lanes] = (gate_u[rows, lanes] * sv).astype(outs[SEG_YD].dtype)
        if seg in outs:
            outs[seg][...] = r.astype(outs[seg].dtype)


def _in_proj(x, shift, scale, norm_g, w_mix, qk_gains, vn_g, w_s, b_full, rope_tabs, out_segs, layer):
    b, l, d = x.shape
    tm = min(ROW_TILE, l)
    batched = shift.shape[0] != 1
    mod_map = (lambda bi, i: (bi, 0, 0)) if batched else (lambda bi, i: (0, 0, 0))
    rope = rope_tabs is not None
    in_specs = [
        pl.BlockSpec((None, tm, d), lambda bi, i: (bi, i, 0)),
        pl.BlockSpec((None, 1, d), mod_map),
        pl.BlockSpec((None, 1, d), mod_map),
        _resident((1, d)),
        _resident((d, MIX_COLS), layer),
        _resident((4, SEG_W)),
        _resident((1, SEG_W)),
        _resident(w_s.shape[1:], layer),
        _resident(b_full.shape),
    ]
    args = [x, shift, scale, norm_g, w_mix, qk_gains, vn_g, w_s, b_full]
    if rope:
        in_specs += [pl.BlockSpec((tm, V7X_LANES), lambda bi, i: (i, 0))] * 3
        args += list(rope_tabs)
    dtypes = [SEG_DTYPES[s] for s in out_segs]
    out_bytes = sum(tm * SEG_W * jnp.dtype(t).itemsize for t in dtypes)
    vmem = d * MIX_COLS * 2 + 2 * (tm * d * 4 + out_bytes) + 8 * tm * SEG_W * 4 + (8 << 20)
    outs = pl.pallas_call(
        functools.partial(_in_proj_kernel, rope=rope, out_segs=tuple(out_segs)),
        grid=(b, l // tm),
        in_specs=in_specs,
        out_specs=[pl.BlockSpec((None, tm, SEG_W), lambda bi, i: (bi, i, 0))] * len(out_segs),
        out_shape=[jax.ShapeDtypeStruct((b, l, SEG_W), t) for t in dtypes],
        compiler_params=_params(2, vmem),
        name="in_proj_rope" if rope else "in_proj",
    )(*args)
    return dict(zip(out_segs, outs))


def _diff_attn_kernel(*refs, n_parts, lq, tq, heads, lam_init):
    al_ref, sg_ref, qkg_ref, q_ref = refs[:4]
    kv = refs[4:4 + 2 * n_parts]
    o_ref, kcat_ref, vt_ref, s_ref, e_ref = refs[4 + 2 * n_parts:]
    al = al_ref[...]
    lam = (jnp.exp(jnp.sum(al[0:1] * al[1:2], axis=-1, keepdims=True))
           - jnp.exp(jnp.sum(al[2:3] * al[3:4], axis=-1, keepdims=True)) + lam_init)
    lane = lax.broadcasted_iota(jnp.int32, (1, V7X_LANES), 1)
    scale = A_DH ** -0.5
    masks = (jnp.where(lane < A_DH, scale, 0.0).astype(BF16),
             jnp.where(lane >= A_DH, scale, 0.0).astype(BF16))
    sub_gain = sg_ref[...] * (1.0 - lam_init)
    gains = jnp.max(jnp.abs(qkg_ref[...]), axis=-1, keepdims=True)
    bound = gains[0:1] * gains[1:2] * (A_DH * scale * SCORE_BOUND_SLACK)
    hw = 2 * A_DH
    n_keys = kcat_ref.shape[0]
    kc = math.gcd(n_keys, ATTN_KEY_CHUNK)
    chunks = range(0, n_keys, kc)
    n_tiles = lq // tq
    vt_ref[hw:, :] = jnp.ones((vt_ref.shape[0] - hw, n_keys), BF16)

    def finish(rows, cols, acc):
        o = (acc[0][:hw] * (1.0 / acc[0][hw:hw + 1]) - acc[1][:hw] * (lam / acc[1][hw:hw + 1])).T
        ms = jnp.mean(o * o, axis=-1, keepdims=True)
        o_ref[rows, cols] = (o * lax.rsqrt(ms + EPS) * sub_gain).astype(o_ref.dtype)

    def bounded_tile(cols, slot, r0):
        qc = q_ref[r0:r0 + tq, cols]
        acc = []
        for c, mask in enumerate(masks):
            qm = qc * mask
            for k0 in chunks:
                s = _dot_nt(kcat_ref[k0:k0 + kc, :], qm)
                e_ref[slot, c, k0:k0 + kc, :] = jnp.exp(s - bound).astype(BF16)
            acc.append(_dot(vt_ref[...], e_ref[slot, c]))
        finish(slice(r0, r0 + tq), cols, acc)
        return jnp.minimum(acc[0][hw:hw + 1], acc[1][hw:hw + 1])

    def exact_tile(cols, r0):
        qc = q_ref[pl.ds(r0, tq), cols]
        acc = []
        for c, mask in enumerate(masks):
            qm = qc * mask
            m8 = None
            for k0 in chunks:
                s = _dot_nt(kcat_ref[k0:k0 + kc, :], qm)
                s_ref[c, k0:k0 + kc, :] = s
                cm = jnp.max(s.reshape(kc // 8, 8, tq), axis=0)
                m8 = cm if m8 is None else jnp.maximum(m8, cm)
            m = jnp.max(m8, axis=0, keepdims=True)
            for k0 in chunks:
                rows = slice(k0, k0 + kc)
                e_ref[0, c, rows, :] = jnp.exp(s_ref[c, rows, :] - m).astype(BF16)
            acc.append(_dot(vt_ref[...], e_ref[0, c]))
        finish(pl.ds(r0, tq), cols, acc)

    def load_head(cols):
        off = 0
        for p in range(n_parts):
            n = kv[2 * p].shape[0]
            kcat_ref[off:off + n, :] = kv[2 * p][:, cols]
            vt_ref[0:hw, off:off + n] = kv[2 * p + 1][:, cols].astype(F32).T.astype(BF16)
            off += n

    for h in range(heads):
        cols = slice(h * hw, (h + 1) * hw)
        load_head(cols)
        if n_tiles == 1:
            exact_tile(cols, 0)
            continue
        smallest = None
        for j in range(n_tiles):
            denom = bounded_tile(cols, j % 2, j * tq)
            smallest = denom if smallest is None else jnp.minimum(smallest, denom)

        @pl.when(jnp.min(smallest) < DENOM_FLOOR)
        def _():
            def redo(j, carry):
                exact_tile(cols, pl.multiple_of(j * tq, tq))
                return carry
            lax.fori_loop(0, n_tiles, redo, 0)


def _diff_attn(q, kv_parts, a_lambda, subln_g, qk_g, lam_init, heads_per_step):
    b, lq, _ = q.shape
    tq = min(ATTN_Q_TILE, lq)
    hw = 2 * A_DH
    wdt = heads_per_step * hw
    head_spec = lambda n: pl.BlockSpec((None, n, wdt), lambda bi, h: (bi, 0, h))
    in_specs = [_resident(a_lambda.shape), _resident((1, hw)), _resident(qk_g.shape), head_spec(lq)]
    args = [a_lambda, subln_g.reshape(1, hw), qk_g, q]
    lk = 0
    for k, v in kv_parts:
        in_specs += [head_spec(k.shape[1]), head_spec(v.shape[1])]
        args += [k, v]
        lk += k.shape[1]
    ones_rows = 16
    scratch = [pltpu.VMEM((lk, hw), BF16), pltpu.VMEM((hw + ones_rows, lk), BF16),
               pltpu.VMEM((2, lk, tq), F32), pltpu.VMEM((2, 2, lk, tq), BF16)]
    vmem = (2 * (2 * lq + 4 * lk) * wdt * 2 + 2 * lk * (hw + ones_rows) * 2 + 2 * lk * tq * 8
            + 6 * ATTN_KEY_CHUNK * tq * 4 + (8 << 20))
    return pl.pallas_call(
        functools.partial(_diff_attn_kernel, n_parts=len(kv_parts), lq=lq, tq=tq,
                          heads=heads_per_step, lam_init=lam_init),
        grid=(b, A_HEADS // heads_per_step),
        in_specs=in_specs,
        out_specs=head_spec(lq),
        out_shape=jax.ShapeDtypeStruct((b, lq, A_HEADS * hw), BF16),
        scratch_shapes=scratch,
        compiler_params=_params(2, vmem),
        name="diff_attn",
    )(*args)


def _tree_sum(terms):
    while len(terms) > 1:
        terms = [terms[i] + terms[i + 1] for i in range(0, len(terms), 2)]
    return terms[0]


def _pool_kernel(p_ref, w_ref, s_ref, o_ref, pad_ref, *, l, tp):
    wdt = p_ref.shape[-1]
    pad_ref[0:POOL_HALO, :] = jnp.zeros((POOL_HALO, wdt), F32)
    pad_ref[POOL_HALO + l:, :] = jnp.zeros((POOL_HALO, wdt), F32)
    pad_ref[POOL_HALO:POOL_HALO + l, :] = p_ref[...]
    n_tiles = l // tp
    for ti in range(n_tiles):
        t0 = ti * tp
        edge = ti == 0 or ti == n_tiles - 1
        if edge:
            t_row = t0 + lax.broadcasted_iota(jnp.int32, (tp, V7X_LANES), 0)
        for g, w in enumerate(POOL_WINDOWS):
            lanes = slice(g * V7X_LANES, (g + 1) * V7X_LANES)
            shifted = [pad_ref[POOL_HALO + t0 + j:POOL_HALO + t0 + j + tp, lanes]
                       for j in range(-(w // 2), w // 2)]
            centre = shifted[w // 2]
            total = _tree_sum(shifted)
            if edge:
                cnt = (jnp.minimum(t_row + w // 2, l) - jnp.maximum(t_row - w // 2, 0)).astype(F32)
                mean = total / cnt
            else:
                mean = total * (1.0 / w)
            y = _dot((mean - centre).astype(BF16), w_ref[g]) * s_ref[:, lanes]
            o_ref[t0:t0 + tp, lanes] = y.astype(o_ref.dtype)


def _pool(p, w_pool, s_pool, layer):
    b, l, wdt = p.shape
    tp = min(POOL_TILE, l)
    seq = pl.BlockSpec((None, l, wdt), lambda bi: (bi, 0, 0))
    return pl.pallas_call(
        functools.partial(_pool_kernel, l=l, tp=tp),
        grid=(b,),
        in_specs=[seq, _resident(w_pool.shape[1:], layer), _resident((1, wdt))],
        out_specs=seq,
        out_shape=jax.ShapeDtypeStruct((b, l, wdt), BF16),
        scratch_shapes=[pltpu.VMEM((l + 2 * POOL_HALO, wdt), F32)],
        compiler_params=_params(1, 5 * l * wdt * 4 + (16 << 20)),
        name="pool",
    )(p, w_pool, s_pool.reshape(1, wdt))


def _nbr_attn_kernel(*refs, n_rows, pairs, local):
    if local:
        qkg_ref, q_ref, k_ref, v_ref, kc_ref, vc_ref, bias_ref, o_ref = refs
    else:
        qkg_ref, q_ref, kc_ref, vc_ref, o_ref = refs
    gains = jnp.max(jnp.abs(qkg_ref[...]), axis=-1, keepdims=True)
    bound = gains[0:1] * gains[1:2] * (C_DH ** 0.5 * SCORE_BOUND_SLACK)
    if local:
        bmax = jnp.max(jnp.max(jnp.max(bias_ref[...], axis=0), axis=0, keepdims=True), axis=1, keepdims=True)
        bound = bound + jnp.maximum(bmax, 0.0)
    lane = lax.broadcasted_iota(jnp.int32, (1, V7X_LANES), 1)
    scale = C_DH ** -0.5
    m_first = jnp.where(lane < C_DH, scale, 0.0).astype(BF16)
    m_second = jnp.where(lane >= C_DH, scale, 0.0).astype(BF16)
    lane_out = lax.broadcasted_iota(jnp.int32, (GRID_W, V7X_LANES), 1)
    n_loc = NA_ROWS * GRID_W
    hq = 2 * GRID_W
    with_ones = lambda v: jnp.concatenate([v, jnp.ones_like(v)], axis=1)
    first_key_row = [min(max(r - NA_ROWS // 2, 0), n_rows - NA_ROWS) for r in range(n_rows)]

    def attend(cols, shift):
        blocks = []
        for r in range(n_rows):
            qr = q_ref[r * GRID_W:(r + 1) * GRID_W, cols]
            blocks += [qr * m_first, qr * m_second]
        qbd = jnp.concatenate(blocks, axis=0)
        s_ctx = _dot_nt(qbd, kc_ref[:, cols])
        if local:
            s_loc = jnp.concatenate(
                [_dot_nt(qbd[r * hq:(r + 1) * hq], k_ref[rs * GRID_W:rs * GRID_W + n_loc, cols])
                 + bias_ref[rs - r + NA_ROWS - 1] for r, rs in enumerate(first_key_row)], axis=0)
        if shift is None:
            shift = jnp.max(s_ctx, axis=-1, keepdims=True)
            if local:
                shift = jnp.maximum(shift, jnp.max(s_loc, axis=-1, keepdims=True))
        e_ctx = jnp.exp(s_ctx - shift)
        full = _dot(e_ctx.astype(BF16), with_ones(vc_ref[:, cols]))
        if local:
            e_loc = jnp.exp(s_loc - shift).astype(BF16)
            full = full + jnp.concatenate(
                [_dot(e_loc[r * hq:(r + 1) * hq], with_ones(v_ref[rs * GRID_W:rs * GRID_W + n_loc, cols]))
                 for r, rs in enumerate(first_key_row)], axis=0)
        denom = full[:, V7X_LANES:V7X_LANES + 1]
        full = full[:, :V7X_LANES] * (1.0 / denom)
        out = jnp.concatenate(
            [jnp.where(lane_out < C_DH, full[r * hq:r * hq + GRID_W], full[r * hq + GRID_W:(r + 1) * hq])
             for r in range(n_rows)], axis=0)
        o_ref[:, cols] = out.astype(o_ref.dtype)
        return jnp.min(denom, axis=0, keepdims=True)

    pair_cols = [slice(p * V7X_LANES, (p + 1) * V7X_LANES) for p in range(pairs)]
    smallest = functools.reduce(jnp.minimum, [attend(cols, bound) for cols in pair_cols])

    @pl.when(jnp.min(smallest) < DENOM_FLOOR)
    def _():
        for cols in pair_cols:
            attend(cols, None)


def _nbr_attn(q, k, v, kc, vc, bias_t, qk_g, pairs_per_step):
    b, lq, wdt = q.shape
    lc = kc.shape[1]
    n_rows = lq // GRID_W
    local = k is not None
    blk = pairs_per_step * V7X_LANES
    pair = lambda n: pl.BlockSpec((None, n, blk), lambda h, bi: (bi, 0, h))
    if local:
        assert pairs_per_step == 1
        in_specs = [_resident(qk_g.shape), pair(lq), pair(lq), pair(lq), pair(lc), pair(lc),
                    pl.BlockSpec((None,) + bias_t.shape[1:], lambda h, bi: (h, 0, 0, 0))]
        args = [qk_g, q, k, v, kc, vc, bias_t]
    else:
        in_specs = [_resident(qk_g.shape), pair(lq), pair(lc), pair(lc)]
        args = [qk_g, q, kc, vc]
    return pl.pallas_call(
        functools.partial(_nbr_attn_kernel, n_rows=n_rows, pairs=pairs_per_step, local=local),
        grid=(wdt // blk, b),
        in_specs=in_specs,
        out_specs=pair(lq),
        out_shape=jax.ShapeDtypeStruct((b, lq, wdt), BF16),
        compiler_params=_params(2, V7X_VMEM_LIMIT),
        name="nbr_attn" if local else "ctx_attn",
    )(*args)


N_ROW_OFFSETS = 2 * NA_ROWS - 1
ROW_OFFSETS_PAD = 16


def _nbr_bias_kernel(base_ref, mask_ref, o_ref):
    lane = lax.broadcasted_iota(jnp.int32, (GRID_W, V7X_LANES), 1)
    first_half = lane < GRID_W
    blocks = []
    for h in range(2):
        per_offset = []
        for dr in range(N_ROW_OFFSETS):
            row = base_ref[h * ROW_OFFSETS_PAD + dr:h * ROW_OFFSETS_PAD + dr + 1, :]
            low = pltpu.roll(jnp.broadcast_to(row, (GRID_W, V7X_LANES)), 0, 1, stride=1, stride_axis=0)
            per_offset.append((low, pltpu.roll(low, GRID_W, 1)))
        blocks.append(per_offset)
    for d0 in range(NA_ROWS):
        for h in range(2):
            rows = slice(h * GRID_W, (h + 1) * GRID_W)
            for c in range(NA_ROWS // 2):
                lanes = slice(c * V7X_LANES, (c + 1) * V7X_LANES)
                blk = jnp.where(first_half, blocks[h][d0 + 2 * c][0], blocks[h][d0 + 2 * c + 1][1])
                o_ref[d0, rows, lanes] = blk + mask_ref[rows, lanes]


def _nbr_bias_table(rpb):
    h, n_dr, n_dc = rpb.shape
    base = jnp.roll(jnp.pad(rpb, ((0, 0), (0, ROW_OFFSETS_PAD - n_dr), (0, V7X_LANES - n_dc))),
                    -(NA_COLS - 1), axis=-1).reshape(h // 2, 2 * ROW_OFFSETS_PAD, V7X_LANES)
    qc = np.arange(GRID_W)[:, None]
    kc = np.arange(GRID_W)[None, :]
    win0 = np.clip(qc - NA_COLS // 2, 0, GRID_W - NA_COLS)
    valid = (kc >= win0) & (kc < win0 + NA_COLS)
    mask = np.tile(np.where(valid, 0.0, -np.inf).astype(np.float32), (2, NA_ROWS))
    shape = (h // 2, NA_ROWS, 2 * GRID_W, NA_ROWS * GRID_W)
    return pl.pallas_call(
        _nbr_bias_kernel,
        grid=(h // 2,),
        in_specs=[pl.BlockSpec((None, 2 * ROW_OFFSETS_PAD, V7X_LANES), lambda p: (p, 0, 0)),
                  _resident(mask.shape)],
        out_specs=pl.BlockSpec((None,) + shape[1:], lambda p: (p, 0, 0, 0)),
        out_shape=jax.ShapeDtypeStruct(shape, F32),
        compiler_params=_params(1, 32 * 1024 * 1024),
        name="nbr_bias",
    )(base, jnp.asarray(mask))


def _merge_kernel(x_ref, sh_ref, sc_ref, g1_ref, ng_ref, ya_ref, yb_ref, yc_ref, yd_ref,
                  wg_ref, wb_ref, wo_ref, o_ref):
    x = x_ref[...]
    d = x.shape[-1]
    hb = _norm_modulate(x, ng_ref[...], sh_ref[...], sc_ref[...])
    acc = None
    for i, y_ref in enumerate((ya_ref, yb_ref, yc_ref, yd_ref)):
        gate = _dot(hb, wg_ref[:, i * d:(i + 1) * d])
        term = jax.nn.sigmoid(gate) * _dot(y_ref[...], wb_ref[i * SEG_W:(i + 1) * SEG_W, :])
        acc = term if acc is None else acc + term
    o_ref[...] = x + g1_ref[...] * _dot(acc.astype(BF16), wo_ref[...])


def _merge(x, shift, scale, g1, norm_g, ys, w_gate, w_branch, w_out, layer):
    b, l, d = x.shape
    tm = min(ROW_TILE, l)
    batched = shift.shape[0] != 1
    mod_map = (lambda bi, i: (bi, 0, 0)) if batched else (lambda bi, i: (0, 0, 0))
    x_tile = pl.BlockSpec((None, tm, d), lambda bi, i: (bi, i, 0))
    y_tile = pl.BlockSpec((None, tm, SEG_W), lambda bi, i: (bi, i, 0))
    mod = pl.BlockSpec((None, 1, d), mod_map)
    weights = sum(math.prod(w.shape[1:]) for w in (w_gate, w_branch, w_out)) * 2
    vmem = weights + 2 * (2 * tm * d * 4 + 4 * tm * SEG_W * 2) + 8 * tm * d * 4 + (8 << 20)
    return pl.pallas_call(
        _merge_kernel,
        grid=(b, l // tm),
        in_specs=[x_tile, mod, mod, mod, _resident((1, d)), y_tile, y_tile, y_tile, y_tile,
                  _resident(w_gate.shape[1:], layer), _resident(w_branch.shape[1:], layer),
                  _resident(w_out.shape[1:], layer)],
        out_specs=x_tile,
        out_shape=jax.ShapeDtypeStruct(x.shape, F32),
        compiler_params=_params(2, vmem),
        name="merge",
    )(x, shift, scale, g1, norm_g, *ys, w_gate, w_branch, w_out)


def _ffn_kernel(x_ref, sh_ref, sc_ref, g2_ref, ng_ref, wgu_ref, wd_ref, o_ref, u_ref):
    x = x_ref[...]
    hb = _norm_modulate(x, ng_ref[...], sh_ref[...], sc_ref[...])
    hidden = wd_ref.shape[0]
    for c in range(hidden // FFN_CHUNK):
        cols = slice(c * FFN_CHUNK, (c + 1) * FFN_CHUNK)
        a = _dot(hb, wgu_ref[:, cols])
        bb = _dot(hb, wgu_ref[:, hidden + c * FFN_CHUNK:hidden + (c + 1) * FFN_CHUNK])
        u_ref[:, cols] = (a * jax.nn.sigmoid(a) * bb).astype(BF16)
    o_ref[...] = x + g2_ref[...] * _dot(u_ref[...], wd_ref[...])


def _ffn(x, shift, scale, g2, norm_g, w_gu, w_down, layer):
    b, l, d = x.shape
    tm = min(ROW_TILE, l)
    hidden = w_down.shape[1]
    batched = shift.shape[0] != 1
    mod_map = (lambda bi, i: (bi, 0, 0)) if batched else (lambda bi, i: (0, 0, 0))
    x_tile = pl.BlockSpec((None, tm, d), lambda bi, i: (bi, i, 0))
    mod = pl.BlockSpec((None, 1, d), mod_map)
    vmem = (math.prod(w_gu.shape[1:]) + math.prod(w_down.shape[1:])) * 2 + 4 * tm * d * 4 + tm * hidden * 2 + 8 * tm * d * 4 + (8 << 20)
    return pl.pallas_call(
        _ffn_kernel,
        grid=(b, l // tm),
        in_specs=[x_tile, mod, mod, mod, _resident((1, d)),
                  _resident(w_gu.shape[1:], layer), _resident(w_down.shape[1:], layer)],
        out_specs=x_tile,
        out_shape=jax.ShapeDtypeStruct(x.shape, F32),
        scratch_shapes=[pltpu.VMEM((tm, hidden), BF16)],
        compiler_params=_params(2, vmem),
        name="ffn",
    )(x, shift, scale, g2, norm_g, w_gu, w_down)


def _rope_tables(n_tok):
    nf = A_DH // 4
    t = np.arange(n_tok)
    row = (t // GRID_W).astype(np.float32)
    col = (t % GRID_W).astype(np.float32)
    inv = np.float32(ROPE_THETA) ** (-np.arange(nf, dtype=np.float32) / np.float32(nf))
    ar = row[:, None] * inv
    ac = col[:, None] * inv
    ang = np.concatenate([ar, ar, ac, ac], axis=-1).astype(np.float64)
    cos, sin = np.cos(ang), np.sin(ang)
    quarter = (np.arange(A_DH) // nf) % 2
    sin_up = np.where(quarter == 0, -sin, 0.0)
    sin_dn = np.where(quarter == 1, sin, 0.0)
    rep = V7X_LANES // A_DH
    return tuple(jnp.asarray(np.tile(a, (1, rep)), dtype=F32) for a in (cos, sin_up, sin_dn))


def kernel(x, c, ctx, c_ctx, w_mod, b_mod, norm1_g, w_in, a_qk_g, a_lambda, a_subln_g, b_pool_w,
           b_pool_s, c_qk_g, c_rpb, d_vn_g, d_ws, d_bs, w_branch, w_out, norm2_g, w_gu, w_down):
    b, s, d = x.shape
    depth = w_mod.shape[0]
    rope_tabs = _rope_tables(s)

    rows = -(-(b + 1) // 8) * 8
    c_all = jnp.zeros((rows, d), F32).at[:b].set(c).at[b].set(c_ctx)
    mod = _modulation(c_all, w_mod, b_mod)

    w_mix = w_in[:, :, :MIX_COLS].astype(BF16)
    w_gate = w_in[:, :, MIX_COLS:].astype(BF16)
    w_br = w_branch.astype(BF16)
    w_o = w_out.astype(BF16)
    w_gu_b = w_gu.astype(BF16)
    w_dn = w_down.astype(BF16)
    w_pool = b_pool_w.astype(BF16)
    w_s = d_ws.astype(BF16)

    for l in range(depth):
        last = l == depth - 1
        lam_init = 0.8 - 0.6 * math.exp(-0.3 * l)
        mx = [mod[l, :b, k * d:(k + 1) * d].reshape(b, 1, d) for k in range(6)]
        mc = [mod[l, b:b + 1, k * d:(k + 1) * d].reshape(1, 1, d) for k in range(6)]
        n1 = norm1_g[l].reshape(1, d)
        n2 = norm2_g[l].reshape(1, d)
        rep = SEG_W // A_DH
        qk_gains = jnp.stack([jnp.tile(a_qk_g[l, 0], rep), jnp.tile(a_qk_g[l, 1], rep),
                              jnp.tile(c_qk_g[l, 0], rep), jnp.tile(c_qk_g[l, 1], rep)])
        vn_g = d_vn_g[l].reshape(1, SEG_W)
        bs_full = jnp.repeat(d_bs[l].T, V7X_LANES, axis=1)
        bias_t = _nbr_bias_table(c_rpb[l])

        px = _in_proj(x, mx[0], mx[1], n1, w_mix, qk_gains, vn_g, w_s, bs_full, rope_tabs, MIXER_SEGS, l)
        lc = ctx.shape[1]
        flat = lambda t: t.reshape(1, b * lc, t.shape[-1])
        pc = _in_proj(flat(ctx), mc[0], mc[1], n1, w_mix, qk_gains, vn_g, w_s, bs_full, None,
                      CTX_KV_SEGS if last else MIXER_SEGS, l)
        pc = {seg: t.reshape(b, lc, SEG_W) for seg, t in pc.items()}

        ya = _diff_attn(px[SEG_AQ], [(px[SEG_AK], px[SEG_AV]), (pc[SEG_AK], pc[SEG_AV])],
                        a_lambda[l], a_subln_g[l], a_qk_g[l], lam_init, heads_per_step=2)
        yb = _pool(px[SEG_B], w_pool, b_pool_s[l], l)
        yc = _nbr_attn(px[SEG_CQ], px[SEG_CK], px[SEG_CV], pc[SEG_CK], pc[SEG_CV], bias_t, c_qk_g[l],
                       pairs_per_step=1)
        x = _merge(x, mx[0], mx[1], mx[2], n1, (ya, yb, yc, px[SEG_YD]), w_gate, w_br, w_o, l)
        x = _ffn(x, mx[3], mx[4], mx[5], n2, w_gu_b, w_dn, l)

        if not last:
            ya_c = _diff_attn(pc[SEG_AQ], [(pc[SEG_AK], pc[SEG_AV])], a_lambda[l], a_subln_g[l], a_qk_g[l], lam_init,
                              heads_per_step=A_HEADS)
            yb_c = _pool(pc[SEG_B], w_pool, b_pool_s[l], l)
            yc_c = _nbr_attn(pc[SEG_CQ], None, None, pc[SEG_CK], pc[SEG_CV], None, c_qk_g[l],
                             pairs_per_step=SEG_W // V7X_LANES)
            ys_c = tuple(flat(t) for t in (ya_c, yb_c, yc_c, pc[SEG_YD]))
            ctx_f = _merge(flat(ctx), mc[0], mc[1], mc[2], n1, ys_c, w_gate, w_br, w_o, l)
            ctx = _ffn(ctx_f, mc[3], mc[4], mc[5], n2, w_gu_b, w_dn, l).reshape(b, lc, d)
    return x
```

```python
import functools
import math

import jax
import jax.numpy as jnp
import numpy as np
from jax import lax
from jax.experimental import pallas as pl
from jax.experimental.pallas import tpu as pltpu

F32 = jnp.float32
BF16 = jnp.bfloat16

D_MODEL = 1024
GRID_W = 64
EPS = 1e-6
ROPE_THETA = 10000.0
A_HEADS = 4
A_DH = 64
POOL_WINDOWS = (2, 4, 8, 16)
C_DH = 64
NA_ROWS = 8
NA_COLS = 16
CHUNK = 128
N_BRANCH = 4
SEG_W = 512
N_SEG = 9
MIX_COLS = N_SEG * SEG_W
FFN_HIDDEN = 2816
SEG_AQ, SEG_AK, SEG_AV, SEG_B, SEG_CQ, SEG_CK, SEG_CV, SEG_DU, SEG_DV = range(N_SEG)
SEG_YD = N_SEG
SEG_DTYPES = (BF16, BF16, BF16, F32, BF16, BF16, BF16, F32, BF16, BF16)
MIXER_SEGS = (SEG_AQ, SEG_AK, SEG_AV, SEG_B, SEG_CQ, SEG_CK, SEG_CV, SEG_YD)
CTX_KV_SEGS = (SEG_AK, SEG_AV, SEG_CK, SEG_CV)

V7X_LANES = 128
V7X_VMEM_BYTES = 64 * 1024 * 1024
V7X_VMEM_LIMIT = V7X_VMEM_BYTES - 8 * 1024 * 1024

ROW_TILE = 1024
ATTN_Q_TILE = 512
ATTN_KEY_CHUNK = 256
SCORE_BOUND_SLACK = 1.02
DENOM_FLOOR = 1e-18
POOL_TILE = 256
POOL_HALO = 8
FFN_CHUNK = 256


def _params(n_axes, vmem_bytes):
    return pltpu.CompilerParams(
        dimension_semantics=("parallel",) * n_axes,
        vmem_limit_bytes=int(min(vmem_bytes, V7X_VMEM_LIMIT)))


def _resident(shape, layer=None):
    zeros = (0,) * len(shape)
    if layer is None:
        return pl.BlockSpec(shape, lambda *_: zeros, pipeline_mode=pl.Buffered(1))
    return pl.BlockSpec((None,) + tuple(shape), lambda *_: (layer,) + zeros, pipeline_mode=pl.Buffered(1))


def _dot(a, b):
    return jnp.dot(a, b, preferred_element_type=F32)


def _dot_nt(a, b):
    return lax.dot_general(a, b, (((1,), (1,)), ((), ())), preferred_element_type=F32)


def _dot_tn(a, b):
    return lax.dot_general(a, b, (((0,), (0,)), ((), ())), preferred_element_type=F32)


def _split_bf16(t):
    hi = t.astype(BF16)
    lo = (t - hi.astype(F32)).astype(BF16)
    return hi, lo


def _norm_modulate(x, gain, shift, scale):
    ms = jnp.mean(x * x, axis=-1, keepdims=True)
    n = x * lax.rsqrt(ms + EPS)
    return ((n * gain) * (1.0 + scale) + shift).astype(BF16)


def _mod_kernel(c_ref, w_ref, b_ref, o_ref):
    a = c_ref[...]
    a = a * jax.nn.sigmoid(a)
    a_hi, a_lo = _split_bf16(a)
    w_hi, w_lo = _split_bf16(w_ref[...])
    o_ref[...] = _dot(a_hi, w_hi) + _dot(a_hi, w_lo) + _dot(a_lo, w_hi) + b_ref[...]


def _modulation(c_all, w_mod, b_mod):
    depth, d, n = w_mod.shape
    rows = c_all.shape[0]
    tn = 768
    return pl.pallas_call(
        _mod_kernel,
        grid=(depth, n // tn),
        in_specs=[
            pl.BlockSpec((rows, d), lambda l, j: (0, 0)),
            pl.BlockSpec((None, d, tn), lambda l, j: (l, 0, j)),
            pl.BlockSpec((None, 1, tn), lambda l, j: (l, 0, j)),
        ],
        out_specs=pl.BlockSpec((None, rows, tn), lambda l, j: (l, 0, j)),
        out_shape=jax.ShapeDtypeStruct((depth, rows, n), F32),
        compiler_params=_params(2, 32 * 1024 * 1024),
        name="modulation",
    )(c_all, w_mod, b_mod.reshape(depth, 1, n))


def _half_block_rms(t, first_half, gain):
    sq = t * t
    s_first = jnp.sum(jnp.where(first_half, sq, 0.0), axis=-1, keepdims=True)
    s_second = jnp.sum(jnp.where(first_half, 0.0, sq), axis=-1, keepdims=True)
    ms = jnp.where(first_half, s_first, s_second) * (1.0 / A_DH)
    return t * lax.rsqrt(ms + EPS) * gain


def _rope_block(t, cos, sin_up, sin_dn):
    return (t * cos + pltpu.roll(t, V7X_LANES - 16, 1) * sin_up + pltpu.roll(t, 16, 1) * sin_dn)


def _in_proj_kernel(*refs, rope, out_segs):
    x_ref, sh_ref, sc_ref, ng_ref, w_ref, qkg_ref, vng_ref, ws_ref, bs_ref = refs[:9]
    pos = 9
    if rope:
        cos_ref, sup_ref, sdn_ref = refs[9:12]
        pos = 12
    outs = dict(zip(out_segs, refs[pos:pos + len(out_segs)]))
    gated = SEG_YD in outs
    needed = set(out_segs) | ({SEG_DU, SEG_DV} if gated else set())
    hb = _norm_modulate(x_ref[...], ng_ref[...], sh_ref[...], sc_ref[...])
    qk_row = {SEG_AQ: 0, SEG_AK: 1, SEG_CQ: 2, SEG_CK: 3}
    first_half = lax.broadcasted_iota(jnp.int32, (1, V7X_LANES), 1) < A_DH
    for seg in range(N_SEG):
        if seg not in needed:
            continue
        r = _dot(hb, w_ref[:, seg * SEG_W:(seg + 1) * SEG_W])
        if seg in qk_row:
            row = qk_row[seg]
            blocks = []
            for j in range(SEG_W // V7X_LANES):
                lanes = slice(j * V7X_LANES, (j + 1) * V7X_LANES)
                t = _half_block_rms(r[:, lanes], first_half, qkg_ref[row:row + 1, lanes])
                if rope and seg in (SEG_AQ, SEG_AK):
                    t = _rope_block(t, cos_ref[...], sup_ref[...], sdn_ref[...])
                blocks.append(t)
            r = jnp.concatenate(blocks, axis=1)
        elif seg == SEG_DU and gated:
            gate_u = r
        elif seg == SEG_DV:
            ms = jnp.mean(r * r, axis=-1, keepdims=True)
            r = r * lax.rsqrt(ms + EPS) * vng_ref[...]
            if gated:
                vb = r.astype(BF16)
                n_chunks = r.shape[0] // CHUNK
                for g in range(ws_ref.shape[0]):
                    lanes = slice(g * V7X_LANES, (g + 1) * V7X_LANES)
                    v_wide = jnp.concatenate([vb[n * CHUNK:(n + 1) * CHUNK, lanes] for n in range(n_chunks)], axis=1)
                    sv_wide = _dot(ws_ref[g], v_wide)
                    for n in range(n_chunks):
                        rows = slice(n * CHUNK, (n + 1) * CHUNK)
                        sv = sv_wide[:, n * V7X_LANES:(n + 1) * V7X_LANES] + bs_ref[:, lanes]
                        outs[SEG_YD][rows, lanes] = (gate_u[rows, lanes] * sv).astype(outs[SEG_YD].dtype)
        if seg in outs:
            outs[seg][...] = r.astype(outs[seg].dtype)


def _in_proj(x, shift, scale, norm_g, w_mix, qk_gains, vn_g, w_s, b_full, rope_tabs, out_segs, layer):
    b, l, d = x.shape
    tm = min(ROW_TILE, l)
    batched = shift.shape[0] != 1
    mod_map = (lambda bi, i: (bi, 0, 0)) if batched else (lambda bi, i: (0, 0, 0))
    rope = rope_tabs is not None
    in_specs = [
        pl.BlockSpec((None, tm, d), lambda bi, i: (bi, i, 0)),
        pl.BlockSpec((None, 1, d), mod_map),
        pl.BlockSpec((None, 1, d), mod_map),
        _resident((1, d)),
        _resident((d, MIX_COLS), layer),
        _resident((4, SEG_W)),
        _resident((1, SEG_W)),
        _resident(w_s.shape[1:], layer),
        _resident(b_full.shape),
    ]
    args = [x, shift, scale, norm_g, w_mix, qk_gains, vn_g, w_s, b_full]
    if rope:
        in_specs += [pl.BlockSpec((tm, V7X_LANES), lambda bi, i: (i, 0))] * 3
        args += list(rope_tabs)
    dtypes = [SEG_DTYPES[s] for s in out_segs]
    out_bytes = sum(tm * SEG_W * jnp.dtype(t).itemsize for t in dtypes)
    vmem = d * MIX_COLS * 2 + 2 * (tm * d * 4 + out_bytes) + 8 * tm * SEG_W * 4 + (8 << 20)
    outs = pl.pallas_call(
        functools.partial(_in_proj_kernel, rope=rope, out_segs=tuple(out_segs)),
        grid=(b, l // tm),
        in_specs=in_specs,
        out_specs=[pl.BlockSpec((None, tm, SEG_W), lambda bi, i: (bi, i, 0))] * len(out_segs),
        out_shape=[jax.ShapeDtypeStruct((b, l, SEG_W), t) for t in dtypes],
        compiler_params=_params(2, vmem),
        name="in_proj_rope" if rope else "in_proj",
    )(*args)
    return dict(zip(out_segs, outs))


def _diff_attn_kernel(*refs, n_parts, lq, tq, heads, lam_init):
    al_ref, sg_ref, qkg_ref, q_ref = refs[:4]
    kv = refs[4:4 + 2 * n_parts]
    o_ref, kcat_ref, vt_ref, s_ref, e_ref = refs[4 + 2 * n_parts:]
    al = al_ref[...]
    lam = (jnp.exp(jnp.sum(al[0:1] * al[1:2], axis=-1, keepdims=True))
           - jnp.exp(jnp.sum(al[2:3] * al[3:4], axis=-1, keepdims=True)) + lam_init)
    lane = lax.broadcasted_iota(jnp.int32, (1, V7X_LANES), 1)
    scale = A_DH ** -0.5
    masks = (jnp.where(lane < A_DH, scale, 0.0).astype(BF16),
             jnp.where(lane >= A_DH, scale, 0.0).astype(BF16))
    sub_gain = sg_ref[...] * (1.0 - lam_init)
    gains = jnp.max(jnp.abs(qkg_ref[...]), axis=-1, keepdims=True)
    bound = gains[0:1] * gains[1:2] * (A_DH * scale * SCORE_BOUND_SLACK)
    hw = 2 * A_DH
    n_keys = kcat_ref.shape[0]
    kc = math.gcd(n_keys, ATTN_KEY_CHUNK)
    chunks = range(0, n_keys, kc)
    n_tiles = lq // tq
    vt_ref[hw:, :] = jnp.ones((vt_ref.shape[0] - hw, n_keys), BF16)

    def finish(rows, cols, acc):
        o = (acc[0][:hw] * (1.0 / acc[0][hw:hw + 1]) - acc[1][:hw] * (lam / acc[1][hw:hw + 1])).T
        ms = jnp.mean(o * o, axis=-1, keepdims=True)
        o_ref[rows, cols] = (o * lax.rsqrt(ms + EPS) * sub_gain).astype(o_ref.dtype)

    def bounded_tile(cols, slot, r0):
        qc = q_ref[r0:r0 + tq, cols]
        acc = []
        for c, mask in enumerate(masks):
            qm = qc * mask
            for k0 in chunks:
                s = _dot_nt(kcat_ref[k0:k0 + kc, :], qm)
                e_ref[slot, c, k0:k0 + kc, :] = jnp.exp(s - bound).astype(BF16)
            acc.append(_dot(vt_ref[...], e_ref[slot, c]))
        finish(slice(r0, r0 + tq), cols, acc)
        return jnp.minimum(acc[0][hw:hw + 1], acc[1][hw:hw + 1])

    def exact_tile(cols, r0):
        qc = q_ref[pl.ds(r0, tq), cols]
        acc = []
        for c, mask in enumerate(masks):
            qm = qc * mask
            m8 = None
            for k0 in chunks:
                s = _dot_nt(kcat_ref[k0:k0 + kc, :], qm)
                s_ref[c, k0:k0 + kc, :] = s
                cm = jnp.max(s.reshape(kc // 8, 8, tq), axis=0)
                m8 = cm if m8 is None else jnp.maximum(m8, cm)
            m = jnp.max(m8, axis=0, keepdims=True)
            for k0 in chunks:
                rows = slice(k0, k0 + kc)
                e_ref[0, c, rows, :] = jnp.exp(s_ref[c, rows, :] - m).astype(BF16)
            acc.append(_dot(vt_ref[...], e_ref[0, c]))
        finish(pl.ds(r0, tq), cols, acc)

    def load_head(cols):
        off = 0
        for p in range(n_parts):
            n = kv[2 * p].shape[0]
            kcat_ref[off:off + n, :] = kv[2 * p][:, cols]
            vt_ref[0:hw, off:off + n] = kv[2 * p + 1][:, cols].astype(F32).T.astype(BF16)
            off += n

    for h in range(heads):
        cols = slice(h * hw, (h + 1) * hw)
        load_head(cols)
        if n_tiles == 1:
            exact_tile(cols, 0)
            continue
        smallest = None
        for j in range(n_tiles):
            denom = bounded_tile(cols, j % 2, j * tq)
            smallest = denom if smallest is None else jnp.minimum(smallest, denom)

        @pl.when(jnp.min(smallest) < DENOM_FLOOR)
        def _():
            def redo(j, carry):
                exact_tile(cols, pl.multiple_of(j * tq, tq))
                return carry
            lax.fori_loop(0, n_tiles, redo, 0)


def _diff_attn(q, kv_parts, a_lambda, subln_g, qk_g, lam_init, heads_per_step):
    b, lq, _ = q.shape
    tq = min(ATTN_Q_TILE, lq)
    hw = 2 * A_DH
    wdt = heads_per_step * hw
    head_spec = lambda n: pl.BlockSpec((None, n, wdt), lambda bi, h: (bi, 0, h))
    in_specs = [_resident(a_lambda.shape), _resident((1, hw)), _resident(qk_g.shape), head_spec(lq)]
    args = [a_lambda, subln_g.reshape(1, hw), qk_g, q]
    lk = 0
    for k, v in kv_parts:
        in_specs += [head_spec(k.shape[1]), head_spec(v.shape[1])]
        args += [k, v]
        lk += k.shape[1]
    ones_rows = 16
    scratch = [pltpu.VMEM((lk, hw), BF16), pltpu.VMEM((hw + ones_rows, lk), BF16),
               pltpu.VMEM((2, lk, tq), F32), pltpu.VMEM((2, 2, lk, tq), BF16)]
    vmem = (2 * (2 * lq + 4 * lk) * wdt * 2 + 2 * lk * (hw + ones_rows) * 2 + 2 * lk * tq * 8
            + 6 * ATTN_KEY_CHUNK * tq * 4 + (8 << 20))
    return pl.pallas_call(
        functools.partial(_diff_attn_kernel, n_parts=len(kv_parts), lq=lq, tq=tq,
                          heads=heads_per_step, lam_init=lam_init),
        grid=(b, A_HEADS // heads_per_step),
        in_specs=in_specs,
        out_specs=head_spec(lq),
        out_shape=jax.ShapeDtypeStruct((b, lq, A_HEADS * hw), BF16),
        scratch_shapes=scratch,
        compiler_params=_params(2, vmem),
        name="diff_attn",
    )(*args)


def _tree_sum(terms):
    while len(terms) > 1:
        terms = [terms[i] + terms[i + 1] for i in range(0, len(terms), 2)]
    return terms[0]


def _pool_kernel(p_ref, w_ref, s_ref, o_ref, pad_ref, *, l, tp):
    wdt = p_ref.shape[-1]
    pad_ref[0:POOL_HALO, :] = jnp.zeros((POOL_HALO, wdt), F32)
    pad_ref[POOL_HALO + l:, :] = jnp.zeros((POOL_HALO, wdt), F32)
    pad_ref[POOL_HALO:POOL_HALO + l, :] = p_ref[...]
    n_tiles = l // tp
    for ti in range(n_tiles):
        t0 = ti * tp
        edge = ti == 0 or ti == n_tiles - 1
        if edge:
            t_row = t0 + lax.broadcasted_iota(jnp.int32, (tp, V7X_LANES), 0)
        for g, w in enumerate(POOL_WINDOWS):
            lanes = slice(g * V7X_LANES, (g + 1) * V7X_LANES)
            shifted = [pad_ref[POOL_HALO + t0 + j:POOL_HALO + t0 + j + tp, lanes]
                       for j in range(-(w // 2), w // 2)]
            centre = shifted[w // 2]
            total = _tree_sum(shifted)
            if edge:
                cnt = (jnp.minimum(t_row + w // 2, l) - jnp.maximum(t_row - w // 2, 0)).astype(F32)
                mean = total / cnt
            else:
                mean = total * (1.0 / w)
            y = _dot((mean - centre).astype(BF16), w_ref[g]) * s_ref[:, lanes]
            o_ref[t0:t0 + tp, lanes] = y.astype(o_ref.dtype)


def _pool(p, w_pool, s_pool, layer):
    b, l, wdt = p.shape
    tp = min(POOL_TILE, l)
    seq = pl.BlockSpec((None, l, wdt), lambda bi: (bi, 0, 0))
    return pl.pallas_call(
        functools.partial(_pool_kernel, l=l, tp=tp),
        grid=(b,),
        in_specs=[seq, _resident(w_pool.shape[1:], layer), _resident((1, wdt))],
        out_specs=seq,
        out_shape=jax.ShapeDtypeStruct((b, l, wdt), BF16),
        scratch_shapes=[pltpu.VMEM((l + 2 * POOL_HALO, wdt), F32)],
        compiler_params=_params(1, 5 * l * wdt * 4 + (16 << 20)),
        name="pool",
    )(p, w_pool, s_pool.reshape(1, wdt))


def _nbr_attn_kernel(*refs, n_rows, pairs, local):
    if local:
        qkg_ref, q_ref, k_ref, v_ref, kc_ref, vc_ref, bias_ref, o_ref = refs
    else:
        qkg_ref, q_ref, kc_ref, vc_ref, o_ref = refs
    gains = jnp.max(jnp.abs(qkg_ref[...]), axis=-1, keepdims=True)
    bound = gains[0:1] * gains[1:2] * (C_DH ** 0.5 * SCORE_BOUND_SLACK)
    if local:
        both = jnp.maximum(bias_ref[0], bias_ref[NA_ROWS - 1])
        bmax = jnp.max(jnp.max(both, axis=0, keepdims=True), axis=1, keepdims=True)
        bound = bound + jnp.maximum(bmax, 0.0)
    lane = lax.broadcasted_iota(jnp.int32, (1, V7X_LANES), 1)
    scale = C_DH ** -0.5
    m_first = jnp.where(lane < C_DH, scale, 0.0).astype(BF16)
    m_second = jnp.where(lane >= C_DH, scale, 0.0).astype(BF16)
    lane_out = lax.broadcasted_iota(jnp.int32, (GRID_W, V7X_LANES), 1)
    n_loc = NA_ROWS * GRID_W
    hq = 2 * GRID_W
    with_ones = lambda v: jnp.concatenate([v, jnp.ones_like(v)], axis=1)
    first_key_row = [min(max(r - NA_ROWS // 2, 0), n_rows - NA_ROWS) for r in range(n_rows)]

    def attend(cols, shift):
        blocks = []
        for r in range(n_rows):
            qr = q_ref[r * GRID_W:(r + 1) * GRID_W, cols]
            blocks += [qr * m_first, qr * m_second]
        qbd = jnp.concatenate(blocks, axis=0)
        s_ctx = _dot_nt(qbd, kc_ref[:, cols])
        if local:
            s_loc = jnp.concatenate(
                [_dot_nt(qbd[r * hq:(r + 1) * hq], k_ref[rs * GRID_W:rs * GRID_W + n_loc, cols])
                 + bias_ref[rs - r + NA_ROWS - 1] for r, rs in enumerate(first_key_row)], axis=0)
        if shift is None:
            shift = jnp.max(s_ctx, axis=-1, keepdims=True)
            if local:
                shift = jnp.maximum(shift, jnp.max(s_loc, axis=-1, keepdims=True))
        e_ctx = jnp.exp(s_ctx - shift)
        full = _dot(e_ctx.astype(BF16), with_ones(vc_ref[:, cols]))
        if local:
            e_loc = jnp.exp(s_loc - shift).astype(BF16)
            full = full + jnp.concatenate(
                [_dot(e_loc[r * hq:(r + 1) * hq], with_ones(v_ref[rs * GRID_W:rs * GRID_W + n_loc, cols]))
                 for r, rs in enumerate(first_key_row)], axis=0)
        denom = full[:, V7X_LANES:V7X_LANES + 1]
        full = full[:, :V7X_LANES] * (1.0 / denom)
        out = jnp.concatenate(
            [jnp.where(lane_out < C_DH, full[r * hq:r * hq + GRID_W], full[r * hq + GRID_W:(r + 1) * hq])
             for r in range(n_rows)], axis=0)
        o_ref[:, cols] = out.astype(o_ref.dtype)
        return jnp.min(denom, axis=0, keepdims=True)

    pair_cols = [slice(p * V7X_LANES, (p + 1) * V7X_LANES) for p in range(pairs)]
    smallest = functools.reduce(jnp.minimum, [attend(cols, bound) for cols in pair_cols])

    @pl.when(jnp.min(smallest) < DENOM_FLOOR)
    def _():
        for cols in pair_cols:
            attend(cols, None)


def _nbr_attn(q, k, v, kc, vc, bias_t, qk_g, pairs_per_step):
    b, lq, wdt = q.shape
    lc = kc.shape[1]
    n_rows = lq // GRID_W
    local = k is not None
    blk = pairs_per_step * V7X_LANES
    pair = lambda n: pl.BlockSpec((None, n, blk), lambda h, bi: (bi, 0, h))
    if local:
        assert pairs_per_step == 1
        in_specs = [_resident(qk_g.shape), pair(lq), pair(lq), pair(lq), pair(lc), pair(lc),
                    pl.BlockSpec((None,) + bias_t.shape[1:], lambda h, bi: (h, 0, 0, 0))]
        args = [qk_g, q, k, v, kc, vc, bias_t]
    else:
        in_specs = [_resident(qk_g.shape), pair(lq), pair(lc), pair(lc)]
        args = [qk_g, q, kc, vc]
    return pl.pallas_call(
        functools.partial(_nbr_attn_kernel, n_rows=n_rows, pairs=pairs_per_step, local=local),
        grid=(wdt // blk, b),
        in_specs=in_specs,
        out_specs=pair(lq),
        out_shape=jax.ShapeDtypeStruct((b, lq, wdt), BF16),
        compiler_params=_params(2, V7X_VMEM_LIMIT),
        name="nbr_attn" if local else "ctx_attn",
    )(*args)


N_ROW_OFFSETS = 2 * NA_ROWS - 1
ROW_OFFSETS_PAD = 16


def _nbr_bias_kernel(base_ref, mask_ref, o_ref):
    lane = lax.broadcasted_iota(jnp.int32, (GRID_W, V7X_LANES), 1)
    first_half = lane < GRID_W
    blocks = []
    for h in range(2):
        per_offset = []
        for dr in range(N_ROW_OFFSETS):
            row = base_ref[h * ROW_OFFSETS_PAD + dr:h * ROW_OFFSETS_PAD + dr + 1, :]
            low = pltpu.roll(jnp.broadcast_to(row, (GRID_W, V7X_LANES)), 0, 1, stride=1, stride_axis=0)
            per_offset.append((low, pltpu.roll(low, GRID_W, 1)))
        blocks.append(per_offset)
    for d0 in range(NA_ROWS):
        for h in range(2):
            rows = slice(h * GRID_W, (h + 1) * GRID_W)
            for c in range(NA_ROWS // 2):
                lanes = slice(c * V7X_LANES, (c + 1) * V7X_LANES)
                blk = jnp.where(first_half, blocks[h][d0 + 2 * c][0], blocks[h][d0 + 2 * c + 1][1])
                o_ref[d0, rows, lanes] = blk + mask_ref[rows, lanes]


def _nbr_bias_table(rpb):
    h, n_dr, n_dc = rpb.shape
    base = jnp.roll(jnp.pad(rpb, ((0, 0), (0, ROW_OFFSETS_PAD - n_dr), (0, V7X_LANES - n_dc))),
                    -(NA_COLS - 1), axis=-1).reshape(h // 2, 2 * ROW_OFFSETS_PAD, V7X_LANES)
    qc = np.arange(GRID_W)[:, None]
    kc = np.arange(GRID_W)[None, :]
    win0 = np.clip(qc - NA_COLS // 2, 0, GRID_W - NA_COLS)
    valid = (kc >= win0) & (kc < win0 + NA_COLS)
    mask = np.tile(np.where(valid, 0.0, -np.inf).astype(np.float32), (2, NA_ROWS))
    shape = (h // 2, NA_ROWS, 2 * GRID_W, NA_ROWS * GRID_W)
    return pl.pallas_call(
        _nbr_bias_kernel,
        grid=(h // 2,),
        in_specs=[pl.BlockSpec((None, 2 * ROW_OFFSETS_PAD, V7X_LANES), lambda p: (p, 0, 0)),
                  _resident(mask.shape)],
        out_specs=pl.BlockSpec((None,) + shape[1:], lambda p: (p, 0, 0, 0)),
        out_shape=jax.ShapeDtypeStruct(shape, F32),
        compiler_params=_params(1, 32 * 1024 * 1024),
        name="nbr_bias",
    )(base, jnp.asarray(mask))


def _merge_kernel(x_ref, sh_ref, sc_ref, g1_ref, ng_ref, ya_ref, yb_ref, yc_ref, yd_ref,
                  wg_ref, wb_ref, wo_ref, o_ref):
    x = x_ref[...]
    d = x.shape[-1]
    hb = _norm_modulate(x, ng_ref[...], sh_ref[...], sc_ref[...])
    acc = None
    for i, y_ref in enumerate((ya_ref, yb_ref, yc_ref, yd_ref)):
        gate = _dot(hb, wg_ref[:, i * d:(i + 1) * d])
        term = jax.nn.sigmoid(gate) * _dot(y_ref[...], wb_ref[i * SEG_W:(i + 1) * SEG_W, :])
        acc = term if acc is None else acc + term
    o_ref[...] = x + g1_ref[...] * _dot(acc.astype(BF16), wo_ref[...])


def _merge(x, shift, scale, g1, norm_g, ys, w_gate, w_branch, w_out, layer):
    b, l, d = x.shape
    tm = min(ROW_TILE, l)
    batched = shift.shape[0] != 1
    mod_map = (lambda bi, i: (bi, 0, 0)) if batched else (lambda bi, i: (0, 0, 0))
    x_tile = pl.BlockSpec((None, tm, d), lambda bi, i: (bi, i, 0))
    y_tile = pl.BlockSpec((None, tm, SEG_W), lambda bi, i: (bi, i, 0))
    mod = pl.BlockSpec((None, 1, d), mod_map)
    weights = sum(math.prod(w.shape[1:]) for w in (w_gate, w_branch, w_out)) * 2
    vmem = weights + 2 * (2 * tm * d * 4 + 4 * tm * SEG_W * 2) + 8 * tm * d * 4 + (8 << 20)
    return pl.pallas_call(
        _merge_kernel,
        grid=(b, l // tm),
        in_specs=[x_tile, mod, mod, mod, _resident((1, d)), y_tile, y_tile, y_tile, y_tile,
                  _resident(w_gate.shape[1:], layer), _resident(w_branch.shape[1:], layer),
                  _resident(w_out.shape[1:], layer)],
        out_specs=x_tile,
        out_shape=jax.ShapeDtypeStruct(x.shape, F32),
        compiler_params=_params(2, vmem),
        name="merge",
    )(x, shift, scale, g1, norm_g, *ys, w_gate, w_branch, w_out)


def _ffn_kernel(x_ref, sh_ref, sc_ref, g2_ref, ng_ref, wgu_ref, wd_ref, o_ref, u_ref):
    x = x_ref[...]
    hb = _norm_modulate(x, ng_ref[...], sh_ref[...], sc_ref[...])
    hidden = wd_ref.shape[0]
    for c in range(hidden // FFN_CHUNK):
        cols = slice(c * FFN_CHUNK, (c + 1) * FFN_CHUNK)
        a = _dot(hb, wgu_ref[:, cols])
        bb = _dot(hb, wgu_ref[:, hidden + c * FFN_CHUNK:hidden + (c + 1) * FFN_CHUNK])
        u_ref[:, cols] = (a * jax.nn.sigmoid(a) * bb).astype(BF16)
    o_ref[...] = x + g2_ref[...] * _dot(u_ref[...], wd_ref[...])


def _ffn(x, shift, scale, g2, norm_g, w_gu, w_down, layer):
    b, l, d = x.shape
    tm = min(ROW_TILE, l)
    hidden = w_down.shape[1]
    batched = shift.shape[0] != 1
    mod_map = (lambda bi, i: (bi, 0, 0)) if batched else (lambda bi, i: (0, 0, 0))
    x_tile = pl.BlockSpec((None, tm, d), lambda bi, i: (bi, i, 0))
    mod = pl.BlockSpec((None, 1, d), mod_map)
    vmem = (math.prod(w_gu.shape[1:]) + math.prod(w_down.shape[1:])) * 2 + 4 * tm * d * 4 + tm * hidden * 2 + 8 * tm * d * 4 + (8 << 20)
    return pl.pallas_call(
        _ffn_kernel,
        grid=(b, l // tm),
        in_specs=[x_tile, mod, mod, mod, _resident((1, d)),
                  _resident(w_gu.shape[1:], layer), _resident(w_down.shape[1:], layer)],
        out_specs=x_tile,
        out_shape=jax.ShapeDtypeStruct(x.shape, F32),
        scratch_shapes=[pltpu.VMEM((tm, hidden), BF16)],
        compiler_params=_params(2, vmem),
        name="ffn",
    )(x, shift, scale, g2, norm_g, w_gu, w_down)


def _rope_tables(n_tok):
    nf = A_DH // 4
    t = np.arange(n_tok)
    row = (t // GRID_W).astype(np.float32)
    col = (t % GRID_W).astype(np.float32)
    inv = np.float32(ROPE_THETA) ** (-np.arange(nf, dtype=np.float32) / np.float32(nf))
    ar = row[:, None] * inv
    ac = col[:, None] * inv
    ang = np.concatenate([ar, ar, ac, ac], axis=-1).astype(np.float64)
    cos, sin = np.cos(ang), np.sin(ang)
    quarter = (np.arange(A_DH) // nf) % 2
    sin_up = np.where(quarter == 0, -sin, 0.0)
    sin_dn = np.where(quarter == 1, sin, 0.0)
    rep = V7X_LANES // A_DH
    return tuple(jnp.asarray(np.tile(a, (1, rep)), dtype=F32) for a in (cos, sin_up, sin_dn))


def kernel(x, c, ctx, c_ctx, w_mod, b_mod, norm1_g, w_in, a_qk_g, a_lambda, a_subln_g, b_pool_w,
           b_pool_s, c_qk_g, c_rpb, d_vn_g, d_ws, d_bs, w_branch, w_out, norm2_g, w_gu, w_down):
    b, s, d = x.shape
    depth = w_mod.shape[0]
    rope_tabs = _rope_tables(s)

    rows = -(-(b + 1) // 8) * 8
    c_all = jnp.zeros((rows, d), F32).at[:b].set(c).at[b].set(c_ctx)
    mod = _modulation(c_all, w_mod, b_mod)

    w_mix = w_in[:, :, :MIX_COLS].astype(BF16)
    w_gate = w_in[:, :, MIX_COLS:].astype(BF16)
    w_br = w_branch.astype(BF16)
    w_o = w_out.astype(BF16)
    w_gu_b = w_gu.astype(BF16)
    w_dn = w_down.astype(BF16)
    w_pool = b_pool_w.astype(BF16)
    w_s = d_ws.astype(BF16)

    for l in range(depth):
        last = l == depth - 1
        lam_init = 0.8 - 0.6 * math.exp(-0.3 * l)
        mx = [mod[l, :b, k * d:(k + 1) * d].reshape(b, 1, d) for k in range(6)]
        mc = [mod[l, b:b + 1, k * d:(k + 1) * d].reshape(1, 1, d) for k in range(6)]
        n1 = norm1_g[l].reshape(1, d)
        n2 = norm2_g[l].reshape(1, d)
        rep = SEG_W // A_DH
        qk_gains = jnp.stack([jnp.tile(a_qk_g[l, 0], rep), jnp.tile(a_qk_g[l, 1], rep),
                              jnp.tile(c_qk_g[l, 0], rep), jnp.tile(c_qk_g[l, 1], rep)])
        vn_g = d_vn_g[l].reshape(1, SEG_W)
        bs_full = jnp.repeat(d_bs[l].T, V7X_LANES, axis=1)
        bias_t = _nbr_bias_table(c_rpb[l])

        px = _in_proj(x, mx[0], mx[1], n1, w_mix, qk_gains, vn_g, w_s, bs_full, rope_tabs, MIXER_SEGS, l)
        lc = ctx.shape[1]
        flat = lambda t: t.reshape(1, b * lc, t.shape[-1])
        pc = _in_proj(flat(ctx), mc[0], mc[1], n1, w_mix, qk_gains, vn_g, w_s, bs_full, None,
                      CTX_KV_SEGS if last else MIXER_SEGS, l)
        pc = {seg: t.reshape(b, lc, SEG_W) for seg, t in pc.items()}

        ya = _diff_attn(px[SEG_AQ], [(px[SEG_AK], px[SEG_AV]), (pc[SEG_AK], pc[SEG_AV])],
                        a_lambda[l], a_subln_g[l], a_qk_g[l], lam_init, heads_per_step=1)
        yb = _pool(px[SEG_B], w_pool, b_pool_s[l], l)
        yc = _nbr_attn(px[SEG_CQ], px[SEG_CK], px[SEG_CV], pc[SEG_CK], pc[SEG_CV], bias_t, c_qk_g[l],
                       pairs_per_step=1)
        x = _merge(x, mx[0], mx[1], mx[2], n1, (ya, yb, yc, px[SEG_YD]), w_gate, w_br, w_o, l)
        x = _ffn(x, mx[3], mx[4], mx[5], n2, w_gu_b, w_dn, l)

        if not last:
            ya_c = _diff_attn(pc[SEG_AQ], [(pc[SEG_AK], pc[SEG_AV])], a_lambda[l], a_subln_g[l], a_qk_g[l], lam_init,
                              heads_per_step=A_HEADS)
            yb_c = _pool(pc[SEG_B], w_pool, b_pool_s[l], l)
            yc_c = _nbr_attn(pc[SEG_CQ], None, None, pc[SEG_CK], pc[SEG_CV], None, c_qk_g[l],
                             pairs_per_step=SEG_W // V7X_LANES)
            ys_c = tuple(flat(t) for t in (ya_c, yb_c, yc_c, pc[SEG_YD]))
            ctx_f = _merge(flat(ctx), mc[0], mc[1], mc[2], n1, ys_c, w_gate, w_br, w_o, l)
            ctx = _ffn(ctx_f, mc[3], mc[4], mc[5], n2, w_gu_b, w_dn, l).reshape(b, lc, d)
    return x
```

```python
import functools
import math

import jax
import jax.numpy as jnp
import numpy as np
from jax import lax
from jax.experimental import pallas as pl
from jax.experimental.pallas import tpu as pltpu

F32 = jnp.float32
BF16 = jnp.bfloat16

D_MODEL = 1024
GRID_W = 64
EPS = 1e-6
ROPE_THETA = 10000.0
A_HEADS = 4
A_DH = 64
POOL_WINDOWS = (2, 4, 8, 16)
C_DH = 64
NA_ROWS = 8
NA_COLS = 16
CHUNK = 128
N_BRANCH = 4
SEG_W = 512
N_SEG = 9
MIX_COLS = N_SEG * SEG_W
FFN_HIDDEN = 2816
SEG_AQ, SEG_AK, SEG_AV, SEG_B, SEG_CQ, SEG_CK, SEG_CV, SEG_DU, SEG_DV = range(N_SEG)
SEG_YD = N_SEG
SEG_DTYPES = (BF16, BF16, BF16, F32, BF16, BF16, BF16, F32, BF16, BF16)
MIXER_SEGS = (SEG_AQ, SEG_AK, SEG_AV, SEG_B, SEG_CQ, SEG_CK, SEG_CV, SEG_YD)
CTX_KV_SEGS = (SEG_AK, SEG_AV, SEG_CK, SEG_CV)

V7X_LANES = 128
V7X_VMEM_BYTES = 64 * 1024 * 1024
V7X_VMEM_LIMIT = V7X_VMEM_BYTES - 8 * 1024 * 1024

ROW_TILE = 1024
ATTN_Q_TILE = 512
ATTN_KEY_CHUNK = 256
SCORE_BOUND_SLACK = 1.02
DENOM_FLOOR = 1e-18
POOL_TILE = 256
POOL_HALO = 8
FFN_CHUNK = 256
MERGE_COL_BLOCK = 256


def _params(n_axes, vmem_bytes):
    return pltpu.CompilerParams(
        dimension_semantics=("parallel",) * n_axes,
        vmem_limit_bytes=int(min(vmem_bytes, V7X_VMEM_LIMIT)))


def _resident(shape, layer=None):
    zeros = (0,) * len(shape)
    if layer is None:
        return pl.BlockSpec(shape, lambda *_: zeros, pipeline_mode=pl.Buffered(1))
    return pl.BlockSpec((None,) + tuple(shape), lambda *_: (layer,) + zeros, pipeline_mode=pl.Buffered(1))


def _dot(a, b):
    return jnp.dot(a, b, preferred_element_type=F32)


def _dot_nt(a, b):
    return lax.dot_general(a, b, (((1,), (1,)), ((), ())), preferred_element_type=F32)


def _dot_tn(a, b):
    return lax.dot_general(a, b, (((0,), (0,)), ((), ())), preferred_element_type=F32)


def _split_bf16(t):
    hi = t.astype(BF16)
    lo = (t - hi.astype(F32)).astype(BF16)
    return hi, lo


def _norm_modulate(x, gain, shift, scale):
    ms = jnp.mean(x * x, axis=-1, keepdims=True)
    n = x * lax.rsqrt(ms + EPS)
    return ((n * gain) * (1.0 + scale) + shift).astype(BF16)


def _mod_kernel(c_ref, w_ref, b_ref, o_ref):
    a = c_ref[...]
    a = a * jax.nn.sigmoid(a)
    a_hi, a_lo = _split_bf16(a)
    w_hi, w_lo = _split_bf16(w_ref[...])
    o_ref[...] = _dot(a_hi, w_hi) + _dot(a_hi, w_lo) + _dot(a_lo, w_hi) + b_ref[...]


def _modulation(c_all, w_mod, b_mod):
    depth, d, n = w_mod.shape
    rows = c_all.shape[0]
    tn = 768
    return pl.pallas_call(
        _mod_kernel,
        grid=(depth, n // tn),
        in_specs=[
            pl.BlockSpec((rows, d), lambda l, j: (0, 0)),
            pl.BlockSpec((None, d, tn), lambda l, j: (l, 0, j)),
            pl.BlockSpec((None, 1, tn), lambda l, j: (l, 0, j)),
        ],
        out_specs=pl.BlockSpec((None, rows, tn), lambda l, j: (l, 0, j)),
        out_shape=jax.ShapeDtypeStruct((depth, rows, n), F32),
        compiler_params=_params(2, 32 * 1024 * 1024),
        name="modulation",
    )(c_all, w_mod, b_mod.reshape(depth, 1, n))


def _half_block_rms(t, first_half, gain):
    sq = t * t
    s_first = jnp.sum(jnp.where(first_half, sq, 0.0), axis=-1, keepdims=True)
    s_second = jnp.sum(jnp.where(first_half, 0.0, sq), axis=-1, keepdims=True)
    ms = jnp.where(first_half, s_first, s_second) * (1.0 / A_DH)
    return t * lax.rsqrt(ms + EPS) * gain


def _rope_block(t, cos, sin_up, sin_dn):
    return (t * cos + pltpu.roll(t, V7X_LANES - 16, 1) * sin_up + pltpu.roll(t, 16, 1) * sin_dn)


def _in_proj_kernel(*refs, rope, out_segs):
    x_ref, sh_ref, sc_ref, ng_ref, w_ref, qkg_ref, vng_ref, ws_ref, bs_ref = refs[:9]
    pos = 9
    if rope:
        cos_ref, sup_ref, sdn_ref = refs[9:12]
        pos = 12
    outs = dict(zip(out_segs, refs[pos:pos + len(out_segs)]))
    gated = SEG_YD in outs
    needed = set(out_segs) | ({SEG_DU, SEG_DV} if gated else set())
    hb = _norm_modulate(x_ref[...], ng_ref[...], sh_ref[...], sc_ref[...])
    qk_row = {SEG_AQ: 0, SEG_AK: 1, SEG_CQ: 2, SEG_CK: 3}
    first_half = lax.broadcasted_iota(jnp.int32, (1, V7X_LANES), 1) < A_DH
    for seg in range(N_SEG):
        if seg not in needed:
            continue
        r = _dot(hb, w_ref[:, seg * SEG_W:(seg + 1) * SEG_W])
        if seg in qk_row:
            row = qk_row[seg]
            blocks = []
            for j in range(SEG_W // V7X_LANES):
                lanes = slice(j * V7X_LANES, (j + 1) * V7X_LANES)
                t = _half_block_rms(r[:, lanes], first_half, qkg_ref[row:row + 1, lanes])
                if rope and seg in (SEG_AQ, SEG_AK):
                    t = _rope_block(t, cos_ref[...], sup_ref[...], sdn_ref[...])
                blocks.append(t)
            r = jnp.concatenate(blocks, axis=1)
        elif seg == SEG_DU and gated:
            gate_u = r
        elif seg == SEG_DV:
            ms = jnp.mean(r * r, axis=-1, keepdims=True)
            r = r * lax.rsqrt(ms + EPS) * vng_ref[...]
            if gated:
                vb = r.astype(BF16)
                n_chunks = r.shape[0] // CHUNK
                for g in range(ws_ref.shape[0]):
                    lanes = slice(g * V7X_LANES, (g + 1) * V7X_LANES)
                    v_wide = jnp.concatenate([vb[n * CHUNK:(n + 1) * CHUNK, lanes] for n in range(n_chunks)], axis=1)
                    sv_wide = _dot(ws_ref[g], v_wide)
                    for n in range(n_chunks):
                        rows = slice(n * CHUNK, (n + 1) * CHUNK)
                        sv = sv_wide[:, n * V7X_LANES:(n + 1) * V7X_LANES] + bs_ref[:, lanes]
                        outs[SEG_YD][rows, lanes] = (gate_u[rows, lanes] * sv).astype(outs[SEG_YD].dtype)
        if seg in outs:
            outs[seg][...] = r.astype(outs[seg].dtype)


def _in_proj(x, shift, scale, norm_g, w_mix, qk_gains, vn_g, w_s, b_full, rope_tabs, out_segs, layer):
    b, l, d = x.shape
    tm = min(ROW_TILE, l)
    batched = shift.shape[0] != 1
    mod_map = (lambda bi, i: (bi, 0, 0)) if batched else (lambda bi, i: (0, 0, 0))
    rope = rope_tabs is not None
    in_specs = [
        pl.BlockSpec((None, tm, d), lambda bi, i: (bi, i, 0)),
        pl.BlockSpec((None, 1, d), mod_map),
        pl.BlockSpec((None, 1, d), mod_map),
        _resident((1, d)),
        _resident((d, MIX_COLS), layer),
        _resident((4, SEG_W)),
        _resident((1, SEG_W)),
        _resident(w_s.shape[1:], layer),
        _resident(b_full.shape),
    ]
    args = [x, shift, scale, norm_g, w_mix, qk_gains, vn_g, w_s, b_full]
    if rope:
        in_specs += [pl.BlockSpec((tm, V7X_LANES), lambda bi, i: (i, 0))] * 3
        args += list(rope_tabs)
    dtypes = [SEG_DTYPES[s] for s in out_segs]
    out_bytes = sum(tm * SEG_W * jnp.dtype(t).itemsize for t in dtypes)
    vmem = d * MIX_COLS * 2 + 2 * (tm * d * 4 + out_bytes) + 8 * tm * SEG_W * 4 + (8 << 20)
    outs = pl.pallas_call(
        functools.partial(_in_proj_kernel, rope=rope, out_segs=tuple(out_segs)),
        grid=(b, l // tm),
        in_specs=in_specs,
        out_specs=[pl.BlockSpec((None, tm, SEG_W), lambda bi, i: (bi, i, 0))] * len(out_segs),
        out_shape=[jax.ShapeDtypeStruct((b, l, SEG_W), t) for t in dtypes],
        compiler_params=_params(2, vmem),
        name="in_proj_rope" if rope else "in_proj",
    )(*args)
    return dict(zip(out_segs, outs))


def _diff_attn_kernel(*refs, n_parts, lq, tq, heads, lam_init):
    al_ref, sg_ref, qkg_ref, q_ref = refs[:4]
    kv = refs[4:4 + 2 * n_parts]
    o_ref, kcat_ref, vt_ref, s_ref, e_ref = refs[4 + 2 * n_parts:]
    al = al_ref[...]
    lam = (jnp.exp(jnp.sum(al[0:1] * al[1:2], axis=-1, keepdims=True))
           - jnp.exp(jnp.sum(al[2:3] * al[3:4], axis=-1, keepdims=True)) + lam_init)
    lane = lax.broadcasted_iota(jnp.int32, (1, V7X_LANES), 1)
    scale = A_DH ** -0.5
    masks = (jnp.where(lane < A_DH, scale, 0.0).astype(BF16),
             jnp.where(lane >= A_DH, scale, 0.0).astype(BF16))
    sub_gain = sg_ref[...] * (1.0 - lam_init)
    gains = jnp.max(jnp.abs(qkg_ref[...]), axis=-1, keepdims=True)
    bound = gains[0:1] * gains[1:2] * (A_DH * scale * SCORE_BOUND_SLACK)
    hw = 2 * A_DH
    n_keys = kcat_ref.shape[0]
    kc = math.gcd(n_keys, ATTN_KEY_CHUNK)
    chunks = range(0, n_keys, kc)
    n_tiles = lq // tq
    vt_ref[hw:, :] = jnp.ones((vt_ref.shape[0] - hw, n_keys), BF16)

    def finish(rows, cols, acc):
        o = (acc[0][:hw] * (1.0 / acc[0][hw:hw + 1]) - acc[1][:hw] * (lam / acc[1][hw:hw + 1])).T
        ms = jnp.mean(o * o, axis=-1, keepdims=True)
        o_ref[rows, cols] = (o * lax.rsqrt(ms + EPS) * sub_gain).astype(o_ref.dtype)

    def bounded_tile(cols, slot, r0):
        qc = q_ref[r0:r0 + tq, cols]
        acc = []
        for c, mask in enumerate(masks):
            qm = qc * mask
            for k0 in chunks:
                s = _dot_nt(kcat_ref[k0:k0 + kc, :], qm)
                e_ref[slot, c, k0:k0 + kc, :] = jnp.exp(s - bound).astype(BF16)
            acc.append(_dot(vt_ref[...], e_ref[slot, c]))
        finish(slice(r0, r0 + tq), cols, acc)
        return jnp.minimum(acc[0][hw:hw + 1], acc[1][hw:hw + 1])

    def exact_tile(cols, r0):
        qc = q_ref[pl.ds(r0, tq), cols]
        acc = []
        for c, mask in enumerate(masks):
            qm = qc * mask
            m8 = None
            for k0 in chunks:
                s = _dot_nt(kcat_ref[k0:k0 + kc, :], qm)
                s_ref[c, k0:k0 + kc, :] = s
                cm = jnp.max(s.reshape(kc // 8, 8, tq), axis=0)
                m8 = cm if m8 is None else jnp.maximum(m8, cm)
            m = jnp.max(m8, axis=0, keepdims=True)
            for k0 in chunks:
                rows = slice(k0, k0 + kc)
                e_ref[0, c, rows, :] = jnp.exp(s_ref[c, rows, :] - m).astype(BF16)
            acc.append(_dot(vt_ref[...], e_ref[0, c]))
        finish(pl.ds(r0, tq), cols, acc)

    def load_head(cols):
        off = 0
        for p in range(n_parts):
            n = kv[2 * p].shape[0]
            kcat_ref[off:off + n, :] = kv[2 * p][:, cols]
            vt_ref[0:hw, off:off + n] = kv[2 * p + 1][:, cols].astype(F32).T.astype(BF16)
            off += n

    for h in range(heads):
        cols = slice(h * hw, (h + 1) * hw)
        load_head(cols)
        if n_tiles == 1:
            exact_tile(cols, 0)
            continue
        smallest = None
        for j in range(n_tiles):
            denom = bounded_tile(cols, j % 2, j * tq)
            smallest = denom if smallest is None else jnp.minimum(smallest, denom)

        @pl.when(jnp.min(smallest) < DENOM_FLOOR)
        def _():
            def redo(j, carry):
                exact_tile(cols, pl.multiple_of(j * tq, tq))
                return carry
            lax.fori_loop(0, n_tiles, redo, 0)


def _diff_attn(q, kv_parts, a_lambda, subln_g, qk_g, lam_init, heads_per_step):
    b, lq, _ = q.shape
    tq = min(ATTN_Q_TILE, lq)
    hw = 2 * A_DH
    wdt = heads_per_step * hw
    head_spec = lambda n: pl.BlockSpec((None, n, wdt), lambda bi, h: (bi, 0, h))
    in_specs = [_resident(a_lambda.shape), _resident((1, hw)), _resident(qk_g.shape), head_spec(lq)]
    args = [a_lambda, subln_g.reshape(1, hw), qk_g, q]
    lk = 0
    for k, v in kv_parts:
        in_specs += [head_spec(k.shape[1]), head_spec(v.shape[1])]
        args += [k, v]
        lk += k.shape[1]
    ones_rows = 16
    scratch = [pltpu.VMEM((lk, hw), BF16), pltpu.VMEM((hw + ones_rows, lk), BF16),
               pltpu.VMEM((2, lk, tq), F32), pltpu.VMEM((2, 2, lk, tq), BF16)]
    vmem = (2 * (2 * lq + 4 * lk) * wdt * 2 + 2 * lk * (hw + ones_rows) * 2 + 2 * lk * tq * 8
            + 6 * ATTN_KEY_CHUNK * tq * 4 + (8 << 20))
    return pl.pallas_call(
        functools.partial(_diff_attn_kernel, n_parts=len(kv_parts), lq=lq, tq=tq,
                          heads=heads_per_step, lam_init=lam_init),
        grid=(b, A_HEADS // heads_per_step),
        in_specs=in_specs,
        out_specs=head_spec(lq),
        out_shape=jax.ShapeDtypeStruct((b, lq, A_HEADS * hw), BF16),
        scratch_shapes=scratch,
        compiler_params=_params(2, vmem),
        name="diff_attn",
    )(*args)


def _tree_sum(terms):
    while len(terms) > 1:
        terms = [terms[i] + terms[i + 1] for i in range(0, len(terms), 2)]
    return terms[0]


def _pool_kernel(p_ref, w_ref, s_ref, o_ref, pad_ref, *, l, tp):
    wdt = p_ref.shape[-1]
    pad_ref[0:POOL_HALO, :] = jnp.zeros((POOL_HALO, wdt), F32)
    pad_ref[POOL_HALO + l:, :] = jnp.zeros((POOL_HALO, wdt), F32)
    pad_ref[POOL_HALO:POOL_HALO + l, :] = p_ref[...]
    n_tiles = l // tp
    for ti in range(n_tiles):
        t0 = ti * tp
        edge = ti == 0 or ti == n_tiles - 1
        if edge:
            t_row = t0 + lax.broadcasted_iota(jnp.int32, (tp, V7X_LANES), 0)
        for g, w in enumerate(POOL_WINDOWS):
            lanes = slice(g * V7X_LANES, (g + 1) * V7X_LANES)
            shifted = [pad_ref[POOL_HALO + t0 + j:POOL_HALO + t0 + j + tp, lanes]
                       for j in range(-(w // 2), w // 2)]
            centre = shifted[w // 2]
            total = _tree_sum(shifted)
            if edge:
                cnt = (jnp.minimum(t_row + w // 2, l) - jnp.maximum(t_row - w // 2, 0)).astype(F32)
                mean = total / cnt
            else:
                mean = total * (1.0 / w)
            y = _dot((mean - centre).astype(BF16), w_ref[g]) * s_ref[:, lanes]
            o_ref[t0:t0 + tp, lanes] = y.astype(o_ref.dtype)


def _pool(p, w_pool, s_pool, layer):
    b, l, wdt = p.shape
    tp = min(POOL_TILE, l)
    seq = pl.BlockSpec((None, l, wdt), lambda bi: (bi, 0, 0))
    return pl.pallas_call(
        functools.partial(_pool_kernel, l=l, tp=tp),
        grid=(b,),
        in_specs=[seq, _resident(w_pool.shape[1:], layer), _resident((1, wdt))],
        out_specs=seq,
        out_shape=jax.ShapeDtypeStruct((b, l, wdt), BF16),
        scratch_shapes=[pltpu.VMEM((l + 2 * POOL_HALO, wdt), F32)],
        compiler_params=_params(1, 5 * l * wdt * 4 + (16 << 20)),
        name="pool",
    )(p, w_pool, s_pool.reshape(1, wdt))


def _nbr_attn_kernel(*refs, n_rows, pairs, local):
    if local:
        qkg_ref, q_ref, k_ref, v_ref, kc_ref, vc_ref, bias_ref, o_ref = refs
    else:
        qkg_ref, q_ref, kc_ref, vc_ref, o_ref = refs
    gains = jnp.max(jnp.abs(qkg_ref[...]), axis=-1, keepdims=True)
    bound = gains[0:1] * gains[1:2] * (C_DH ** 0.5 * SCORE_BOUND_SLACK)
    if local:
        both = jnp.maximum(bias_ref[0], bias_ref[NA_ROWS - 1])
        bmax = jnp.max(jnp.max(both, axis=0, keepdims=True), axis=1, keepdims=True)
        bound = bound + jnp.maximum(bmax, 0.0)
    lane = lax.broadcasted_iota(jnp.int32, (1, V7X_LANES), 1)
    scale = C_DH ** -0.5
    m_first = jnp.where(lane < C_DH, scale, 0.0).astype(BF16)
    m_second = jnp.where(lane >= C_DH, scale, 0.0).astype(BF16)
    lane_out = lax.broadcasted_iota(jnp.int32, (GRID_W, V7X_LANES), 1)
    n_loc = NA_ROWS * GRID_W
    hq = 2 * GRID_W
    with_ones = lambda v: jnp.concatenate([v, jnp.ones_like(v)], axis=1)
    first_key_row = [min(max(r - NA_ROWS // 2, 0), n_rows - NA_ROWS) for r in range(n_rows)]

    def attend(cols, shift):
        blocks = []
        for r in range(n_rows):
            qr = q_ref[r * GRID_W:(r + 1) * GRID_W, cols]
            blocks += [qr * m_first, qr * m_second]
        qbd = jnp.concatenate(blocks, axis=0)
        s_ctx = _dot_nt(qbd, kc_ref[:, cols])
        if local:
            s_loc = jnp.concatenate(
                [_dot_nt(qbd[r * hq:(r + 1) * hq], k_ref[rs * GRID_W:rs * GRID_W + n_loc, cols])
                 + bias_ref[rs - r + NA_ROWS - 1] for r, rs in enumerate(first_key_row)], axis=0)
        if shift is None:
            shift = jnp.max(s_ctx, axis=-1, keepdims=True)
            if local:
                shift = jnp.maximum(shift, jnp.max(s_loc, axis=-1, keepdims=True))
        e_ctx = jnp.exp(s_ctx - shift)
        full = _dot(e_ctx.astype(BF16), with_ones(vc_ref[:, cols]))
        if local:
            e_loc = jnp.exp(s_loc - shift).astype(BF16)
            full = full + jnp.concatenate(
                [_dot(e_loc[r * hq:(r + 1) * hq], with_ones(v_ref[rs * GRID_W:rs * GRID_W + n_loc, cols]))
                 for r, rs in enumerate(first_key_row)], axis=0)
        denom = full[:, V7X_LANES:V7X_LANES + 1]
        full = full[:, :V7X_LANES] * (1.0 / denom)
        out = jnp.concatenate(
            [jnp.where(lane_out < C_DH, full[r * hq:r * hq + GRID_W], full[r * hq + GRID_W:(r + 1) * hq])
             for r in range(n_rows)], axis=0)
        o_ref[:, cols] = out.astype(o_ref.dtype)
        return jnp.min(denom, axis=0, keepdims=True)

    pair_cols = [slice(p * V7X_LANES, (p + 1) * V7X_LANES) for p in range(pairs)]
    smallest = functools.reduce(jnp.minimum, [attend(cols, bound) for cols in pair_cols])

    @pl.when(jnp.min(smallest) < DENOM_FLOOR)
    def _():
        for cols in pair_cols:
            attend(cols, None)


def _nbr_attn(q, k, v, kc, vc, bias_t, qk_g, pairs_per_step):
    b, lq, wdt = q.shape
    lc = kc.shape[1]
    n_rows = lq // GRID_W
    local = k is not None
    blk = pairs_per_step * V7X_LANES
    pair = lambda n: pl.BlockSpec((None, n, blk), lambda h, bi: (bi, 0, h))
    if local:
        assert pairs_per_step == 1
        in_specs = [_resident(qk_g.shape), pair(lq), pair(lq), pair(lq), pair(lc), pair(lc),
                    pl.BlockSpec((None,) + bias_t.shape[1:], lambda h, bi: (h, 0, 0, 0))]
        args = [qk_g, q, k, v, kc, vc, bias_t]
    else:
        in_specs = [_resident(qk_g.shape), pair(lq), pair(lc), pair(lc)]
        args = [qk_g, q, kc, vc]
    return pl.pallas_call(
        functools.partial(_nbr_attn_kernel, n_rows=n_rows, pairs=pairs_per_step, local=local),
        grid=(wdt // blk, b),
        in_specs=in_specs,
        out_specs=pair(lq),
        out_shape=jax.ShapeDtypeStruct((b, lq, wdt), BF16),
        compiler_params=_params(2, V7X_VMEM_LIMIT),
        name="nbr_attn" if local else "ctx_attn",
    )(*args)


N_ROW_OFFSETS = 2 * NA_ROWS - 1
ROW_OFFSETS_PAD = 16


def _nbr_bias_kernel(base_ref, mask_ref, o_ref):
    lane = lax.broadcasted_iota(jnp.int32, (GRID_W, V7X_LANES), 1)
    first_half = lane < GRID_W
    blocks = []
    for h in range(2):
        per_offset = []
        for dr in range(N_ROW_OFFSETS):
            row = base_ref[h * ROW_OFFSETS_PAD + dr:h * ROW_OFFSETS_PAD + dr + 1, :]
            low = pltpu.roll(jnp.broadcast_to(row, (GRID_W, V7X_LANES)), 0, 1, stride=1, stride_axis=0)
            per_offset.append((low, pltpu.roll(low, GRID_W, 1)))
        blocks.append(per_offset)
    for d0 in range(NA_ROWS):
        for h in range(2):
            rows = slice(h * GRID_W, (h + 1) * GRID_W)
            for c in range(NA_ROWS // 2):
                lanes = slice(c * V7X_LANES, (c + 1) * V7X_LANES)
                blk = jnp.where(first_half, blocks[h][d0 + 2 * c][0], blocks[h][d0 + 2 * c + 1][1])
                o_ref[d0, rows, lanes] = blk + mask_ref[rows, lanes]


def _nbr_bias_table(rpb):
    h, n_dr, n_dc = rpb.shape
    base = jnp.roll(jnp.pad(rpb, ((0, 0), (0, ROW_OFFSETS_PAD - n_dr), (0, V7X_LANES - n_dc))),
                    -(NA_COLS - 1), axis=-1).reshape(h // 2, 2 * ROW_OFFSETS_PAD, V7X_LANES)
    qc = np.arange(GRID_W)[:, None]
    kc = np.arange(GRID_W)[None, :]
    win0 = np.clip(qc - NA_COLS // 2, 0, GRID_W - NA_COLS)
    valid = (kc >= win0) & (kc < win0 + NA_COLS)
    mask = np.tile(np.where(valid, 0.0, -np.inf).astype(np.float32), (2, NA_ROWS))
    shape = (h // 2, NA_ROWS, 2 * GRID_W, NA_ROWS * GRID_W)
    return pl.pallas_call(
        _nbr_bias_kernel,
        grid=(h // 2,),
        in_specs=[pl.BlockSpec((None, 2 * ROW_OFFSETS_PAD, V7X_LANES), lambda p: (p, 0, 0)),
                  _resident(mask.shape)],
        out_specs=pl.BlockSpec((None,) + shape[1:], lambda p: (p, 0, 0, 0)),
        out_shape=jax.ShapeDtypeStruct(shape, F32),
        compiler_params=_params(1, 32 * 1024 * 1024),
        name="nbr_bias",
    )(base, jnp.asarray(mask))


def _merge_kernel(x_ref, sh_ref, sc_ref, g1_ref, ng_ref, ya_ref, yb_ref, yc_ref, yd_ref,
                  wg_ref, wb_ref, wo_ref, o_ref):
    x = x_ref[...]
    d = x.shape[-1]
    hb = _norm_modulate(x, ng_ref[...], sh_ref[...], sc_ref[...])
    parts = []
    for n0 in range(0, d, MERGE_COL_BLOCK):
        cols = slice(n0, n0 + MERGE_COL_BLOCK)
        acc = None
        for i, y_ref in enumerate((ya_ref, yb_ref, yc_ref, yd_ref)):
            gate = _dot(hb, wg_ref[:, i * d + n0:i * d + n0 + MERGE_COL_BLOCK])
            term = jax.nn.sigmoid(gate) * _dot(y_ref[...], wb_ref[i * SEG_W:(i + 1) * SEG_W, cols])
            acc = term if acc is None else acc + term
        parts.append(acc.astype(BF16))
    o_ref[...] = x + g1_ref[...] * _dot(jnp.concatenate(parts, axis=1), wo_ref[...])


def _merge(x, shift, scale, g1, norm_g, ys, w_gate, w_branch, w_out, layer):
    b, l, d = x.shape
    tm = min(ROW_TILE, l)
    batched = shift.shape[0] != 1
    mod_map = (lambda bi, i: (bi, 0, 0)) if batched else (lambda bi, i: (0, 0, 0))
    x_tile = pl.BlockSpec((None, tm, d), lambda bi, i: (bi, i, 0))
    y_tile = pl.BlockSpec((None, tm, SEG_W), lambda bi, i: (bi, i, 0))
    mod = pl.BlockSpec((None, 1, d), mod_map)
    weights = sum(math.prod(w.shape[1:]) for w in (w_gate, w_branch, w_out)) * 2
    vmem = weights + 2 * (2 * tm * d * 4 + 4 * tm * SEG_W * 2) + 8 * tm * d * 4 + (8 << 20)
    return pl.pallas_call(
        _merge_kernel,
        grid=(b, l // tm),
        in_specs=[x_tile, mod, mod, mod, _resident((1, d)), y_tile, y_tile, y_tile, y_tile,
                  _resident(w_gate.shape[1:], layer), _resident(w_branch.shape[1:], layer),
                  _resident(w_out.shape[1:], layer)],
        out_specs=x_tile,
        out_shape=jax.ShapeDtypeStruct(x.shape, F32),
        compiler_params=_params(2, vmem),
        name="merge",
    )(x, shift, scale, g1, norm_g, *ys, w_gate, w_branch, w_out)


def _ffn_kernel(x_ref, sh_ref, sc_ref, g2_ref, ng_ref, wgu_ref, wd_ref, o_ref, u_ref):
    x = x_ref[...]
    hb = _norm_modulate(x, ng_ref[...], sh_ref[...], sc_ref[...])
    hidden = wd_ref.shape[0]
    for c in range(hidden // FFN_CHUNK):
        cols = slice(c * FFN_CHUNK, (c + 1) * FFN_CHUNK)
        a = _dot(hb, wgu_ref[:, cols])
        bb = _dot(hb, wgu_ref[:, hidden + c * FFN_CHUNK:hidden + (c + 1) * FFN_CHUNK])
        u_ref[:, cols] = (a * jax.nn.sigmoid(a) * bb).astype(BF16)
    o_ref[...] = x + g2_ref[...] * _dot(u_ref[...], wd_ref[...])


def _ffn(x, shift, scale, g2, norm_g, w_gu, w_down, layer):
    b, l, d = x.shape
    tm = min(ROW_TILE, l)
    hidden = w_down.shape[1]
    batched = shift.shape[0] != 1
    mod_map = (lambda bi, i: (bi, 0, 0)) if batched else (lambda bi, i: (0, 0, 0))
    x_tile = pl.BlockSpec((None, tm, d), lambda bi, i: (bi, i, 0))
    mod = pl.BlockSpec((None, 1, d), mod_map)
    vmem = (math.prod(w_gu.shape[1:]) + math.prod(w_down.shape[1:])) * 2 + 4 * tm * d * 4 + tm * hidden * 2 + 8 * tm * d * 4 + (8 << 20)
    return pl.pallas_call(
        _ffn_kernel,
        grid=(b, l // tm),
        in_specs=[x_tile, mod, mod, mod, _resident((1, d)),
                  _resident(w_gu.shape[1:], layer), _resident(w_down.shape[1:], layer)],
        out_specs=x_tile,
        out_shape=jax.ShapeDtypeStruct(x.shape, F32),
        scratch_shapes=[pltpu.VMEM((tm, hidden), BF16)],
        compiler_params=_params(2, vmem),
        name="ffn",
    )(x, shift, scale, g2, norm_g, w_gu, w_down)


def _rope_tables(n_tok):
    nf = A_DH // 4
    t = np.arange(n_tok)
    row = (t // GRID_W).astype(np.float32)
    col = (t % GRID_W).astype(np.float32)
    inv = np.float32(ROPE_THETA) ** (-np.arange(nf, dtype=np.float32) / np.float32(nf))
    ar = row[:, None] * inv
    ac = col[:, None] * inv
    ang = np.concatenate([ar, ar, ac, ac], axis=-1).astype(np.float64)
    cos, sin = np.cos(ang), np.sin(ang)
    quarter = (np.arange(A_DH) // nf) % 2
    sin_up = np.where(quarter == 0, -sin, 0.0)
    sin_dn = np.where(quarter == 1, sin, 0.0)
    rep = V7X_LANES // A_DH
    return tuple(jnp.asarray(np.tile(a, (1, rep)), dtype=F32) for a in (cos, sin_up, sin_dn))


def kernel(x, c, ctx, c_ctx, w_mod, b_mod, norm1_g, w_in, a_qk_g, a_lambda, a_subln_g, b_pool_w,
           b_pool_s, c_qk_g, c_rpb, d_vn_g, d_ws, d_bs, w_branch, w_out, norm2_g, w_gu, w_down):
    b, s, d = x.shape
    depth = w_mod.shape[0]
    rope_tabs = _rope_tables(s)

    rows = -(-(b + 1) // 8) * 8
    c_all = jnp.zeros((rows, d), F32).at[:b].set(c).at[b].set(c_ctx)
    mod = _modulation(c_all, w_mod, b_mod)

    w_mix = w_in[:, :, :MIX_COLS].astype(BF16)
    w_gate = w_in[:, :, MIX_COLS:].astype(BF16)
    w_br = w_branch.astype(BF16)
    w_o = w_out.astype(BF16)
    w_gu_b = w_gu.astype(BF16)
    w_dn = w_down.astype(BF16)
    w_pool = b_pool_w.astype(BF16)
    w_s = d_ws.astype(BF16)

    for l in range(depth):
        last = l == depth - 1
        lam_init = 0.8 - 0.6 * math.exp(-0.3 * l)
        mx = [mod[l, :b, k * d:(k + 1) * d].reshape(b, 1, d) for k in range(6)]
        mc = [mod[l, b:b + 1, k * d:(k + 1) * d].reshape(1, 1, d) for k in range(6)]
        n1 = norm1_g[l].reshape(1, d)
        n2 = norm2_g[l].reshape(1, d)
        rep = SEG_W // A_DH
        qk_gains = jnp.stack([jnp.tile(a_qk_g[l, 0], rep), jnp.tile(a_qk_g[l, 1], rep),
                              jnp.tile(c_qk_g[l, 0], rep), jnp.tile(c_qk_g[l, 1], rep)])
        vn_g = d_vn_g[l].reshape(1, SEG_W)
        bs_full = jnp.repeat(d_bs[l].T, V7X_LANES, axis=1)
        bias_t = _nbr_bias_table(c_rpb[l])

        px = _in_proj(x, mx[0], mx[1], n1, w_mix, qk_gains, vn_g, w_s, bs_full, rope_tabs, MIXER_SEGS, l)
        lc = ctx.shape[1]
        flat = lambda t: t.reshape(1, b * lc, t.shape[-1])
        pc = _in_proj(flat(ctx), mc[0], mc[1], n1, w_mix, qk_gains, vn_g, w_s, bs_full, None,
                      CTX_KV_SEGS if last else MIXER_SEGS, l)
        pc = {seg: t.reshape(b, lc, SEG_W) for seg, t in pc.items()}

        ya = _diff_attn(px[SEG_AQ], [(px[SEG_AK], px[SEG_AV]), (pc[SEG_AK], pc[SEG_AV])],
                        a_lambda[l], a_subln_g[l], a_qk_g[l], lam_init, heads_per_step=1)
        yb = _pool(px[SEG_B], w_pool, b_pool_s[l], l)
        yc = _nbr_attn(px[SEG_CQ], px[SEG_CK], px[SEG_CV], pc[SEG_CK], pc[SEG_CV], bias_t, c_qk_g[l],
                       pairs_per_step=1)
        x = _merge(x, mx[0], mx[1], mx[2], n1, (ya, yb, yc, px[SEG_YD]), w_gate, w_br, w_o, l)
        x = _ffn(x, mx[3], mx[4], mx[5], n2, w_gu_b, w_dn, l)

        if not last:
            ya_c = _diff_attn(pc[SEG_AQ], [(pc[SEG_AK], pc[SEG_AV])], a_lambda[l], a_subln_g[l], a_qk_g[l], lam_init,
                              heads_per_step=A_HEADS)
            yb_c = _pool(pc[SEG_B], w_pool, b_pool_s[l], l)
            yc_c = _nbr_attn(pc[SEG_CQ], None, None, pc[SEG_CK], pc[SEG_CV], None, c_qk_g[l],
                             pairs_per_step=SEG_W // V7X_LANES)
            ys_c = tuple(flat(t) for t in (ya_c, yb_c, yc_c, pc[SEG_YD]))
            ctx_f = _merge(flat(ctx), mc[0], mc[1], mc[2], n1, ys_c, w_gate, w_br, w_o, l)
            ctx = _ffn(ctx_f, mc[3], mc[4], mc[5], n2, w_gu_b, w_dn, l).reshape(b, lc, d)
    return x
```

```python
import functools
import math

import jax
import jax.numpy as jnp
import numpy as np
from jax import lax
from jax.experimental import pallas as pl
from jax.experimental.pallas import tpu as pltpu

F32 = jnp.float32
BF16 = jnp.bfloat16

D_MODEL = 1024
GRID_W = 64
EPS = 1e-6
ROPE_THETA = 10000.0
A_HEADS = 4
A_DH = 64
POOL_WINDOWS = (2, 4, 8, 16)
C_DH = 64
NA_ROWS = 8
NA_COLS = 16
CHUNK = 128
N_BRANCH = 4
SEG_W = 512
N_SEG = 9
MIX_COLS = N_SEG * SEG_W
FFN_HIDDEN = 2816
SEG_AQ, SEG_AK, SEG_AV, SEG_B, SEG_CQ, SEG_CK, SEG_CV, SEG_DU, SEG_DV = range(N_SEG)
SEG_YD = N_SEG
SEG_DTYPES = (BF16, BF16, BF16, F32, BF16, BF16, BF16, F32, BF16, BF16)
MIXER_SEGS = (SEG_AQ, SEG_AK, SEG_AV, SEG_B, SEG_CQ, SEG_CK, SEG_CV, SEG_YD)
CTX_KV_SEGS = (SEG_AK, SEG_AV, SEG_CK, SEG_CV)

V7X_LANES = 128
V7X_VMEM_BYTES = 64 * 1024 * 1024
V7X_VMEM_LIMIT = V7X_VMEM_BYTES - 8 * 1024 * 1024

ROW_TILE = 1024
ATTN_Q_TILE = 512
ATTN_KEY_CHUNK = 256
SCORE_BOUND_SLACK = 1.02
DENOM_FLOOR = 1e-18
POOL_TILE = 256
POOL_HALO = 8
FFN_CHUNK = 256
MERGE_COL_BLOCK = 256


def _params(n_axes, vmem_bytes):
    return pltpu.CompilerParams(
        dimension_semantics=("parallel",) * n_axes,
        vmem_limit_bytes=int(min(vmem_bytes, V7X_VMEM_LIMIT)))


def _resident(shape, layer=None):
    zeros = (0,) * len(shape)
    if layer is None:
        return pl.BlockSpec(shape, lambda *_: zeros, pipeline_mode=pl.Buffered(1))
    return pl.BlockSpec((None,) + tuple(shape), lambda *_: (layer,) + zeros, pipeline_mode=pl.Buffered(1))


def _dot(a, b):
    return jnp.dot(a, b, preferred_element_type=F32)


def _dot_nt(a, b):
    return lax.dot_general(a, b, (((1,), (1,)), ((), ())), preferred_element_type=F32)


def _dot_tn(a, b):
    return lax.dot_general(a, b, (((0,), (0,)), ((), ())), preferred_element_type=F32)


def _split_bf16(t):
    hi = t.astype(BF16)
    lo = (t - hi.astype(F32)).astype(BF16)
    return hi, lo


def _norm_modulate(x, gain, shift, scale):
    ms = jnp.mean(x * x, axis=-1, keepdims=True)
    n = x * lax.rsqrt(ms + EPS)
    return ((n * gain) * (1.0 + scale) + shift).astype(BF16)


def _mod_kernel(c_ref, w_ref, b_ref, o_ref):
    a = c_ref[...]
    a = a * jax.nn.sigmoid(a)
    a_hi, a_lo = _split_bf16(a)
    w_hi, w_lo = _split_bf16(w_ref[...])
    o_ref[...] = _dot(a_hi, w_hi) + _dot(a_hi, w_lo) + _dot(a_lo, w_hi) + b_ref[...]


def _modulation(c_all, w_mod, b_mod):
    depth, d, n = w_mod.shape
    rows = c_all.shape[0]
    tn = 768
    return pl.pallas_call(
        _mod_kernel,
        grid=(depth, n // tn),
        in_specs=[
            pl.BlockSpec((rows, d), lambda l, j: (0, 0)),
            pl.BlockSpec((None, d, tn), lambda l, j: (l, 0, j)),
            pl.BlockSpec((None, 1, tn), lambda l, j: (l, 0, j)),
        ],
        out_specs=pl.BlockSpec((None, rows, tn), lambda l, j: (l, 0, j)),
        out_shape=jax.ShapeDtypeStruct((depth, rows, n), F32),
        compiler_params=_params(2, 32 * 1024 * 1024),
        name="modulation",
    )(c_all, w_mod, b_mod.reshape(depth, 1, n))


def _half_block_rms(t, first_half, gain):
    sq = t * t
    s_first = jnp.sum(jnp.where(first_half, sq, 0.0), axis=-1, keepdims=True)
    s_second = jnp.sum(jnp.where(first_half, 0.0, sq), axis=-1, keepdims=True)
    ms = jnp.where(first_half, s_first, s_second) * (1.0 / A_DH)
    return t * lax.rsqrt(ms + EPS) * gain


def _rope_block(t, cos, sin_up, sin_dn):
    return (t * cos + pltpu.roll(t, V7X_LANES - 16, 1) * sin_up + pltpu.roll(t, 16, 1) * sin_dn)


def _in_proj_kernel(*refs, rope, out_segs):
    x_ref, sh_ref, sc_ref, ng_ref, w_ref, qkg_ref, vng_ref, ws_ref, bs_ref = refs[:9]
    pos = 9
    if rope:
        cos_ref, sup_ref, sdn_ref = refs[9:12]
        pos = 12
    outs = dict(zip(out_segs, refs[pos:pos + len(out_segs)]))
    gated = SEG_YD in outs
    needed = set(out_segs) | ({SEG_DU, SEG_DV} if gated else set())
    hb = _norm_modulate(x_ref[...], ng_ref[...], sh_ref[...], sc_ref[...])
    qk_row = {SEG_AQ: 0, SEG_AK: 1, SEG_CQ: 2, SEG_CK: 3}
    first_half = lax.broadcasted_iota(jnp.int32, (1, V7X_LANES), 1) < A_DH
    for seg in range(N_SEG):
        if seg not in needed:
            continue
        r = _dot(hb, w_ref[:, seg * SEG_W:(seg + 1) * SEG_W])
        if seg in qk_row:
            row = qk_row[seg]
            blocks = []
            for j in range(SEG_W // V7X_LANES):
                lanes = slice(j * V7X_LANES, (j + 1) * V7X_LANES)
                t = _half_block_rms(r[:, lanes], first_half, qkg_ref[row:row + 1, lanes])
                if rope and seg in (SEG_AQ, SEG_AK):
                    t = _rope_block(t, cos_ref[...], sup_ref[...], sdn_ref[...])
                blocks.append(t)
            r = jnp.concatenate(blocks, axis=1)
        elif seg == SEG_DU and gated:
            gate_u = r
        elif seg == SEG_DV:
            ms = jnp.mean(r * r, axis=-1, keepdims=True)
            r = r * lax.rsqrt(ms + EPS) * vng_ref[...]
            if gated:
                vb = r.astype(BF16)
                n_chunks = r.shape[0] // CHUNK
                for g in range(ws_ref.shape[0]):
                    lanes = slice(g * V7X_LANES, (g + 1) * V7X_LANES)
                    v_wide = jnp.concatenate([vb[n * CHUNK:(n + 1) * CHUNK, lanes] for n in range(n_chunks)], axis=1)
                    sv_wide = _dot(ws_ref[g], v_wide)
                    for n in range(n_chunks):
                        rows = slice(n * CHUNK, (n + 1) * CHUNK)
                        sv = sv_wide[:, n * V7X_LANES:(n + 1) * V7X_LANES] + bs_ref[:, lanes]
                        outs[SEG_YD][rows, lanes] = (gate_u[rows, lanes] * sv).astype(outs[SEG_YD].dtype)
        if seg in outs:
            outs[seg][...] = r.astype(outs[seg].dtype)


def _in_proj(x, shift, scale, norm_g, w_mix, qk_gains, vn_g, w_s, b_full, rope_tabs, out_segs, layer):
    b, l, d = x.shape
    tm = min(ROW_TILE, l)
    batched = shift.shape[0] != 1
    mod_map = (lambda bi, i: (bi, 0, 0)) if batched else (lambda bi, i: (0, 0, 0))
    rope = rope_tabs is not None
    in_specs = [
        pl.BlockSpec((None, tm, d), lambda bi, i: (bi, i, 0)),
        pl.BlockSpec((None, 1, d), mod_map),
        pl.BlockSpec((None, 1, d), mod_map),
        _resident((1, d)),
        _resident((d, MIX_COLS), layer),
        _resident((4, SEG_W)),
        _resident((1, SEG_W)),
        _resident(w_s.shape[1:], layer),
        _resident(b_full.shape),
    ]
    args = [x, shift, scale, norm_g, w_mix, qk_gains, vn_g, w_s, b_full]
    if rope:
        in_specs += [pl.BlockSpec((tm, V7X_LANES), lambda bi, i: (i, 0))] * 3
        args += list(rope_tabs)
    dtypes = [SEG_DTYPES[s] for s in out_segs]
    out_bytes = sum(tm * SEG_W * jnp.dtype(t).itemsize for t in dtypes)
    vmem = d * MIX_COLS * 2 + 2 * (tm * d * 4 + out_bytes) + 8 * tm * SEG_W * 4 + (8 << 20)
    outs = pl.pallas_call(
        functools.partial(_in_proj_kernel, rope=rope, out_segs=tuple(out_segs)),
        grid=(b, l // tm),
        in_specs=in_specs,
        out_specs=[pl.BlockSpec((None, tm, SEG_W), lambda bi, i: (bi, i, 0))] * len(out_segs),
        out_shape=[jax.ShapeDtypeStruct((b, l, SEG_W), t) for t in dtypes],
        compiler_params=_params(2, vmem),
        name="in_proj_rope" if rope else "in_proj",
    )(*args)
    return dict(zip(out_segs, outs))


def _diff_attn_kernel(*refs, n_parts, lq, tq, heads, lam_init):
    al_ref, sg_ref, qkg_ref, q_ref = refs[:4]
    kv = refs[4:4 + 2 * n_parts]
    o_ref, kcat_banks, vt_banks, s_banks, e_banks = refs[4 + 2 * n_parts:]
    kcat_ref, vt_ref, s_ref, e_ref = (r.at[0] for r in (kcat_banks, vt_banks, s_banks, e_banks))
    al = al_ref[...]
    lam = (jnp.exp(jnp.sum(al[0:1] * al[1:2], axis=-1, keepdims=True))
           - jnp.exp(jnp.sum(al[2:3] * al[3:4], axis=-1, keepdims=True)) + lam_init)
    lane = lax.broadcasted_iota(jnp.int32, (1, V7X_LANES), 1)
    scale = A_DH ** -0.5
    masks = (jnp.where(lane < A_DH, scale, 0.0).astype(BF16),
             jnp.where(lane >= A_DH, scale, 0.0).astype(BF16))
    sub_gain = sg_ref[...] * (1.0 - lam_init)
    gains = jnp.max(jnp.abs(qkg_ref[...]), axis=-1, keepdims=True)
    bound = gains[0:1] * gains[1:2] * (A_DH * scale * SCORE_BOUND_SLACK)
    hw = 2 * A_DH
    n_keys = kcat_ref.shape[0]
    kc = math.gcd(n_keys, ATTN_KEY_CHUNK)
    chunks = range(0, n_keys, kc)
    n_tiles = lq // tq
    for bank in range(vt_banks.shape[0]):
        vt_banks[bank, hw:, :] = jnp.ones((vt_ref.shape[0] - hw, n_keys), BF16)

    def finish(rows, cols, acc):
        o = (acc[0][:hw] * (1.0 / acc[0][hw:hw + 1]) - acc[1][:hw] * (lam / acc[1][hw:hw + 1])).T
        ms = jnp.mean(o * o, axis=-1, keepdims=True)
        o_ref[rows, cols] = (o * lax.rsqrt(ms + EPS) * sub_gain).astype(o_ref.dtype)

    def bounded_tile(cols, slot, r0):
        qc = q_ref[r0:r0 + tq, cols]
        acc = []
        for c, mask in enumerate(masks):
            qm = qc * mask
            for k0 in chunks:
                s = _dot_nt(kcat_ref[k0:k0 + kc, :], qm)
                e_ref[slot, c, k0:k0 + kc, :] = jnp.exp(s - bound).astype(BF16)
            acc.append(_dot(vt_ref[...], e_ref[slot, c]))
        finish(slice(r0, r0 + tq), cols, acc)
        return jnp.minimum(acc[0][hw:hw + 1], acc[1][hw:hw + 1])

    def exact_tile(cols, r0):
        qc = q_ref[pl.ds(r0, tq), cols]
        acc = []
        for c, mask in enumerate(masks):
            qm = qc * mask
            m8 = None
            for k0 in chunks:
                s = _dot_nt(kcat_ref[k0:k0 + kc, :], qm)
                s_ref[c, k0:k0 + kc, :] = s
                cm = jnp.max(s.reshape(kc // 8, 8, tq), axis=0)
                m8 = cm if m8 is None else jnp.maximum(m8, cm)
            m = jnp.max(m8, axis=0, keepdims=True)
            for k0 in chunks:
                rows = slice(k0, k0 + kc)
                e_ref[0, c, rows, :] = jnp.exp(s_ref[c, rows, :] - m).astype(BF16)
            acc.append(_dot(vt_ref[...], e_ref[0, c]))
        finish(pl.ds(r0, tq), cols, acc)

    def load_head(cols):
        off = 0
        for p in range(n_parts):
            n = kv[2 * p].shape[0]
            kcat_ref[off:off + n, :] = kv[2 * p][:, cols]
            vt_ref[0:hw, off:off + n] = kv[2 * p + 1][:, cols].astype(F32).T.astype(BF16)
            off += n

    if n_tiles == 1:
        head_cols = [slice(h * hw, (h + 1) * hw) for h in range(heads)]
        bank = lambda h: tuple(r.at[h] for r in (kcat_banks, vt_banks, s_banks, e_banks))
        for h, cols in enumerate(head_cols):
            kcat_ref, vt_ref, s_ref, e_ref = bank(h)
            load_head(cols)
        maxima = {}
        for h, cols in enumerate(head_cols):
            kcat_ref, vt_ref, s_ref, e_ref = bank(h)
            qc = q_ref[0:tq, cols]
            for c, mask in enumerate(masks):
                qm = qc * mask
                m8 = None
                for k0 in chunks:
                    s = _dot_nt(kcat_ref[k0:k0 + kc, :], qm)
                    s_ref[c, k0:k0 + kc, :] = s
                    cm = jnp.max(s.reshape(kc // 8, 8, tq), axis=0)
                    m8 = cm if m8 is None else jnp.maximum(m8, cm)
                maxima[h, c] = jnp.max(m8, axis=0, keepdims=True)
        for h in range(heads):
            kcat_ref, vt_ref, s_ref, e_ref = bank(h)
            for c in range(2):
                for k0 in chunks:
                    rows = slice(k0, k0 + kc)
                    e_ref[0, c, rows, :] = jnp.exp(s_ref[c, rows, :] - maxima[h, c]).astype(BF16)
        for h, cols in enumerate(head_cols):
            kcat_ref, vt_ref, s_ref, e_ref = bank(h)
            finish(slice(0, tq), cols, [_dot(vt_ref[...], e_ref[0, c]) for c in range(2)])
        return

    for h in range(heads):
        cols = slice(h * hw, (h + 1) * hw)
        load_head(cols)
        smallest = None
        for j in range(n_tiles):
            denom = bounded_tile(cols, j % 2, j * tq)
            smallest = denom if smallest is None else jnp.minimum(smallest, denom)

        @pl.when(jnp.min(smallest) < DENOM_FLOOR)
        def _():
            def redo(j, carry):
                exact_tile(cols, pl.multiple_of(j * tq, tq))
                return carry
            lax.fori_loop(0, n_tiles, redo, 0)


def _diff_attn(q, kv_parts, a_lambda, subln_g, qk_g, lam_init, heads_per_step):
    b, lq, _ = q.shape
    tq = min(ATTN_Q_TILE, lq)
    hw = 2 * A_DH
    wdt = heads_per_step * hw
    head_spec = lambda n: pl.BlockSpec((None, n, wdt), lambda bi, h: (bi, 0, h))
    in_specs = [_resident(a_lambda.shape), _resident((1, hw)), _resident(qk_g.shape), head_spec(lq)]
    args = [a_lambda, subln_g.reshape(1, hw), qk_g, q]
    lk = 0
    for k, v in kv_parts:
        in_specs += [head_spec(k.shape[1]), head_spec(v.shape[1])]
        args += [k, v]
        lk += k.shape[1]
    ones_rows = 16
    banks = heads_per_step if lq == tq else 1
    scratch = [pltpu.VMEM((banks, lk, hw), BF16), pltpu.VMEM((banks, hw + ones_rows, lk), BF16),
               pltpu.VMEM((banks, 2, lk, tq), F32), pltpu.VMEM((banks, 2, 2, lk, tq), BF16)]
    vmem = (2 * (2 * lq + 4 * lk) * wdt * 2 + 2 * lk * (hw + ones_rows) * 2 + 2 * lk * tq * 8
            + 6 * ATTN_KEY_CHUNK * tq * 4 + (8 << 20))
    return pl.pallas_call(
        functools.partial(_diff_attn_kernel, n_parts=len(kv_parts), lq=lq, tq=tq,
                          heads=heads_per_step, lam_init=lam_init),
        grid=(b, A_HEADS // heads_per_step),
        in_specs=in_specs,
        out_specs=head_spec(lq),
        out_shape=jax.ShapeDtypeStruct((b, lq, A_HEADS * hw), BF16),
        scratch_shapes=scratch,
        compiler_params=_params(2, vmem),
        name="diff_attn",
    )(*args)


def _tree_sum(terms):
    while len(terms) > 1:
        terms = [terms[i] + terms[i + 1] for i in range(0, len(terms), 2)]
    return terms[0]


def _pool_kernel(p_ref, w_ref, s_ref, o_ref, pad_ref, *, l, tp):
    wdt = p_ref.shape[-1]
    pad_ref[0:POOL_HALO, :] = jnp.zeros((POOL_HALO, wdt), F32)
    pad_ref[POOL_HALO + l:, :] = jnp.zeros((POOL_HALO, wdt), F32)
    pad_ref[POOL_HALO:POOL_HALO + l, :] = p_ref[...]
    n_tiles = l // tp
    for ti in range(n_tiles):
        t0 = ti * tp
        edge = ti == 0 or ti == n_tiles - 1
        if edge:
            t_row = t0 + lax.broadcasted_iota(jnp.int32, (tp, V7X_LANES), 0)
        for g, w in enumerate(POOL_WINDOWS):
            lanes = slice(g * V7X_LANES, (g + 1) * V7X_LANES)
            shifted = [pad_ref[POOL_HALO + t0 + j:POOL_HALO + t0 + j + tp, lanes]
                       for j in range(-(w // 2), w // 2)]
            centre = shifted[w // 2]
            total = _tree_sum(shifted)
            if edge:
                cnt = (jnp.minimum(t_row + w // 2, l) - jnp.maximum(t_row - w // 2, 0)).astype(F32)
                mean = total / cnt
            else:
                mean = total * (1.0 / w)
            y = _dot((mean - centre).astype(BF16), w_ref[g]) * s_ref[:, lanes]
            o_ref[t0:t0 + tp, lanes] = y.astype(o_ref.dtype)


def _pool(p, w_pool, s_pool, layer):
    b, l, wdt = p.shape
    tp = min(POOL_TILE, l)
    seq = pl.BlockSpec((None, l, wdt), lambda bi: (bi, 0, 0))
    return pl.pallas_call(
        functools.partial(_pool_kernel, l=l, tp=tp),
        grid=(b,),
        in_specs=[seq, _resident(w_pool.shape[1:], layer), _resident((1, wdt))],
        out_specs=seq,
        out_shape=jax.ShapeDtypeStruct((b, l, wdt), BF16),
        scratch_shapes=[pltpu.VMEM((l + 2 * POOL_HALO, wdt), F32)],
        compiler_params=_params(1, 5 * l * wdt * 4 + (16 << 20)),
        name="pool",
    )(p, w_pool, s_pool.reshape(1, wdt))


def _nbr_attn_kernel(*refs, n_rows, pairs, local):
    if local:
        qkg_ref, q_ref, k_ref, v_ref, kc_ref, vc_ref, bias_ref, o_ref = refs
    else:
        qkg_ref, q_ref, kc_ref, vc_ref, o_ref = refs
    gains = jnp.max(jnp.abs(qkg_ref[...]), axis=-1, keepdims=True)
    bound = gains[0:1] * gains[1:2] * (C_DH ** 0.5 * SCORE_BOUND_SLACK)
    if local:
        both = jnp.maximum(bias_ref[0], bias_ref[NA_ROWS - 1])
        bmax = jnp.max(jnp.max(both, axis=0, keepdims=True), axis=1, keepdims=True)
        bound = bound + jnp.maximum(bmax, 0.0)
    lane = lax.broadcasted_iota(jnp.int32, (1, V7X_LANES), 1)
    scale = C_DH ** -0.5
    m_first = jnp.where(lane < C_DH, scale, 0.0).astype(BF16)
    m_second = jnp.where(lane >= C_DH, scale, 0.0).astype(BF16)
    lane_out = lax.broadcasted_iota(jnp.int32, (GRID_W, V7X_LANES), 1)
    n_loc = NA_ROWS * GRID_W
    hq = 2 * GRID_W
    with_ones = lambda v: jnp.concatenate([v, jnp.ones_like(v)], axis=1)
    first_key_row = [min(max(r - NA_ROWS // 2, 0), n_rows - NA_ROWS) for r in range(n_rows)]

    def attend(cols, shift):
        blocks = []
        for r in range(n_rows):
            qr = q_ref[r * GRID_W:(r + 1) * GRID_W, cols]
            blocks += [qr * m_first, qr * m_second]
        qbd = jnp.concatenate(blocks, axis=0)
        s_ctx = _dot_nt(qbd, kc_ref[:, cols])
        if local:
            s_loc = jnp.concatenate(
                [_dot_nt(qbd[r * hq:(r + 1) * hq], k_ref[rs * GRID_W:rs * GRID_W + n_loc, cols])
                 + bias_ref[rs - r + NA_ROWS - 1] for r, rs in enumerate(first_key_row)], axis=0)
        if shift is None:
            shift = jnp.max(s_ctx, axis=-1, keepdims=True)
            if local:
                shift = jnp.maximum(shift, jnp.max(s_loc, axis=-1, keepdims=True))
        e_ctx = jnp.exp(s_ctx - shift)
        full = _dot(e_ctx.astype(BF16), with_ones(vc_ref[:, cols]))
        if local:
            e_loc = jnp.exp(s_loc - shift).astype(BF16)
            full = full + jnp.concatenate(
                [_dot(e_loc[r * hq:(r + 1) * hq], with_ones(v_ref[rs * GRID_W:rs * GRID_W + n_loc, cols]))
                 for r, rs in enumerate(first_key_row)], axis=0)
        denom = full[:, V7X_LANES:V7X_LANES + 1]
        full = full[:, :V7X_LANES] * (1.0 / denom)
        out = jnp.concatenate(
            [jnp.where(lane_out < C_DH, full[r * hq:r * hq + GRID_W], full[r * hq + GRID_W:(r + 1) * hq])
             for r in range(n_rows)], axis=0)
        o_ref[:, cols] = out.astype(o_ref.dtype)
        return jnp.min(denom, axis=0, keepdims=True)

    pair_cols = [slice(p * V7X_LANES, (p + 1) * V7X_LANES) for p in range(pairs)]
    smallest = functools.reduce(jnp.minimum, [attend(cols, bound) for cols in pair_cols])

    @pl.when(jnp.min(smallest) < DENOM_FLOOR)
    def _():
        for cols in pair_cols:
            attend(cols, None)


def _nbr_attn(q, k, v, kc, vc, bias_t, qk_g, pairs_per_step):
    b, lq, wdt = q.shape
    lc = kc.shape[1]
    n_rows = lq // GRID_W
    local = k is not None
    blk = pairs_per_step * V7X_LANES
    pair = lambda n: pl.BlockSpec((None, n, blk), lambda h, bi: (bi, 0, h))
    if local:
        assert pairs_per_step == 1
        in_specs = [_resident(qk_g.shape), pair(lq), pair(lq), pair(lq), pair(lc), pair(lc),
                    pl.BlockSpec((None,) + bias_t.shape[1:], lambda h, bi: (h, 0, 0, 0))]
        args = [qk_g, q, k, v, kc, vc, bias_t]
    else:
        in_specs = [_resident(qk_g.shape), pair(lq), pair(lc), pair(lc)]
        args = [qk_g, q, kc, vc]
    return pl.pallas_call(
        functools.partial(_nbr_attn_kernel, n_rows=n_rows, pairs=pairs_per_step, local=local),
        grid=(wdt // blk, b),
        in_specs=in_specs,
        out_specs=pair(lq),
        out_shape=jax.ShapeDtypeStruct((b, lq, wdt), BF16),
        compiler_params=_params(2, V7X_VMEM_LIMIT),
        name="nbr_attn" if local else "ctx_attn",
    )(*args)


N_ROW_OFFSETS = 2 * NA_ROWS - 1
ROW_OFFSETS_PAD = 16


def _nbr_bias_kernel(base_ref, mask_ref, o_ref):
    lane = lax.broadcasted_iota(jnp.int32, (GRID_W, V7X_LANES), 1)
    first_half = lane < GRID_W
    blocks = []
    for h in range(2):
        per_offset = []
        for dr in range(N_ROW_OFFSETS):
            row = base_ref[h * ROW_OFFSETS_PAD + dr:h * ROW_OFFSETS_PAD + dr + 1, :]
            low = pltpu.roll(jnp.broadcast_to(row, (GRID_W, V7X_LANES)), 0, 1, stride=1, stride_axis=0)
            per_offset.append((low, pltpu.roll(low, GRID_W, 1)))
        blocks.append(per_offset)
    for d0 in range(NA_ROWS):
        for h in range(2):
            rows = slice(h * GRID_W, (h + 1) * GRID_W)
            for c in range(NA_ROWS // 2):
                lanes = slice(c * V7X_LANES, (c + 1) * V7X_LANES)
                blk = jnp.where(first_half, blocks[h][d0 + 2 * c][0], blocks[h][d0 + 2 * c + 1][1])
                o_ref[d0, rows, lanes] = blk + mask_ref[rows, lanes]


def _nbr_bias_table(rpb):
    h, n_dr, n_dc = rpb.shape
    base = jnp.roll(jnp.pad(rpb, ((0, 0), (0, ROW_OFFSETS_PAD - n_dr), (0, V7X_LANES - n_dc))),
                    -(NA_COLS - 1), axis=-1).reshape(h // 2, 2 * ROW_OFFSETS_PAD, V7X_LANES)
    qc = np.arange(GRID_W)[:, None]
    kc = np.arange(GRID_W)[None, :]
    win0 = np.clip(qc - NA_COLS // 2, 0, GRID_W - NA_COLS)
    valid = (kc >= win0) & (kc < win0 + NA_COLS)
    mask = np.tile(np.where(valid, 0.0, -np.inf).astype(np.float32), (2, NA_ROWS))
    shape = (h // 2, NA_ROWS, 2 * GRID_W, NA_ROWS * GRID_W)
    return pl.pallas_call(
        _nbr_bias_kernel,
        grid=(h // 2,),
        in_specs=[pl.BlockSpec((None, 2 * ROW_OFFSETS_PAD, V7X_LANES), lambda p: (p, 0, 0)),
                  _resident(mask.shape)],
        out_specs=pl.BlockSpec((None,) + shape[1:], lambda p: (p, 0, 0, 0)),
        out_shape=jax.ShapeDtypeStruct(shape, F32),
        compiler_params=_params(1, 32 * 1024 * 1024),
        name="nbr_bias",
    )(base, jnp.asarray(mask))


def _merge_kernel(x_ref, sh_ref, sc_ref, g1_ref, ng_ref, ya_ref, yb_ref, yc_ref, yd_ref,
                  wg_ref, wb_ref, wo_ref, o_ref):
    x = x_ref[...]
    d = x.shape[-1]
    hb = _norm_modulate(x, ng_ref[...], sh_ref[...], sc_ref[...])
    parts = []
    for n0 in range(0, d, MERGE_COL_BLOCK):
        cols = slice(n0, n0 + MERGE_COL_BLOCK)
        acc = None
        for i, y_ref in enumerate((ya_ref, yb_ref, yc_ref, yd_ref)):
            gate = _dot(hb, wg_ref[:, i * d + n0:i * d + n0 + MERGE_COL_BLOCK])
            term = jax.nn.sigmoid(gate) * _dot(y_ref[...], wb_ref[i * SEG_W:(i + 1) * SEG_W, cols])
            acc = term if acc is None else acc + term
        parts.append(acc.astype(BF16))
    o_ref[...] = x + g1_ref[...] * _dot(jnp.concatenate(parts, axis=1), wo_ref[...])


def _merge(x, shift, scale, g1, norm_g, ys, w_gate, w_branch, w_out, layer):
    b, l, d = x.shape
    tm = min(ROW_TILE, l)
    batched = shift.shape[0] != 1
    mod_map = (lambda bi, i: (bi, 0, 0)) if batched else (lambda bi, i: (0, 0, 0))
    x_tile = pl.BlockSpec((None, tm, d), lambda bi, i: (bi, i, 0))
    y_tile = pl.BlockSpec((None, tm, SEG_W), lambda bi, i: (bi, i, 0))
    mod = pl.BlockSpec((None, 1, d), mod_map)
    weights = sum(math.prod(w.shape[1:]) for w in (w_gate, w_branch, w_out)) * 2
    vmem = weights + 2 * (2 * tm * d * 4 + 4 * tm * SEG_W * 2) + 8 * tm * d * 4 + (8 << 20)
    return pl.pallas_call(
        _merge_kernel,
        grid=(b, l // tm),
        in_specs=[x_tile, mod, mod, mod, _resident((1, d)), y_tile, y_tile, y_tile, y_tile,
                  _resident(w_gate.shape[1:], layer), _resident(w_branch.shape[1:], layer),
                  _resident(w_out.shape[1:], layer)],
        out_specs=x_tile,
        out_shape=jax.ShapeDtypeStruct(x.shape, F32),
        compiler_params=_params(2, vmem),
        name="merge",
    )(x, shift, scale, g1, norm_g, *ys, w_gate, w_branch, w_out)


def _ffn_kernel(x_ref, sh_ref, sc_ref, g2_ref, ng_ref, wgu_ref, wd_ref, o_ref, u_ref):
    x = x_ref[...]
    hb = _norm_modulate(x, ng_ref[...], sh_ref[...], sc_ref[...])
    hidden = wd_ref.shape[0]
    for c in range(hidden // FFN_CHUNK):
        cols = slice(c * FFN_CHUNK, (c + 1) * FFN_CHUNK)
        a = _dot(hb, wgu_ref[:, cols])
        bb = _dot(hb, wgu_ref[:, hidden + c * FFN_CHUNK:hidden + (c + 1) * FFN_CHUNK])
        u_ref[:, cols] = (a * jax.nn.sigmoid(a) * bb).astype(BF16)
    o_ref[...] = x + g2_ref[...] * _dot(u_ref[...], wd_ref[...])


def _ffn(x, shift, scale, g2, norm_g, w_gu, w_down, layer):
    b, l, d = x.shape
    tm = min(ROW_TILE, l)
    hidden = w_down.shape[1]
    batched = shift.shape[0] != 1
    mod_map = (lambda bi, i: (bi, 0, 0)) if batched else (lambda bi, i: (0, 0, 0))
    x_tile = pl.BlockSpec((None, tm, d), lambda bi, i: (bi, i, 0))
    mod = pl.BlockSpec((None, 1, d), mod_map)
    vmem = (math.prod(w_gu.shape[1:]) + math.prod(w_down.shape[1:])) * 2 + 4 * tm * d * 4 + tm * hidden * 2 + 8 * tm * d * 4 + (8 << 20)
    return pl.pallas_call(
        _ffn_kernel,
        grid=(b, l // tm),
        in_specs=[x_tile, mod, mod, mod, _resident((1, d)),
                  _resident(w_gu.shape[1:], layer), _resident(w_down.shape[1:], layer)],
        out_specs=x_tile,
        out_shape=jax.ShapeDtypeStruct(x.shape, F32),
        scratch_shapes=[pltpu.VMEM((tm, hidden), BF16)],
        compiler_params=_params(2, vmem),
        name="ffn",
    )(x, shift, scale, g2, norm_g, w_gu, w_down)


def _rope_tables(n_tok):
    nf = A_DH // 4
    t = np.arange(n_tok)
    row = (t // GRID_W).astype(np.float32)
    col = (t % GRID_W).astype(np.float32)
    inv = np.float32(ROPE_THETA) ** (-np.arange(nf, dtype=np.float32) / np.float32(nf))
    ar = row[:, None] * inv
    ac = col[:, None] * inv
    ang = np.concatenate([ar, ar, ac, ac], axis=-1).astype(np.float64)
    cos, sin = np.cos(ang), np.sin(ang)
    quarter = (np.arange(A_DH) // nf) % 2
    sin_up = np.where(quarter == 0, -sin, 0.0)
    sin_dn = np.where(quarter == 1, sin, 0.0)
    rep = V7X_LANES // A_DH
    return tuple(jnp.asarray(np.tile(a, (1, rep)), dtype=F32) for a in (cos, sin_up, sin_dn))


def kernel(x, c, ctx, c_ctx, w_mod, b_mod, norm1_g, w_in, a_qk_g, a_lambda, a_subln_g, b_pool_w,
           b_pool_s, c_qk_g, c_rpb, d_vn_g, d_ws, d_bs, w_branch, w_out, norm2_g, w_gu, w_down):
    b, s, d = x.shape
    depth = w_mod.shape[0]
    rope_tabs = _rope_tables(s)

    rows = -(-(b + 1) // 8) * 8
    c_all = jnp.zeros((rows, d), F32).at[:b].set(c).at[b].set(c_ctx)
    mod = _modulation(c_all, w_mod, b_mod)

    w_mix = w_in[:, :, :MIX_COLS].astype(BF16)
    w_gate = w_in[:, :, MIX_COLS:].astype(BF16)
    w_br = w_branch.astype(BF16)
    w_o = w_out.astype(BF16)
    w_gu_b = w_gu.astype(BF16)
    w_dn = w_down.astype(BF16)
    w_pool = b_pool_w.astype(BF16)
    w_s = d_ws.astype(BF16)

    for l in range(depth):
        last = l == depth - 1
        lam_init = 0.8 - 0.6 * math.exp(-0.3 * l)
        mx = [mod[l, :b, k * d:(k + 1) * d].reshape(b, 1, d) for k in range(6)]
        mc = [mod[l, b:b + 1, k * d:(k + 1) * d].reshape(1, 1, d) for k in range(6)]
        n1 = norm1_g[l].reshape(1, d)
        n2 = norm2_g[l].reshape(1, d)
        rep = SEG_W // A_DH
        qk_gains = jnp.stack([jnp.tile(a_qk_g[l, 0], rep), jnp.tile(a_qk_g[l, 1], rep),
                              jnp.tile(c_qk_g[l, 0], rep), jnp.tile(c_qk_g[l, 1], rep)])
        vn_g = d_vn_g[l].reshape(1, SEG_W)
        bs_full = jnp.repeat(d_bs[l].T, V7X_LANES, axis=1)
        bias_t = _nbr_bias_table(c_rpb[l])

        px = _in_proj(x, mx[0], mx[1], n1, w_mix, qk_gains, vn_g, w_s, bs_full, rope_tabs, MIXER_SEGS, l)
        lc = ctx.shape[1]
        flat = lambda t: t.reshape(1, b * lc, t.shape[-1])
        pc = _in_proj(flat(ctx), mc[0], mc[1], n1, w_mix, qk_gains, vn_g, w_s, bs_full, None,
                      CTX_KV_SEGS if last else MIXER_SEGS, l)
        pc = {seg: t.reshape(b, lc, SEG_W) for seg, t in pc.items()}

        ya = _diff_attn(px[SEG_AQ], [(px[SEG_AK], px[SEG_AV]), (pc[SEG_AK], pc[SEG_AV])],
                        a_lambda[l], a_subln_g[l], a_qk_g[l], lam_init, heads_per_step=1)
        yb = _pool(px[SEG_B], w_pool, b_pool_s[l], l)
        yc = _nbr_attn(px[SEG_CQ], px[SEG_CK], px[SEG_CV], pc[SEG_CK], pc[SEG_CV], bias_t, c_qk_g[l],
                       pairs_per_step=1)
        x = _merge(x, mx[0], mx[1], mx[2], n1, (ya, yb, yc, px[SEG_YD]), w_gate, w_br, w_o, l)
        x = _ffn(x, mx[3], mx[4], mx[5], n2, w_gu_b, w_dn, l)

        if not last:
            ya_c = _diff_attn(pc[SEG_AQ], [(pc[SEG_AK], pc[SEG_AV])], a_lambda[l], a_subln_g[l], a_qk_g[l], lam_init,
                              heads_per_step=A_HEADS)
            yb_c = _pool(pc[SEG_B], w_pool, b_pool_s[l], l)
            yc_c = _nbr_attn(pc[SEG_CQ], None, None, pc[SEG_CK], pc[SEG_CV], None, c_qk_g[l],
                             pairs_per_step=SEG_W // V7X_LANES)
            ys_c = tuple(flat(t) for t in (ya_c, yb_c, yc_c, pc[SEG_YD]))
            ctx_f = _merge(flat(ctx), mc[0], mc[1], mc[2], n1, ys_c, w_gate, w_br, w_o, l)
            ctx = _ffn(ctx_f, mc[3], mc[4], mc[5], n2, w_gu_b, w_dn, l).reshape(b, lc, d)
    return x
```
